```python
import math
import jax
import jax.numpy as jnp
from jax import lax
import numpy as np

D_MODEL = 2048
BATCH = 1
SEQ = 16384
DEPTH = 1
DEC_BATCH = 16
DEC_SEQ = 16
PAST_LEN = 1024

CHUNK = 64
Q_BLOCK = 128
D_MLSTM = D_MODEL // 2
N_MLSTM_HEADS = 4
HD_MLSTM = D_MLSTM // N_MLSTM_HEADS
D_SB = D_MODEL // 2
N_SB_HEADS = 8
HD_SB = D_SB // N_SB_HEADS
CONV_W = 4
N_GROUPS = 4
N_EXP_PER_GROUP = 8
N_EXPERTS = N_GROUPS * N_EXP_PER_GROUP
TOP_K_IN_GROUP = 2
D_EXPERT = D_MODEL // 4
LN_EPS = 1e-5
DEEPNORM_ALPHA = (2.0 * DEPTH) ** 0.25
DEEPNORM_BETA = (8.0 * DEPTH) ** -0.25
NEG = -1e30

IN_SIZES = (2 * D_MLSTM, D_MLSTM, D_MLSTM, 2 * N_MLSTM_HEADS, D_SB, D_SB, D_SB, D_MODEL, D_MODEL)
IN_SPLITS = tuple(int(s) for s in np.cumsum(IN_SIZES)[:-1])
D_IN = int(sum(IN_SIZES))
V_COLUMN_GROUPS = (1, 6)

kernel_name = 'hybrid_mlstm_stickbreaking_hmoe_step'


def _layer_norm(x, g, b):
    xf = x.astype(jnp.float32)
    mu = xf.mean(-1, keepdims=True)
    var = jnp.square(xf - mu).mean(-1, keepdims=True)
    y = (xf - mu) * lax.rsqrt(var + LN_EPS) * g.astype(jnp.float32) + b.astype(jnp.float32)
    return y.astype(x.dtype)


def _heads(a, n_heads):
    b, t, _ = a.shape
    return a.reshape(b, t, n_heads, -1).transpose(0, 2, 1, 3)


def _merge_heads(a):
    b, h, t, d = a.shape
    return a.transpose(0, 2, 1, 3).reshape(b, t, h * d)


def _causal_dwconv(u_new, conv_state, w, bias):
    t = u_new.shape[1]
    u = jnp.concatenate([conv_state.astype(u_new.dtype), u_new], axis=1)
    out = bias + u[:, 0:t] * w[0]
    for j in range(1, CONV_W):
        out = out + u[:, j:j + t] * w[j]
    return out, u[:, -(CONV_W - 1):]


def _mlstm_chunkwise(q, k, v, i_pre, log_f, c0, n0, m0, chunk):
    b, h, t, d = q.shape
    nc = t // chunk

    def to_chunks(a):
        a = a.reshape(a.shape[:2] + (nc, chunk) + a.shape[3:])
        return jnp.moveaxis(a, 2, 0)

    causal = jnp.tril(jnp.ones((chunk, chunk), dtype=bool))

    def step(carry, xs):
        c_prev, n_prev, m_prev = carry
        qc, kc, vc, ic, fc = xs
        bcum = jnp.cumsum(fc, axis=-1)
        dmat = jnp.where(causal, bcum[..., :, None] - bcum[..., None, :] + ic[..., None, :], NEG)
        inter = bcum + m_prev[..., None]
        m_t = jnp.maximum(dmat.max(-1), inter)
        w = jnp.exp(dmat - m_t[..., None])
        s = jnp.einsum('bhtd,bhsd->bhts', qc, kc) * w
        e_inter = jnp.exp(inter - m_t)
        num = jnp.einsum('bhts,bhsd->bhtd', s, vc) + e_inter[..., None] * jnp.einsum('bhtd,bhde->bhte', qc, c_prev)
        den = s.sum(-1) + e_inter * jnp.einsum('bhtd,bhd->bht', qc, n_prev)
        hc = num / jnp.maximum(jnp.abs(den), jnp.exp(-m_t))[..., None]
        g_last = bcum[..., -1]
        w_last = g_last[..., None] - bcum + ic
        m_new = jnp.maximum(g_last + m_prev, w_last.max(-1))
        decay = jnp.exp(g_last + m_prev - m_new)
        ws = jnp.exp(w_last - m_new[..., None])
        c_new = decay[..., None, None] * c_prev + jnp.einsum('bhsd,bhse->bhde', ws[..., None] * kc, vc)
        n_new = decay[..., None] * n_prev + jnp.einsum('bhs,bhsd->bhd', ws, kc)
        return (c_new, n_new, m_new), hc

    xs = (to_chunks(q), to_chunks(k), to_chunks(v), to_chunks(i_pre), to_chunks(log_f))
    (c_f, n_f, m_f), hs = lax.scan(step, (c0, n0, m0), xs)
    hout = jnp.moveaxis(hs, 0, 2).reshape(b, h, t, d)
    return hout, c_f, n_f, m_f


def _stick_breaking(q, k, v, q_pos, k_pos):
    z = jnp.einsum('bhqd,bhkd->bhqk', q, k).astype(jnp.float32) * (HD_SB ** -0.5)
    mask = k_pos[None, :] < q_pos[:, None]
    log_keep = jnp.where(mask, jax.nn.log_sigmoid(-z), 0.0)
    later = lax.cumsum(log_keep, axis=3, reverse=True) - log_keep
    a = jnp.where(mask, jnp.exp(jax.nn.log_sigmoid(z) + later), 0.0)
    return jnp.einsum('bhqk,bhkd->bhqd', a.astype(v.dtype), v)


def _stick_breaking_blocked(q, k, v):
    b, h, t, d = q.shape
    nb = t // Q_BLOCK
    q_blocks = jnp.moveaxis(q.reshape(b, h, nb, Q_BLOCK, d), 2, 0)
    pos = jnp.arange(t, dtype=jnp.int32)
    pos_blocks = pos.reshape(nb, Q_BLOCK)
    out = lax.map(lambda a: _stick_breaking(a[0], k, v, a[1], pos), (q_blocks, pos_blocks))
    return jnp.moveaxis(out, 0, 2).reshape(b, h, t, d)


def _mixer(x, conv_state, c0, n0, m0, past_k, past_v, w_in, b_gate, conv_w, conv_b, w_branch_m, w_branch_s, w_out):
    f32 = jnp.float32
    t = x.shape[1]
    proj = x @ w_in
    qk_raw, v_m, o_m, if_pre, q_s, k_s, v_s, g_m, g_s = jnp.split(proj, IN_SPLITS, axis=-1)
    qk, new_conv = _causal_dwconv(qk_raw, conv_state, conv_w, conv_b)
    qk = jax.nn.silu(qk)
    q_m = _heads(qk[..., :D_MLSTM], N_MLSTM_HEADS).astype(f32)
    k_m = _heads(qk[..., D_MLSTM:], N_MLSTM_HEADS).astype(f32) * (HD_MLSTM ** -0.5)
    v_mh = _heads(v_m, N_MLSTM_HEADS).astype(f32)
    gates = (if_pre.astype(f32) + b_gate.astype(f32)).transpose(0, 2, 1)
    i_pre = gates[:, :N_MLSTM_HEADS]
    log_f = jax.nn.log_sigmoid(gates[:, N_MLSTM_HEADS:])
    h_a, c_new, n_new, m_new = _mlstm_chunkwise(q_m, k_m, v_mh, i_pre, log_f, c0.astype(f32), n0.astype(f32), m0.astype(f32), min(CHUNK, t))
    h_m = jax.nn.sigmoid(o_m) * _merge_heads(h_a).astype(x.dtype)
    q_sh = _heads(q_s, N_SB_HEADS)
    k_sh = _heads(k_s, N_SB_HEADS)
    v_sh = _heads(v_s, N_SB_HEADS)
    if past_k is None:
        h_b = _stick_breaking_blocked(q_sh, k_sh, v_sh)
    else:
        past_len = past_k.shape[2]
        k_all = jnp.concatenate([past_k.astype(k_sh.dtype), k_sh], axis=2)
        v_all = jnp.concatenate([past_v.astype(v_sh.dtype), v_sh], axis=2)
        h_b = _stick_breaking(q_sh, k_all, v_all, past_len + jnp.arange(t, dtype=jnp.int32), jnp.arange(past_len + t, dtype=jnp.int32))
    h_s = _merge_heads(h_b)
    merged = jax.nn.sigmoid(g_m) * (h_m @ w_branch_m) + jax.nn.sigmoid(g_s) * (h_s @ w_branch_s)
    return merged @ w_out, (k_sh, v_sh, c_new, n_new, m_new, new_conv)


def _hier_moe(x, w_rg, b_rg, w_re, b_re, w_eg, w_eu, w_ed):
    f32 = jnp.float32
    b, t, d = x.shape
    xf = x.reshape(b * t, d)
    lg = (xf @ w_rg).astype(f32) + b_rg.astype(f32)
    pg = jax.nn.softmax(lg, axis=-1)
    grp = jnp.argmax(lg, axis=-1)
    p_sel = jnp.take_along_axis(pg, grp[:, None], axis=1)[:, 0]
    le = ((xf @ w_re).astype(f32) + b_re.astype(f32)).reshape(-1, N_GROUPS, N_EXP_PER_GROUP)
    le_sel = jnp.take_along_axis(le, grp[:, None, None], axis=1)[:, 0]
    top_v, top_i = lax.top_k(le_sel, TOP_K_IN_GROUP)
    w_top = jax.nn.softmax(top_v, axis=-1) * p_sel[:, None]
    within = jnp.einsum('nk,nke->ne', w_top, jax.nn.one_hot(top_i, N_EXP_PER_GROUP, dtype=f32))
    combine = (jax.nn.one_hot(grp, N_GROUPS, dtype=f32)[:, :, None] * within[:, None, :]).reshape(-1, N_EXPERTS)

    def expert(acc, e):
        wg, wu, wd, c = e
        hid = jax.nn.silu(xf @ wg) * (xf @ wu)
        return acc + c[:, None].astype(xf.dtype) * (hid @ wd), None

    y, _ = lax.scan(expert, jnp.zeros_like(xf), (w_eg, w_eu, w_ed, combine.T))
    return y.reshape(b, t, d)


def _layer(x, state, past_k, past_v, p):
    (w_in, b_gate, conv_w, conv_b, w_bm, w_bs, w_out, ln1_g, ln1_b,
     w_rg, b_rg, w_re, b_re, w_eg, w_eu, w_ed, ln2_g, ln2_b) = p
    conv_state, c0, n0, m0 = state
    y_mix, new_state = _mixer(x, conv_state, c0, n0, m0, past_k, past_v, w_in, b_gate, conv_w, conv_b, w_bm, w_bs, w_out)
    x1 = _layer_norm(DEEPNORM_ALPHA * x + y_mix, ln1_g, ln1_b)
    x2 = _layer_norm(DEEPNORM_ALPHA * x1 + _hier_moe(x1, w_rg, b_rg, w_re, b_re, w_eg, w_eu, w_ed), ln2_g, ln2_b)
    return x2, new_state


def _stack(states, i):
    return jnp.stack([s[i] for s in states], axis=0)


def setup_inputs(seed: int = 0) -> dict:
    key = jax.random.key(seed)
    ks = jax.random.split(key, 32)
    f32 = jnp.float32

    def nrm(k, shape, scale=1.0):
        return jax.random.normal(k, shape, f32) * scale

    col_scale = jnp.concatenate([jnp.full((s,), DEEPNORM_BETA if i in V_COLUMN_GROUPS else 1.0, f32) for i, s in enumerate(IN_SIZES)])
    f_bias = jnp.linspace(3.0, 6.0, N_MLSTM_HEADS, dtype=f32)
    b_gate = jnp.concatenate([nrm(ks[10], (DEPTH, N_MLSTM_HEADS), 0.1), f_bias[None, :] + nrm(ks[11], (DEPTH, N_MLSTM_HEADS), 0.1)], axis=-1)
    return {
        'x_prompt': nrm(ks[0], (BATCH, SEQ, D_MODEL)),
        'x_sample': nrm(ks[1], (DEC_BATCH, DEC_SEQ, D_MODEL)),
        'cache_sb_k': nrm(ks[2], (DEPTH, DEC_BATCH, N_SB_HEADS, PAST_LEN, HD_SB)),
        'cache_sb_v': nrm(ks[3], (DEPTH, DEC_BATCH, N_SB_HEADS, PAST_LEN, HD_SB), DEEPNORM_BETA),
        'state_mlstm_C': nrm(ks[4], (DEPTH, DEC_BATCH, N_MLSTM_HEADS, HD_MLSTM, HD_MLSTM), 0.1),
        'state_mlstm_n': nrm(ks[5], (DEPTH, DEC_BATCH, N_MLSTM_HEADS, HD_MLSTM), 0.1),
        'state_mlstm_m': nrm(ks[6], (DEPTH, DEC_BATCH, N_MLSTM_HEADS)),
        'state_conv': nrm(ks[7], (DEPTH, DEC_BATCH, CONV_W - 1, 2 * D_MLSTM)),
        'w_in': nrm(ks[8], (DEPTH, D_MODEL, D_IN), D_MODEL ** -0.5) * col_scale,
        'b_gate': b_gate,
        'conv_w': nrm(ks[12], (DEPTH, CONV_W, 2 * D_MLSTM), CONV_W ** -0.5),
        'conv_b': nrm(ks[13], (DEPTH, 2 * D_MLSTM), 0.01),
        'w_branch_m': nrm(ks[14], (DEPTH, D_MLSTM, D_MODEL), D_MLSTM ** -0.5),
        'w_branch_s': nrm(ks[15], (DEPTH, D_SB, D_MODEL), D_SB ** -0.5),
        'w_out': nrm(ks[16], (DEPTH, D_MODEL, D_MODEL), D_MODEL ** -0.5 * DEEPNORM_BETA),
        'ln1_g': 1.0 + nrm(ks[17], (DEPTH, D_MODEL), 0.01),
        'ln1_b': nrm(ks[18], (DEPTH, D_MODEL), 0.01),
        'w_router_group': nrm(ks[19], (DEPTH, D_MODEL, N_GROUPS), D_MODEL ** -0.5),
        'b_router_group': nrm(ks[20], (DEPTH, N_GROUPS), 0.01),
        'w_router_expert': nrm(ks[21], (DEPTH, D_MODEL, N_EXPERTS), D_MODEL ** -0.5),
        'b_router_expert': nrm(ks[22], (DEPTH, N_EXPERTS), 0.01),
        'w_exp_gate': nrm(ks[23], (DEPTH, N_EXPERTS, D_MODEL, D_EXPERT), D_MODEL ** -0.5),
        'w_exp_up': nrm(ks[24], (DEPTH, N_EXPERTS, D_MODEL, D_EXPERT), D_MODEL ** -0.5),
        'w_exp_down': nrm(ks[25], (DEPTH, N_EXPERTS, D_EXPERT, D_MODEL), D_EXPERT ** -0.5 * DEEPNORM_BETA),
        'ln2_g': 1.0 + nrm(ks[26], (DEPTH, D_MODEL), 0.01),
        'ln2_b': nrm(ks[27], (DEPTH, D_MODEL), 0.01),
    }


def reference(x_prompt, x_sample, cache_sb_k, cache_sb_v, state_mlstm_C, state_mlstm_n, state_mlstm_m, state_conv,
              w_in, b_gate, conv_w, conv_b, w_branch_m, w_branch_s, w_out, ln1_g, ln1_b,
              w_router_group, b_router_group, w_router_expert, b_router_expert,
              w_exp_gate, w_exp_up, w_exp_down, ln2_g, ln2_b):
    f32 = jnp.float32
    params = (w_in, b_gate, conv_w, conv_b, w_branch_m, w_branch_s, w_out, ln1_g, ln1_b,
              w_router_group, b_router_group, w_router_expert, b_router_expert,
              w_exp_gate, w_exp_up, w_exp_down, ln2_g, ln2_b)
    bp = x_prompt.shape[0]
    xp = x_prompt
    xs = x_sample
    st_p_all = []
    st_s_all = []
    for l in range(DEPTH):
        p = tuple(a[l] for a in params)
        init_p = (jnp.zeros((bp, CONV_W - 1, 2 * D_MLSTM), x_prompt.dtype),
                  jnp.zeros((bp, N_MLSTM_HEADS, HD_MLSTM, HD_MLSTM), f32),
                  jnp.zeros((bp, N_MLSTM_HEADS, HD_MLSTM), f32),
                  jnp.full((bp, N_MLSTM_HEADS), NEG, f32))
        xp, st_p = _layer(xp, init_p, None, None, p)
        xs, st_s = _layer(xs, (state_conv[l], state_mlstm_C[l], state_mlstm_n[l], state_mlstm_m[l]), cache_sb_k[l], cache_sb_v[l], p)
        st_p_all.append(st_p)
        st_s_all.append(st_s)
    return (xp, xs,
            _stack(st_p_all, 0), _stack(st_p_all, 1), _stack(st_p_all, 2), _stack(st_p_all, 3), _stack(st_p_all, 4), _stack(st_p_all, 5),
            _stack(st_s_all, 0), _stack(st_s_all, 1), _stack(st_s_all, 2), _stack(st_s_all, 3), _stack(st_s_all, 4), _stack(st_s_all, 5))
```

```python
import functools

import jax
import jax.numpy as jnp
from jax import lax
from jax.experimental import pallas as pl
from jax.experimental.pallas import tpu as pltpu

F32 = jnp.float32
BF16 = jnp.bfloat16
HIGHEST = lax.Precision.HIGHEST

LANES = 128
SUBLANES = 8
LN_EPS = 1e-5
NEG = -1e30
CONV_W = 4
TOP_K_IN_GROUP = 2
EXP_ZERO_BELOW = -104.0
MIB = 1024 * 1024

_NT = (((1,), (1,)), ((), ()))
_TN = (((0,), (0,)), ((), ()))


def _tile(n, pref):
    if n <= pref:
        return n
    t = pref
    while t >= SUBLANES:
        if n % t == 0 and t % SUBLANES == 0:
            return t
        t -= SUBLANES
    return n


def _log_sigmoid(x):
    return jnp.minimum(x, 0.0) - jnp.log1p(jnp.exp(-jnp.abs(x)))


def _layer_norm(x, g, b):
    mu = jnp.mean(x, axis=-1, keepdims=True)
    d = x - mu
    var = jnp.mean(d * d, axis=-1, keepdims=True)
    return d * lax.rsqrt(var + LN_EPS) * g + b


def _proj_kernel(x_ref, w_ref, o_ref, xb_ref, *, head_major):
    @pl.when(pl.program_id(1) == 0)
    def _():
        xb_ref[...] = x_ref[...].astype(BF16)

    acc = jnp.dot(xb_ref[...], w_ref[...], preferred_element_type=F32)
    if head_major:
        for h in range(o_ref.shape[0]):
            o_ref[h] = acc[:, h * LANES:(h + 1) * LANES].astype(o_ref.dtype)
    else:
        o_ref[...] = acc.astype(o_ref.dtype)


def _project(x, w, out_dtype, head_major=False):
    n, k = x.shape
    c = w.shape[1]
    tm = _tile(n, 1024)
    tn = _tile(c, 512) if c % LANES == 0 else c
    if head_major:
        out_shape = jax.ShapeDtypeStruct((c // LANES, n, LANES), out_dtype)
        out_spec = pl.BlockSpec((tn // LANES, tm, LANES), lambda i, j: (j, i, 0))
    else:
        out_shape = jax.ShapeDtypeStruct((n, c), out_dtype)
        out_spec = pl.BlockSpec((tm, tn), lambda i, j: (i, j))
    return pl.pallas_call(
        functools.partial(_proj_kernel, head_major=head_major),
        grid=(n // tm, c // tn),
        in_specs=[pl.BlockSpec((tm, k), lambda i, j: (i, 0)),
                  pl.BlockSpec((k, tn), lambda i, j: (0, j))],
        out_specs=out_spec,
        out_shape=out_shape,
        scratch_shapes=[pltpu.VMEM((tm, k), BF16)],
        compiler_params=pltpu.CompilerParams(
            dimension_semantics=("parallel", "arbitrary"),
            vmem_limit_bytes=48 * MIB),
        name="in_proj",
    )(x, w)


def _mlstm_kernel(q_ref, k_ref, v_ref, g_ref, bg_ref, cwq_ref, cwk_ref, cbq_ref, cbk_ref,
                  csq_ref, csk_ref, c0_ref, n0_ref, m0_ref,
                  h_ref, c_ref, n_ref, m_ref, qext_ref, kext_ref, *, chunk, n_heads, k_scale):
    head = pl.program_id(1)
    step = pl.program_id(2)
    L = chunk
    halo = CONV_W - 1
    base = SUBLANES - halo

    @pl.when(step == 0)
    def _():
        qext_ref[base:SUBLANES, :] = csq_ref[0]
        kext_ref[base:SUBLANES, :] = csk_ref[0]
        c_ref[...] = c0_ref[...]
        n_ref[...] = n0_ref[...]
        m_ref[...] = m0_ref[...]

    def conv_silu(raw_ref, ext_ref, cw_ref, cb_ref):
        ext_ref[SUBLANES:SUBLANES + L, :] = raw_ref[...]
        acc = cb_ref[...] + ext_ref[base:base + L, :] * cw_ref[0:1, :]
        for j in range(1, CONV_W):
            acc = acc + ext_ref[base + j:base + j + L, :] * cw_ref[j:j + 1, :]
        ext_ref[base:SUBLANES, :] = ext_ref[base + L:SUBLANES + L, :]
        return acc * jax.nn.sigmoid(acc)

    q = conv_silu(q_ref, qext_ref, cwq_ref, cbq_ref)
    k = conv_silu(k_ref, kext_ref, cwk_ref, cbk_ref) * k_scale
    q_b = q.astype(BF16)
    k_b = k.astype(BF16)
    v_b = v_ref[...].astype(BF16)

    lane = lax.broadcasted_iota(jnp.int32, (L, LANES), 1)
    pre = g_ref[...] + bg_ref[...]
    gates = jnp.where(lane < n_heads, pre, _log_sigmoid(pre))
    row_i = lax.broadcasted_iota(jnp.int32, (L, L), 0)
    col_i = lax.broadcasted_iota(jnp.int32, (L, L), 1)
    causal = col_i <= row_i
    tril = causal.astype(F32)
    triu = (row_i <= col_i).astype(F32)
    sel = (lax.broadcasted_iota(jnp.int32, (SUBLANES, LANES), 0)
           == lax.broadcasted_iota(jnp.int32, (SUBLANES, LANES), 1)).astype(F32)
    cum_cols = jnp.dot(tril, gates, precision=HIGHEST, preferred_element_type=F32)
    gates_t = lax.dot_general(sel, gates, _NT, precision=HIGHEST, preferred_element_type=F32)
    cum_rows = jnp.dot(gates_t, triu, precision=HIGHEST, preferred_element_type=F32)

    sub = lax.broadcasted_iota(jnp.int32, (SUBLANES, L), 0)
    i_col = jnp.sum(jnp.where(lane == head, gates, 0.0), axis=1, keepdims=True)
    b_col = jnp.sum(jnp.where(lane == head + n_heads, cum_cols, 0.0), axis=1, keepdims=True)
    i_row = jnp.sum(jnp.where(sub == head, gates_t, 0.0), axis=0, keepdims=True)
    b_row = jnp.sum(jnp.where(sub == head + n_heads, cum_rows, 0.0), axis=0, keepdims=True)

    c_prev = c_ref[0, 0]
    n_prev = n_ref[0, 0]
    m_prev = m_ref[0, 0][:, 0:1]

    dmat = jnp.where(causal, b_col - b_row + i_row, NEG)
    inter = b_col + m_prev
    m_t = jnp.maximum(jnp.max(dmat, axis=-1, keepdims=True), inter)
    w = jnp.exp(dmat - m_t)
    s = lax.dot_general(q_b, k_b, _NT, preferred_element_type=F32) * w
    e_inter = jnp.exp(inter - m_t)
    num = (jnp.dot(s.astype(BF16), v_b, preferred_element_type=F32)
           + e_inter * jnp.dot(q_b, c_prev.astype(BF16), preferred_element_type=F32))
    den = jnp.sum(s, axis=-1, keepdims=True) + e_inter * jnp.sum(q * n_prev, axis=-1, keepdims=True)
    h_ref[...] = num / jnp.maximum(jnp.abs(den), jnp.exp(-m_t))

    g_last = b_col[L - 1:L, :]
    w_last = g_last - b_col + i_col
    m_new = jnp.maximum(g_last + m_prev, jnp.max(w_last, axis=0, keepdims=True))
    decay = jnp.exp(g_last + m_prev - m_new)
    kw = jnp.exp(w_last - m_new) * k
    c_ref[0, 0] = decay * c_prev + lax.dot_general(kw.astype(BF16), v_b, _TN, preferred_element_type=F32)
    n_ref[0, 0] = decay * n_prev + jnp.sum(kw, axis=0, keepdims=True)
    m_ref[0, 0] = jnp.broadcast_to(m_new, (1, LANES))


def _mlstm(proj, gates, b_gate_row, conv_w, conv_b, conv_state, c0, n0, m0, *, batch, seq, chunk,
           n_heads, head_dim, q_col, k_col, v_col):
    n_tok = batch * seq
    nc = seq // chunk
    d = head_dim
    qb, kb, vb = q_col // d, k_col // d, v_col // d
    kcw = (k_col - q_col) // d
    row = lambda b, h, c: b * nc + c
    n0r = n0.reshape(batch, n_heads, 1, d)
    m0r = jnp.broadcast_to(m0.reshape(batch, n_heads, 1, 1), (batch, n_heads, 1, LANES))
    cbr = conv_b.reshape(1, -1)
    state_spec = lambda shape: pl.BlockSpec(shape, lambda b, h, c: (b, h, 0, 0))
    h, c_new, n_new, m_new = pl.pallas_call(
        functools.partial(_mlstm_kernel, chunk=chunk, n_heads=n_heads, k_scale=float(d) ** -0.5),
        grid=(batch, n_heads, nc),
        in_specs=[
            pl.BlockSpec((chunk, d), lambda b, h, c: (row(b, h, c), qb + h)),
            pl.BlockSpec((chunk, d), lambda b, h, c: (row(b, h, c), kb + h)),
            pl.BlockSpec((chunk, d), lambda b, h, c: (row(b, h, c), vb + h)),
            pl.BlockSpec((chunk, LANES), lambda b, h, c: (row(b, h, c), 0)),
            pl.BlockSpec((1, LANES), lambda b, h, c: (0, 0)),
            pl.BlockSpec((CONV_W, d), lambda b, h, c: (0, h)),
            pl.BlockSpec((CONV_W, d), lambda b, h, c: (0, kcw + h)),
            pl.BlockSpec((1, d), lambda b, h, c: (0, h)),
            pl.BlockSpec((1, d), lambda b, h, c: (0, kcw + h)),
            pl.BlockSpec((1, CONV_W - 1, d), lambda b, h, c: (b, 0, h)),
            pl.BlockSpec((1, CONV_W - 1, d), lambda b, h, c: (b, 0, kcw + h)),
            state_spec((1, 1, d, d)),
            state_spec((1, 1, 1, d)),
            state_spec((1, 1, 1, LANES)),
        ],
        out_specs=[
            pl.BlockSpec((chunk, d), lambda b, h, c: (row(b, h, c), h)),
            state_spec((1, 1, d, d)),
            state_spec((1, 1, 1, d)),
            state_spec((1, 1, 1, LANES)),
        ],
        out_shape=[
            jax.ShapeDtypeStruct((n_tok, n_heads * d), F32),
            jax.ShapeDtypeStruct((batch, n_heads, d, d), F32),
            jax.ShapeDtypeStruct((batch, n_heads, 1, d), F32),
            jax.ShapeDtypeStruct((batch, n_heads, 1, LANES), F32),
        ],
        scratch_shapes=[pltpu.VMEM((chunk + SUBLANES, d), F32), pltpu.VMEM((chunk + SUBLANES, d), F32)],
        compiler_params=pltpu.CompilerParams(
            dimension_semantics=("parallel", "parallel", "arbitrary"),
            vmem_limit_bytes=32 * MIB),
        name="mlstm",
    )(proj, proj, proj, gates, b_gate_row, conv_w, conv_w, cbr, cbr, conv_state, conv_state, c0, n0r, m0r)
    return h, c_new, n_new.reshape(batch, n_heads, d), m_new[:, :, 0, 0]


def _sb_block(q_b, k_blk, v_blk, carry, acc, *, scale, diagonal):
    tq, tk = q_b.shape[0], k_blk.shape[0]
    z = lax.dot_general(q_b, k_blk.astype(BF16), _NT, preferred_element_type=F32) * scale
    softplus = jnp.maximum(z, 0.0) + jnp.log1p(jnp.exp(-jnp.abs(z)))
    log_keep = -softplus
    log_beta = z - softplus
    if diagonal:
        mask = (lax.broadcasted_iota(jnp.int32, (tq, tk), 1) < lax.broadcasted_iota(jnp.int32, (tq, tk), 0))
        log_keep = jnp.where(mask, log_keep, 0.0)
    later_mat = (lax.broadcasted_iota(jnp.int32, (tk, tk), 0)
                 > lax.broadcasted_iota(jnp.int32, (tk, tk), 1)).astype(BF16)
    hi = log_keep.astype(BF16)
    rem = log_keep - hi.astype(F32)
    mid = rem.astype(BF16)
    lo = (rem - mid.astype(F32)).astype(BF16)
    later = (jnp.dot(hi, later_mat, preferred_element_type=F32)
             + jnp.dot(mid, later_mat, preferred_element_type=F32)
             + jnp.dot(lo, later_mat, preferred_element_type=F32)) + carry
    a = jnp.exp(log_beta + later)
    if diagonal:
        a = jnp.where(mask, a, 0.0)
    acc = acc + jnp.dot(a.astype(BF16), v_blk.astype(BF16), preferred_element_type=F32)
    carry = carry + jnp.sum(log_keep, axis=-1, keepdims=True)
    return carry, acc


def _sb_sweep(q_b, k_ref_at, v_ref_at, first_block, carry, acc, *, tk, scale):
    def cond(state):
        j, cmax, _, _ = state
        return jnp.logical_and(j >= 0, cmax > EXP_ZERO_BELOW)

    def body(state):
        j, _, carry, acc = state
        start = pl.multiple_of(j * tk, tk)
        carry, acc = _sb_block(q_b, k_ref_at[pl.ds(start, tk), :], v_ref_at[pl.ds(start, tk), :],
                               carry, acc, scale=scale, diagonal=False)
        return j - 1, jnp.max(carry), carry, acc

    _, _, carry, acc = lax.while_loop(cond, body, (first_block, jnp.max(carry), carry, acc))
    return acc


def _sb_prompt_kernel(q_ref, k_ref, v_ref, o_ref, *, tq, scale):
    i = pl.program_id(1)
    q_b = q_ref[...]
    k_at, v_at = k_ref.at[0], v_ref.at[0]
    start = pl.multiple_of(i * tq, tq)
    carry = jnp.zeros((tq, 1), F32)
    acc = jnp.zeros(o_ref.shape, F32)
    carry, acc = _sb_block(q_b, k_at[pl.ds(start, tq), :], v_at[pl.ds(start, tq), :], carry, acc,
                           scale=scale, diagonal=True)
    acc = _sb_sweep(q_b, k_at, v_at, i - 1, carry, acc, tk=tq, scale=scale)
    o_ref[...] = acc.astype(o_ref.dtype)


def _sb_prompt(q, k_hm, v_hm, *, seq):
    n_heads, _, d = k_hm.shape
    tq = _tile(seq, 256)
    kv_spec = pl.BlockSpec((1, seq, d), lambda h, i: (h, 0, 0), pipeline_mode=pl.Buffered(1))
    return pl.pallas_call(
        functools.partial(_sb_prompt_kernel, tq=tq, scale=float(d) ** -0.5),
        grid=(n_heads, seq // tq),
        in_specs=[pl.BlockSpec((tq, d), lambda h, i: (i, h)), kv_spec, kv_spec],
        out_specs=pl.BlockSpec((tq, d), lambda h, i: (i, h)),
        out_shape=jax.ShapeDtypeStruct((seq, n_heads * d), BF16),
        compiler_params=pltpu.CompilerParams(
            dimension_semantics=("parallel", "arbitrary"),
            vmem_limit_bytes=48 * MIB),
        name="sb_prompt",
    )(q, k_hm, v_hm)


def _sb_decode_kernel(q_ref, kn_ref, vn_ref, kp_ref, vp_ref, o_ref, *, tk, scale):
    q_b = q_ref[...]
    tq = q_b.shape[0]
    carry = jnp.zeros((tq, 1), F32)
    acc = jnp.zeros(o_ref.shape, F32)
    carry, acc = _sb_block(q_b, kn_ref[0], vn_ref[0], carry, acc, scale=scale, diagonal=True)
    past_len = kp_ref.shape[2]
    acc = _sb_sweep(q_b, kp_ref.at[0, 0], vp_ref.at[0, 0], past_len // tk - 1, carry, acc, tk=tk, scale=scale)
    o_ref[...] = acc.astype(o_ref.dtype)


def _sb_decode(q, k_hm, v_hm, past_k, past_v, *, batch, seq):
    n_heads, _, d = k_hm.shape
    past_len = past_k.shape[2]
    tk = _tile(past_len, 256)
    new_spec = pl.BlockSpec((1, seq, d), lambda b, h: (h, b, 0))
    past_spec = pl.BlockSpec((1, 1, past_len, d), lambda b, h: (b, h, 0, 0))
    return pl.pallas_call(
        functools.partial(_sb_decode_kernel, tk=tk, scale=float(d) ** -0.5),
        grid=(batch, n_heads),
        in_specs=[pl.BlockSpec((seq, d), lambda b, h: (b, h)), new_spec, new_spec, past_spec, past_spec],
        out_specs=pl.BlockSpec((seq, d), lambda b, h: (b, h)),
        out_shape=jax.ShapeDtypeStruct((batch * seq, n_heads * d), BF16),
        compiler_params=pltpu.CompilerParams(dimension_semantics=("parallel", "parallel")),
        name="sb_decode",
    )(q, k_hm, v_hm, past_k, past_v)


def _merge_kernel(x_ref, ha_ref, om_ref, hs_ref, gm_ref, gs_ref, wbm_ref, wbs_ref, wout_ref, g_ref, b_ref,
                  o_ref, *, alpha):
    h_m = (jax.nn.sigmoid(om_ref[...]) * ha_ref[...]).astype(BF16)
    t_m = jnp.dot(h_m, wbm_ref[...], preferred_element_type=F32)
    t_s = jnp.dot(hs_ref[...], wbs_ref[...], preferred_element_type=F32)
    merged = jax.nn.sigmoid(gm_ref[...]) * t_m + jax.nn.sigmoid(gs_ref[...]) * t_s
    y = jnp.dot(merged.astype(BF16), wout_ref[...], preferred_element_type=F32)
    o_ref[...] = _layer_norm(alpha * x_ref[...] + y, g_ref[...], b_ref[...])


def _merge(x, h_a, proj, h_s, w_bm, w_bs, w_out, ln_g, ln_b, *, o_col, gm_col, gs_col, alpha):
    n, dm = x.shape
    da, ds = h_a.shape[1], h_s.shape[1]
    tm = _tile(n, 256)
    const = lambda shape: pl.BlockSpec(shape, lambda i: (0, 0), pipeline_mode=pl.Buffered(1))
    return pl.pallas_call(
        functools.partial(_merge_kernel, alpha=alpha),
        grid=(n // tm,),
        in_specs=[
            pl.BlockSpec((tm, dm), lambda i: (i, 0)),
            pl.BlockSpec((tm, da), lambda i: (i, 0)),
            pl.BlockSpec((tm, da), lambda i: (i, o_col // da)),
            pl.BlockSpec((tm, ds), lambda i: (i, 0)),
            pl.BlockSpec((tm, dm), lambda i: (i, gm_col // dm)),
            pl.BlockSpec((tm, dm), lambda i: (i, gs_col // dm)),
            const((da, dm)), const((ds, dm)), const((dm, dm)), const((1, dm)), const((1, dm)),
        ],
        out_specs=pl.BlockSpec((tm, dm), lambda i: (i, 0)),
        out_shape=jax.ShapeDtypeStruct((n, dm), F32),
        compiler_params=pltpu.CompilerParams(dimension_semantics=("parallel",), vmem_limit_bytes=48 * MIB),
        name="merge_out_ln1",
    )(x, h_a, proj, h_s, proj, proj, w_bm, w_bs, w_out, ln_g, ln_b)


def _route(logits, *, n_experts, n_groups):
    per_group = n_experts // n_groups
    lane = lax.broadcasted_iota(jnp.int32, logits.shape, 1)
    big = jnp.int32(LANES)
    is_group = jnp.logical_and(lane >= n_experts, lane < n_experts + n_groups)
    lg = jnp.where(is_group, logits, -jnp.inf)
    lg_max = jnp.max(lg, axis=-1, keepdims=True)
    p_sel = 1.0 / jnp.sum(jnp.exp(lg - lg_max), axis=-1, keepdims=True)
    grp = jnp.min(jnp.where(lg == lg_max, lane - n_experts, big), axis=-1, keepdims=True)
    in_grp = jnp.logical_and(lane >= grp * per_group, lane < (grp + 1) * per_group)
    le = jnp.where(in_grp, logits, -jnp.inf)
    v1 = jnp.max(le, axis=-1, keepdims=True)
    i1 = jnp.min(jnp.where(le == v1, lane, big), axis=-1, keepdims=True)
    le2 = jnp.where(lane == i1, -jnp.inf, le)
    v2 = jnp.max(le2, axis=-1, keepdims=True)
    i2 = jnp.min(jnp.where(le2 == v2, lane, big), axis=-1, keepdims=True)
    e2 = jnp.exp(v2 - v1)
    w1 = p_sel / (1.0 + e2)
    w2 = p_sel * e2 / (1.0 + e2)
    return jnp.where(lane == i1, w1, jnp.where(lane == i2, w2, 0.0))


def _moe_dense_kernel(x_ref, wr_ref, br_ref, wg_ref, wu_ref, wd_ref, g_ref, b_ref, o_ref,
                      xb_ref, comb_ref, acc_ref, *, alpha, n_experts, n_groups):
    e = pl.program_id(1)

    @pl.when(e == 0)
    def _():
        x = x_ref[...]
        xb_ref[...] = x.astype(BF16)
        logits = jnp.dot(x, wr_ref[...], precision=HIGHEST, preferred_element_type=F32) + br_ref[...]
        comb_ref[...] = _route(logits, n_experts=n_experts, n_groups=n_groups)
        acc_ref[...] = jnp.zeros_like(acc_ref)

    xb = xb_ref[...]
    gate = jnp.dot(xb, wg_ref[0], preferred_element_type=F32)
    up = jnp.dot(xb, wu_ref[0], preferred_element_type=F32)
    hid = (gate * jax.nn.sigmoid(gate) * up).astype(BF16)
    out_e = jnp.dot(hid, wd_ref[0], preferred_element_type=F32)
    lane = lax.broadcasted_iota(jnp.int32, comb_ref.shape, 1)
    c_e = jnp.sum(jnp.where(lane == e, comb_ref[...], 0.0), axis=-1, keepdims=True)
    acc_ref[...] += c_e * out_e

    @pl.when(e == n_experts - 1)
    def _():
        o_ref[...] = _layer_norm(alpha * x_ref[...] + acc_ref[...], g_ref[...], b_ref[...])


def _moe_dense(x, w_router, b_router, w_gate, w_up, w_down, ln_g, ln_b, *, n_groups, alpha):
    n, dm = x.shape
    n_experts, _, de = w_gate.shape
    tm = _tile(n, 512)
    const = lambda shape: pl.BlockSpec(shape, lambda i, e: (0, 0))
    return pl.pallas_call(
        functools.partial(_moe_dense_kernel, alpha=alpha, n_experts=n_experts, n_groups=n_groups),
        grid=(n // tm, n_experts),
        in_specs=[
            pl.BlockSpec((tm, dm), lambda i, e: (i, 0)),
            const((dm, LANES)), const((1, LANES)),
            pl.BlockSpec((1, dm, de), lambda i, e: (e, 0, 0)),
            pl.BlockSpec((1, dm, de), lambda i, e: (e, 0, 0)),
            pl.BlockSpec((1, de, dm), lambda i, e: (e, 0, 0)),
            const((1, dm)), const((1, dm)),
        ],
        out_specs=pl.BlockSpec((tm, dm), lambda i, e: (i, 0)),
        out_shape=jax.ShapeDtypeStruct((n, dm), F32),
        scratch_shapes=[pltpu.VMEM((tm, dm), BF16), pltpu.VMEM((tm, LANES), F32), pltpu.VMEM((tm, dm), F32)],
        compiler_params=pltpu.CompilerParams(
            dimension_semantics=("parallel", "arbitrary"), vmem_limit_bytes=48 * MIB),
        name="moe_ln2",
    )(x, w_router, b_router, w_gate, w_up, w_down, ln_g, ln_b)


def _layer(x, conv_state, c0, n0, m0, past_k, past_v, p, *, chunk_pref):
    batch, seq, dm = x.shape
    n_tok = batch * seq
    xf = x.reshape(n_tok, dm)
    n_heads, d = p["n_heads"], p["head_dim"]
    da = n_heads * d
    cols = p["cols"]

    proj = _project(xf, p["w_main"], F32)
    gates = _project(xf, p["w_if"], F32)
    q_s = _project(xf, p["w_qs"], BF16)
    k_hm = _project(xf, p["w_ks"], F32, head_major=True)
    v_hm = _project(xf, p["w_vs"], F32, head_major=True)

    chunk = _tile(seq, chunk_pref)
    h_a, c_new, n_new, m_new = _mlstm(
        proj, gates, p["b_gate_row"], p["conv_w"], p["conv_b"], conv_state, c0, n0, m0,
        batch=batch, seq=seq, chunk=chunk, n_heads=n_heads, head_dim=d,
        q_col=cols["q"], k_col=cols["k"], v_col=cols["v"])

    if past_k is None:
        assert batch == 1
        h_s = _sb_prompt(q_s, k_hm, v_hm, seq=seq)
    else:
        h_s = _sb_decode(q_s, k_hm, v_hm, past_k, past_v, batch=batch, seq=seq)

    x1 = _merge(xf, h_a, proj, h_s, p["w_bm"], p["w_bs"], p["w_out"], p["ln1_g"], p["ln1_b"],
                o_col=cols["o"], gm_col=cols["gm"], gs_col=cols["gs"], alpha=p["alpha"])
    x2 = _moe_dense(x1, p["w_router"], p["b_router"], p["w_eg"], p["w_eu"], p["w_ed"], p["ln2_g"], p["ln2_b"],
                    n_groups=p["n_groups"], alpha=p["alpha"])

    n_sb = k_hm.shape[0]
    sb_k = k_hm.reshape(n_sb, batch, seq, -1).transpose(1, 0, 2, 3)
    sb_v = v_hm.reshape(n_sb, batch, seq, -1).transpose(1, 0, 2, 3)
    new_conv = proj.reshape(batch, seq, -1)[:, seq - (CONV_W - 1):, cols["q"]:cols["q"] + 2 * da]
    return x2.reshape(batch, seq, dm), (sb_k, sb_v, c_new, n_new, m_new, new_conv)


def _prepare_layer(l, depth, w_in, b_gate, conv_w, conv_b, w_branch_m, w_branch_s, w_out, ln1_g, ln1_b,
                   w_router_group, b_router_group, w_router_expert, b_router_expert,
                   w_exp_gate, w_exp_up, w_exp_down, ln2_g, ln2_b, head_dim):
    dm = w_in.shape[1]
    da2 = conv_w.shape[-1]
    da = da2 // 2
    n_heads = b_gate.shape[-1] // 2
    ds = w_branch_s.shape[1]
    sizes = (da2, da, da, 2 * n_heads, ds, ds, ds, dm, dm)
    offs = [0]
    for s in sizes:
        offs.append(offs[-1] + s)
    w = w_in[l]
    part = lambda i: w[:, offs[i]:offs[i + 1]]
    w_main = jnp.concatenate([part(0), part(1), part(2), part(7), part(8)], axis=1).astype(BF16)
    cols = {"q": 0, "k": da, "v": da2, "o": da2 + da, "gm": da2 + 2 * da, "gs": da2 + 2 * da + dm}
    assert cols["o"] % da == 0 and cols["gm"] % dm == 0 and cols["gs"] % dm == 0
    pad_if = LANES - 2 * n_heads
    w_if = jnp.pad(part(3), ((0, 0), (0, pad_if))).astype(BF16)
    b_gate_row = jnp.pad(b_gate[l].astype(F32), (0, pad_if)).reshape(1, LANES)
    n_groups = w_router_group.shape[-1]
    n_experts = w_router_expert.shape[-1]
    pad_r = LANES - n_experts - n_groups
    w_router = jnp.pad(jnp.concatenate([w_router_expert[l], w_router_group[l]], axis=1), ((0, 0), (0, pad_r)))
    b_router = jnp.pad(jnp.concatenate([b_router_expert[l], b_router_group[l]]), (0, pad_r)).reshape(1, LANES)
    return {
        "n_heads": n_heads, "head_dim": head_dim, "cols": cols, "n_groups": n_groups,
        "alpha": (2.0 * depth) ** 0.25,
        "w_main": w_main, "w_if": w_if, "b_gate_row": b_gate_row,
        "w_qs": part(4).astype(BF16), "w_ks": part(5).astype(BF16), "w_vs": part(6).astype(BF16),
        "conv_w": conv_w[l], "conv_b": conv_b[l],
        "w_bm": w_branch_m[l].astype(BF16), "w_bs": w_branch_s[l].astype(BF16), "w_out": w_out[l].astype(BF16),
        "ln1_g": ln1_g[l].reshape(1, dm), "ln1_b": ln1_b[l].reshape(1, dm),
        "w_router": w_router.astype(F32), "b_router": b_router.astype(F32),
        "w_eg": w_exp_gate[l].astype(BF16), "w_eu": w_exp_up[l].astype(BF16), "w_ed": w_exp_down[l].astype(BF16),
        "ln2_g": ln2_g[l].reshape(1, dm), "ln2_b": ln2_b[l].reshape(1, dm),
    }


def kernel(x_prompt, x_sample, cache_sb_k, cache_sb_v, state_mlstm_C, state_mlstm_n, state_mlstm_m, state_conv,
           w_in, b_gate, conv_w, conv_b, w_branch_m, w_branch_s, w_out, ln1_g, ln1_b,
           w_router_group, b_router_group, w_router_expert, b_router_expert,
           w_exp_gate, w_exp_up, w_exp_down, ln2_g, ln2_b):
    depth = w_in.shape[0]
    bp = x_prompt.shape[0]
    n_heads, head_dim = state_mlstm_C.shape[2], state_mlstm_C.shape[3]
    da2 = conv_w.shape[-1]
    xp, xs = x_prompt, x_sample
    st_p_all, st_s_all = [], []
    for l in range(depth):
        p = _prepare_layer(l, depth, w_in, b_gate, conv_w, conv_b, w_branch_m, w_branch_s, w_out, ln1_g, ln1_b,
                           w_router_group, b_router_group, w_router_expert, b_router_expert,
                           w_exp_gate, w_exp_up, w_exp_down, ln2_g, ln2_b, head_dim)
        xp, st_p = _layer(
            xp, jnp.zeros((bp, CONV_W - 1, da2), F32), jnp.zeros((bp, n_heads, head_dim, head_dim), F32),
            jnp.zeros((bp, n_heads, head_dim), F32), jnp.full((bp, n_heads), NEG, F32), None, None, p,
            chunk_pref=256)
        xs, st_s = _layer(xs, state_conv[l], state_mlstm_C[l], state_mlstm_n[l], state_mlstm_m[l],
                          cache_sb_k[l], cache_sb_v[l], p, chunk_pref=256)
        st_p_all.append(st_p)
        st_s_all.append(st_s)
    stack = lambda states, i: jnp.stack([s[i] for s in states], axis=0)
    return (xp, xs,
            *(stack(st_p_all, i) for i in range(6)),
            *(stack(st_s_all, i) for i in range(6)))
```

```python
import functools

import jax
import jax.numpy as jnp
from jax import lax
from jax.experimental import pallas as pl
from jax.experimental.pallas import tpu as pltpu

F32 = jnp.float32
BF16 = jnp.bfloat16
HIGHEST = lax.Precision.HIGHEST

LANES = 128
SUBLANES = 8
LN_EPS = 1e-5
NEG = -1e30
CONV_W = 4
TOP_K_IN_GROUP = 2
EXP_ZERO_BELOW = -104.0
MIB = 1024 * 1024

_NT = (((1,), (1,)), ((), ()))
_TN = (((0,), (0,)), ((), ()))


def _tile(n, pref):
    if n <= pref:
        return n
    t = pref
    while t >= SUBLANES:
        if n % t == 0 and t % SUBLANES == 0:
            return t
        t -= SUBLANES
    return n


def _log_sigmoid(x):
    return jnp.minimum(x, 0.0) - jnp.log1p(jnp.exp(-jnp.abs(x)))


def _layer_norm(x, g, b):
    mu = jnp.mean(x, axis=-1, keepdims=True)
    d = x - mu
    var = jnp.mean(d * d, axis=-1, keepdims=True)
    return d * lax.rsqrt(var + LN_EPS) * g + b


def _proj_kernel(x_ref, w_ref, o_ref, xb_ref, *, head_major):
    @pl.when(pl.program_id(1) == 0)
    def _():
        xb_ref[...] = x_ref[...].astype(BF16)

    acc = jnp.dot(xb_ref[...], w_ref[...], preferred_element_type=F32)
    if head_major:
        for h in range(o_ref.shape[0]):
            o_ref[h] = acc[:, h * LANES:(h + 1) * LANES].astype(o_ref.dtype)
    else:
        o_ref[...] = acc.astype(o_ref.dtype)


def _project(x, w, out_dtype, head_major=False):
    n, k = x.shape
    c = w.shape[1]
    tm = _tile(n, 1024)
    tn = _tile(c, 512) if c % LANES == 0 else c
    if head_major:
        out_shape = jax.ShapeDtypeStruct((c // LANES, n, LANES), out_dtype)
        out_spec = pl.BlockSpec((tn // LANES, tm, LANES), lambda i, j: (j, i, 0))
    else:
        out_shape = jax.ShapeDtypeStruct((n, c), out_dtype)
        out_spec = pl.BlockSpec((tm, tn), lambda i, j: (i, j))
    return pl.pallas_call(
        functools.partial(_proj_kernel, head_major=head_major),
        grid=(n // tm, c // tn),
        in_specs=[pl.BlockSpec((tm, k), lambda i, j: (i, 0)),
                  pl.BlockSpec((k, tn), lambda i, j: (0, j))],
        out_specs=out_spec,
        out_shape=out_shape,
        scratch_shapes=[pltpu.VMEM((tm, k), BF16)],
        compiler_params=pltpu.CompilerParams(
            dimension_semantics=("parallel", "arbitrary"),
            vmem_limit_bytes=48 * MIB),
        name="in_proj",
    )(x, w)


def _mlstm_kernel(q_ref, k_ref, v_ref, g_ref, bg_ref, cwq_ref, cwk_ref, cbq_ref, cbk_ref,
                  csq_ref, csk_ref, c0_ref, n0_ref, m0_ref,
                  h_ref, c_ref, n_ref, m_ref, qext_ref, kext_ref, *, chunk, n_heads, k_scale):
    head = pl.program_id(1)
    step = pl.program_id(2)
    L = chunk
    halo = CONV_W - 1
    base = SUBLANES - halo

    @pl.when(step == 0)
    def _():
        qext_ref[base:SUBLANES, :] = csq_ref[0]
        kext_ref[base:SUBLANES, :] = csk_ref[0]
        c_ref[...] = c0_ref[...]
        n_ref[...] = n0_ref[...]
        m_ref[...] = m0_ref[...]

    def conv_silu(raw_ref, ext_ref, cw_ref, cb_ref):
        ext_ref[SUBLANES:SUBLANES + L, :] = raw_ref[...]
        acc = cb_ref[...] + ext_ref[base:base + L, :] * cw_ref[0:1, :]
        for j in range(1, CONV_W):
            acc = acc + ext_ref[base + j:base + j + L, :] * cw_ref[j:j + 1, :]
        ext_ref[base:SUBLANES, :] = ext_ref[base + L:SUBLANES + L, :]
        return acc * jax.nn.sigmoid(acc)

    q = conv_silu(q_ref, qext_ref, cwq_ref, cbq_ref)
    k = conv_silu(k_ref, kext_ref, cwk_ref, cbk_ref) * k_scale
    q_b = q.astype(BF16)
    k_b = k.astype(BF16)
    v_b = v_ref[...].astype(BF16)

    lane = lax.broadcasted_iota(jnp.int32, (L, LANES), 1)
    pre = g_ref[...] + bg_ref[...]
    gates = jnp.where(lane < n_heads, pre, _log_sigmoid(pre))
    row_i = lax.broadcasted_iota(jnp.int32, (L, L), 0)
    col_i = lax.broadcasted_iota(jnp.int32, (L, L), 1)
    causal = col_i <= row_i
    tril = causal.astype(F32)
    triu = (row_i <= col_i).astype(F32)
    sel = (lax.broadcasted_iota(jnp.int32, (SUBLANES, LANES), 0)
           == lax.broadcasted_iota(jnp.int32, (SUBLANES, LANES), 1)).astype(F32)
    cum_cols = jnp.dot(tril, gates, precision=HIGHEST, preferred_element_type=F32)
    gates_t = lax.dot_general(sel, gates, _NT, precision=HIGHEST, preferred_element_type=F32)
    cum_rows = jnp.dot(gates_t, triu, precision=HIGHEST, preferred_element_type=F32)

    sub = lax.broadcasted_iota(jnp.int32, (SUBLANES, L), 0)
    i_col = jnp.sum(jnp.where(lane == head, gates, 0.0), axis=1, keepdims=True)
    b_col = jnp.sum(jnp.where(lane == head + n_heads, cum_cols, 0.0), axis=1, keepdims=True)
    i_row = jnp.sum(jnp.where(sub == head, gates_t, 0.0), axis=0, keepdims=True)
    b_row = jnp.sum(jnp.where(sub == head + n_heads, cum_rows, 0.0), axis=0, keepdims=True)

    c_prev = c_ref[0, 0]
    n_prev = n_ref[0, 0]
    m_prev = m_ref[0, 0][:, 0:1]

    dmat = jnp.where(causal, b_col - b_row + i_row, NEG)
    inter = b_col + m_prev
    m_t = jnp.maximum(jnp.max(dmat, axis=-1, keepdims=True), inter)
    w = jnp.exp(dmat - m_t)
    s = lax.dot_general(q_b, k_b, _NT, preferred_element_type=F32) * w
    e_inter = jnp.exp(inter - m_t)
    num = (jnp.dot(s.astype(BF16), v_b, preferred_element_type=F32)
           + e_inter * jnp.dot(q_b, c_prev.astype(BF16), preferred_element_type=F32))
    den = jnp.sum(s, axis=-1, keepdims=True) + e_inter * jnp.sum(q * n_prev, axis=-1, keepdims=True)
    h_ref[...] = num / jnp.maximum(jnp.abs(den), jnp.exp(-m_t))

    g_last = b_col[L - 1:L, :]
    w_last = g_last - b_col + i_col
    m_new = jnp.maximum(g_last + m_prev, jnp.max(w_last, axis=0, keepdims=True))
    decay = jnp.exp(g_last + m_prev - m_new)
    kw = jnp.exp(w_last - m_new) * k
    c_ref[0, 0] = decay * c_prev + lax.dot_general(kw.astype(BF16), v_b, _TN, preferred_element_type=F32)
    n_ref[0, 0] = decay * n_prev + jnp.sum(kw, axis=0, keepdims=True)
    m_ref[0, 0] = jnp.broadcast_to(m_new, (1, LANES))


def _mlstm(proj, gates, b_gate_row, conv_w, conv_b, conv_state, c0, n0, m0, *, batch, seq, chunk,
           n_heads, head_dim, q_col, k_col, v_col):
    n_tok = batch * seq
    nc = seq // chunk
    d = head_dim
    qb, kb, vb = q_col // d, k_col // d, v_col // d
    kcw = (k_col - q_col) // d
    row = lambda b, h, c: b * nc + c
    n0r = n0.reshape(batch, n_heads, 1, d)
    m0r = jnp.broadcast_to(m0.reshape(batch, n_heads, 1, 1), (batch, n_heads, 1, LANES))
    cbr = conv_b.reshape(1, -1)
    state_spec = lambda shape: pl.BlockSpec(shape, lambda b, h, c: (b, h, 0, 0))
    h, c_new, n_new, m_new = pl.pallas_call(
        functools.partial(_mlstm_kernel, chunk=chunk, n_heads=n_heads, k_scale=float(d) ** -0.5),
        grid=(batch, n_heads, nc),
        in_specs=[
            pl.BlockSpec((chunk, d), lambda b, h, c: (row(b, h, c), qb + h)),
            pl.BlockSpec((chunk, d), lambda b, h, c: (row(b, h, c), kb + h)),
            pl.BlockSpec((chunk, d), lambda b, h, c: (row(b, h, c), vb + h)),
            pl.BlockSpec((chunk, LANES), lambda b, h, c: (row(b, h, c), 0)),
            pl.BlockSpec((1, LANES), lambda b, h, c: (0, 0)),
            pl.BlockSpec((CONV_W, d), lambda b, h, c: (0, h)),
            pl.BlockSpec((CONV_W, d), lambda b, h, c: (0, kcw + h)),
            pl.BlockSpec((1, d), lambda b, h, c: (0, h)),
            pl.BlockSpec((1, d), lambda b, h, c: (0, kcw + h)),
            pl.BlockSpec((1, CONV_W - 1, d), lambda b, h, c: (b, 0, h)),
            pl.BlockSpec((1, CONV_W - 1, d), lambda b, h, c: (b, 0, kcw + h)),
            state_spec((1, 1, d, d)),
            state_spec((1, 1, 1, d)),
            state_spec((1, 1, 1, LANES)),
        ],
        out_specs=[
            pl.BlockSpec((chunk, d), lambda b, h, c: (row(b, h, c), h)),
            state_spec((1, 1, d, d)),
            state_spec((1, 1, 1, d)),
            state_spec((1, 1, 1, LANES)),
        ],
        out_shape=[
            jax.ShapeDtypeStruct((n_tok, n_heads * d), F32),
            jax.ShapeDtypeStruct((batch, n_heads, d, d), F32),
            jax.ShapeDtypeStruct((batch, n_heads, 1, d), F32),
            jax.ShapeDtypeStruct((batch, n_heads, 1, LANES), F32),
        ],
        scratch_shapes=[pltpu.VMEM((chunk + SUBLANES, d), F32), pltpu.VMEM((chunk + SUBLANES, d), F32)],
        compiler_params=pltpu.CompilerParams(
            dimension_semantics=("parallel", "parallel", "arbitrary"),
            vmem_limit_bytes=32 * MIB),
        name="mlstm",
    )(proj, proj, proj, gates, b_gate_row, conv_w, conv_w, cbr, cbr, conv_state, conv_state, c0, n0r, m0r)
    return h, c_new, n_new.reshape(batch, n_heads, d), m_new[:, :, 0, 0]


def _sb_block(q_b, k_blk, v_blk, carry, acc, *, scale, diagonal):
    tq, tk = q_b.shape[0], k_blk.shape[0]
    z = lax.dot_general(q_b, k_blk.astype(BF16), _NT, preferred_element_type=F32) * scale
    softplus = jnp.maximum(z, 0.0) + jnp.log1p(jnp.exp(-jnp.abs(z)))
    log_keep = -softplus
    log_beta = z - softplus
    if diagonal:
        mask = (lax.broadcasted_iota(jnp.int32, (tq, tk), 1) < lax.broadcasted_iota(jnp.int32, (tq, tk), 0))
        log_keep = jnp.where(mask, log_keep, 0.0)
    later_mat = (lax.broadcasted_iota(jnp.int32, (tk, tk), 0)
                 > lax.broadcasted_iota(jnp.int32, (tk, tk), 1)).astype(BF16)
    hi = log_keep.astype(BF16)
    rem = log_keep - hi.astype(F32)
    mid = rem.astype(BF16)
    lo = (rem - mid.astype(F32)).astype(BF16)
    later = (jnp.dot(hi, later_mat, preferred_element_type=F32)
             + jnp.dot(mid, later_mat, preferred_element_type=F32)
             + jnp.dot(lo, later_mat, preferred_element_type=F32)) + carry
    a = jnp.exp(log_beta + later)
    if diagonal:
        a = jnp.where(mask, a, 0.0)
    acc = acc + jnp.dot(a.astype(BF16), v_blk.astype(BF16), preferred_element_type=F32)
    carry = carry + jnp.sum(log_keep, axis=-1, keepdims=True)
    return carry, acc


def _sb_sweep(q_b, k_ref_at, v_ref_at, first_block, carry, acc, *, tk, scale):
    def cond(state):
        j, cmax, _, _ = state
        return jnp.logical_and(j >= 0, cmax > EXP_ZERO_BELOW)

    def body(state):
        j, _, carry, acc = state
        start = pl.multiple_of(j * tk, tk)
        carry, acc = _sb_block(q_b, k_ref_at[pl.ds(start, tk), :], v_ref_at[pl.ds(start, tk), :],
                               carry, acc, scale=scale, diagonal=False)
        return j - 1, jnp.max(carry), carry, acc

    _, _, carry, acc = lax.while_loop(cond, body, (first_block, jnp.max(carry), carry, acc))
    return acc


def _sb_prompt_kernel(q_ref, k_ref, v_ref, o_ref, *, tq, scale):
    i = pl.program_id(1)
    q_b = q_ref[...]
    k_at, v_at = k_ref.at[0], v_ref.at[0]
    start = pl.multiple_of(i * tq, tq)
    carry = jnp.zeros((tq, 1), F32)
    acc = jnp.zeros(o_ref.shape, F32)
    carry, acc = _sb_block(q_b, k_at[pl.ds(start, tq), :], v_at[pl.ds(start, tq), :], carry, acc,
                           scale=scale, diagonal=True)
    acc = _sb_sweep(q_b, k_at, v_at, i - 1, carry, acc, tk=tq, scale=scale)
    o_ref[...] = acc.astype(o_ref.dtype)


def _sb_prompt(q, k_hm, v_hm, *, seq):
    n_heads, _, d = k_hm.shape
    tq = _tile(seq, 256)
    kv_spec = pl.BlockSpec((1, seq, d), lambda h, i: (h, 0, 0), pipeline_mode=pl.Buffered(1))
    return pl.pallas_call(
        functools.partial(_sb_prompt_kernel, tq=tq, scale=float(d) ** -0.5),
        grid=(n_heads, seq // tq),
        in_specs=[pl.BlockSpec((tq, d), lambda h, i: (i, h)), kv_spec, kv_spec],
        out_specs=pl.BlockSpec((tq, d), lambda h, i: (i, h)),
        out_shape=jax.ShapeDtypeStruct((seq, n_heads * d), BF16),
        compiler_params=pltpu.CompilerParams(
            dimension_semantics=("parallel", "arbitrary"),
            vmem_limit_bytes=48 * MIB),
        name="sb_prompt",
    )(q, k_hm, v_hm)


def _sb_decode_kernel(q_ref, kn_ref, vn_ref, kp_ref, vp_ref, o_ref, *, tk, scale):
    q_b = q_ref[...]
    tq = q_b.shape[0]
    carry = jnp.zeros((tq, 1), F32)
    acc = jnp.zeros(o_ref.shape, F32)
    carry, acc = _sb_block(q_b, kn_ref[0], vn_ref[0], carry, acc, scale=scale, diagonal=True)
    past_len = kp_ref.shape[2]
    acc = _sb_sweep(q_b, kp_ref.at[0, 0], vp_ref.at[0, 0], past_len // tk - 1, carry, acc, tk=tk, scale=scale)
    o_ref[...] = acc.astype(o_ref.dtype)


def _sb_decode(q, k_hm, v_hm, past_k, past_v, *, batch, seq):
    n_heads, _, d = k_hm.shape
    past_len = past_k.shape[2]
    tk = _tile(past_len, 256)
    new_spec = pl.BlockSpec((1, seq, d), lambda b, h: (h, b, 0))
    past_spec = pl.BlockSpec((1, 1, past_len, d), lambda b, h: (b, h, 0, 0))
    return pl.pallas_call(
        functools.partial(_sb_decode_kernel, tk=tk, scale=float(d) ** -0.5),
        grid=(batch, n_heads),
        in_specs=[pl.BlockSpec((seq, d), lambda b, h: (b, h)), new_spec, new_spec, past_spec, past_spec],
        out_specs=pl.BlockSpec((seq, d), lambda b, h: (b, h)),
        out_shape=jax.ShapeDtypeStruct((batch * seq, n_heads * d), BF16),
        compiler_params=pltpu.CompilerParams(dimension_semantics=("parallel", "parallel")),
        name="sb_decode",
    )(q, k_hm, v_hm, past_k, past_v)


ROUTE_E1, ROUTE_E2, ROUTE_W1, ROUTE_W2 = 0, 1, 2, 3


def _route(logits, *, n_experts, n_groups):
    per_group = n_experts // n_groups
    lane = lax.broadcasted_iota(jnp.int32, logits.shape, 1)
    big = jnp.int32(LANES)
    is_group = jnp.logical_and(lane >= n_experts, lane < n_experts + n_groups)
    lg = jnp.where(is_group, logits, -jnp.inf)
    lg_max = jnp.max(lg, axis=-1, keepdims=True)
    p_sel = 1.0 / jnp.sum(jnp.exp(lg - lg_max), axis=-1, keepdims=True)
    grp = jnp.min(jnp.where(lg == lg_max, lane - n_experts, big), axis=-1, keepdims=True)
    in_grp = jnp.logical_and(lane >= grp * per_group, lane < (grp + 1) * per_group)
    le = jnp.where(in_grp, logits, -jnp.inf)
    v1 = jnp.max(le, axis=-1, keepdims=True)
    i1 = jnp.min(jnp.where(le == v1, lane, big), axis=-1, keepdims=True)
    le2 = jnp.where(lane == i1, -jnp.inf, le)
    v2 = jnp.max(le2, axis=-1, keepdims=True)
    i2 = jnp.min(jnp.where(le2 == v2, lane, big), axis=-1, keepdims=True)
    e2 = jnp.exp(v2 - v1)
    w1 = p_sel / (1.0 + e2)
    w2 = p_sel * e2 / (1.0 + e2)
    rec = jnp.where(lane == ROUTE_E1, i1.astype(F32), 0.0)
    rec = jnp.where(lane == ROUTE_E2, i2.astype(F32), rec)
    rec = jnp.where(lane == ROUTE_W1, w1, rec)
    return jnp.where(lane == ROUTE_W2, w2, rec)


def _merge_kernel(x_ref, ha_ref, om_ref, hs_ref, gm_ref, gs_ref, wbm_ref, wbs_ref, wout_ref, g_ref, b_ref,
                  wr_ref, br_ref, o_ref, route_ref, *, alpha, n_experts, n_groups):
    h_m = (jax.nn.sigmoid(om_ref[...]) * ha_ref[...]).astype(BF16)
    t_m = jnp.dot(h_m, wbm_ref[...], preferred_element_type=F32)
    t_s = jnp.dot(hs_ref[...], wbs_ref[...], preferred_element_type=F32)
    merged = jax.nn.sigmoid(gm_ref[...]) * t_m + jax.nn.sigmoid(gs_ref[...]) * t_s
    y = jnp.dot(merged.astype(BF16), wout_ref[...], preferred_element_type=F32)
    x1 = _layer_norm(alpha * x_ref[...] + y, g_ref[...], b_ref[...])
    o_ref[...] = x1
    logits = jnp.dot(x1, wr_ref[...], precision=HIGHEST, preferred_element_type=F32) + br_ref[...]
    route_ref[...] = _route(logits, n_experts=n_experts, n_groups=n_groups)


def _merge(x, h_a, proj, h_s, w_bm, w_bs, w_out, ln_g, ln_b, w_router, b_router, *, o_col, gm_col, gs_col, alpha,
           n_experts, n_groups):
    n, dm = x.shape
    da, ds = h_a.shape[1], h_s.shape[1]
    tm = _tile(n, 256)
    const = lambda shape: pl.BlockSpec(shape, lambda i: (0, 0), pipeline_mode=pl.Buffered(1))
    return pl.pallas_call(
        functools.partial(_merge_kernel, alpha=alpha, n_experts=n_experts, n_groups=n_groups),
        grid=(n // tm,),
        in_specs=[
            pl.BlockSpec((tm, dm), lambda i: (i, 0)),
            pl.BlockSpec((tm, da), lambda i: (i, 0)),
            pl.BlockSpec((tm, da), lambda i: (i, o_col // da)),
            pl.BlockSpec((tm, ds), lambda i: (i, 0)),
            pl.BlockSpec((tm, dm), lambda i: (i, gm_col // dm)),
            pl.BlockSpec((tm, dm), lambda i: (i, gs_col // dm)),
            const((da, dm)), const((ds, dm)), const((dm, dm)), const((1, dm)), const((1, dm)),
            const((dm, LANES)), const((1, LANES)),
        ],
        out_specs=[pl.BlockSpec((tm, dm), lambda i: (i, 0)), pl.BlockSpec((tm, LANES), lambda i: (i, 0))],
        out_shape=[jax.ShapeDtypeStruct((n, dm), F32), jax.ShapeDtypeStruct((n, LANES), F32)],
        compiler_params=pltpu.CompilerParams(dimension_semantics=("parallel",), vmem_limit_bytes=48 * MIB),
        name="merge_out_ln1",
    )(x, h_a, proj, h_s, proj, proj, w_bm, w_bs, w_out, ln_g, ln_b, w_router, b_router)


def _expert_kernel(te_ref, tb_ref, tv_ref, na_ref, perm_ref,
                   x_hbm, wg_ref, wu_ref, wd_ref, y_hbm,
                   xbuf, obuf, wgb, wub, wdb, gsem, ssem, *, n_tok):
    j = pl.program_id(0)
    n_act = na_ref[0]
    slot = lax.rem(j, 2)

    def gather_copy(s, i, token):
        return pltpu.make_async_copy(x_hbm.at[pl.ds(token, 1), :], xbuf.at[s, pl.ds(i, 1), :], gsem.at[s])

    def scatter_copy(s, i, row):
        return pltpu.make_async_copy(obuf.at[s, pl.ds(i, 1), :], y_hbm.at[pl.ds(row, 1), :], ssem.at[s])

    def for_rows(count, fn):
        groups = lax.shift_right_logical(count, 3)

        def group_body(g, c):
            for u in range(SUBLANES):
                fn(g * SUBLANES + u)
            return c

        def row_body(i, c):
            fn(i)
            return c

        lax.fori_loop(0, groups, group_body, 0)
        lax.fori_loop(groups * SUBLANES, count, row_body, 0)

    def start_gather(t, s):
        base = tb_ref[t]

        def start(i):
            pair = perm_ref[base + i]
            gather_copy(s, i, jnp.where(pair >= n_tok, pair - n_tok, pair)).start()

        for_rows(tv_ref[t], start)

    def start_scatter(t, s):
        base = tb_ref[t]
        for_rows(tv_ref[t], lambda i: scatter_copy(s, i, perm_ref[base + i]).start())

    def wait_rows(count, block_copy, row_copy):
        bulk = pl.multiple_of(lax.shift_right_logical(count, 3) * SUBLANES, SUBLANES)

        @pl.when(bulk > 0)
        def _():
            block_copy(pl.ds(0, bulk)).wait()

        def row_body(i, c):
            row_copy(i).wait()
            return c

        lax.fori_loop(bulk, count, row_body, 0)

    def wait_gather(t, s):
        wait_rows(tv_ref[t],
                  lambda rows: pltpu.make_async_copy(x_hbm.at[rows, :], xbuf.at[s, rows, :], gsem.at[s]),
                  lambda i: gather_copy(s, i, 0))

    def wait_scatter(t, s):
        wait_rows(tv_ref[t],
                  lambda rows: pltpu.make_async_copy(obuf.at[s, rows, :], y_hbm.at[rows, :], ssem.at[s]),
                  lambda i: scatter_copy(s, i, 0))

    @pl.when(j == 0)
    def _():
        xbuf[...] = jnp.zeros_like(xbuf)
        start_gather(0, 0)

    @pl.when(j + 1 < n_act)
    def _():
        start_gather(j + 1, 1 - slot)

    @pl.when(j < n_act)
    def _():
        wait_gather(j, slot)

        @pl.when(j >= 2)
        def _():
            wait_scatter(j - 2, slot)

        @pl.when(jnp.logical_or(j == 0, te_ref[j] != te_ref[jnp.maximum(j - 1, 0)]))
        def _():
            wgb[...] = wg_ref[0].astype(BF16)
            wub[...] = wu_ref[0].astype(BF16)
            wdb[...] = wd_ref[0].astype(BF16)

        x = xbuf[slot].astype(BF16)
        gate = jnp.dot(x, wgb[...], preferred_element_type=F32)
        up = jnp.dot(x, wub[...], preferred_element_type=F32)
        hid = (gate * jax.nn.sigmoid(gate) * up).astype(BF16)
        obuf[slot] = jnp.dot(hid, wdb[...], preferred_element_type=F32)
        start_scatter(j, slot)

        @pl.when(j == n_act - 1)
        def _():
            @pl.when(j >= 1)
            def _():
                wait_scatter(j - 1, 1 - slot)

            wait_scatter(j, slot)


def _expert_tables(route, n_experts, tile):
    n_tok = route.shape[0]
    e_flat = jnp.concatenate([route[:, ROUTE_E1], route[:, ROUTE_E2]]).astype(jnp.int32)
    perm = jnp.argsort(e_flat, stable=True).astype(jnp.int32)
    counts = jnp.sum((e_flat[:, None] == jnp.arange(n_experts, dtype=jnp.int32)[None, :]).astype(jnp.int32), axis=0)
    pair_start = jnp.cumsum(counts) - counts
    tiles_e = (counts + tile - 1) // tile
    tile_end = jnp.cumsum(tiles_e)
    n_act = tile_end[-1]
    t_max = (2 * n_tok) // tile + n_experts
    j = jnp.arange(t_max, dtype=jnp.int32)
    active = j < n_act
    te = jnp.minimum(jnp.searchsorted(tile_end, j, side="right").astype(jnp.int32), n_experts - 1)
    te = jnp.where(active, te, te[jnp.maximum(n_act - 1, 0)])
    k = j - (tile_end[te] - tiles_e[te])
    tb = jnp.where(active, pair_start[te] + k * tile, 0).astype(jnp.int32)
    tv = jnp.where(active, jnp.clip(counts[te] - k * tile, 0, tile), 0).astype(jnp.int32)
    return te, tb, tv, n_act.reshape(1).astype(jnp.int32), perm, t_max


def _experts(x1, route, w_gate, w_up, w_down):
    n_tok, dm = x1.shape
    n_experts, _, de = w_gate.shape
    tile = 256
    te, tb, tv, n_act, perm, t_max = _expert_tables(route, n_experts, tile)
    w_in_spec = pl.BlockSpec((1, dm, de), lambda j, te, tb, tv, na, perm: (te[j], 0, 0))
    w_out_spec = pl.BlockSpec((1, de, dm), lambda j, te, tb, tv, na, perm: (te[j], 0, 0))
    grid_spec = pltpu.PrefetchScalarGridSpec(
        num_scalar_prefetch=5,
        grid=(t_max,),
        in_specs=[pl.BlockSpec(memory_space=pl.ANY), w_in_spec, w_in_spec, w_out_spec],
        out_specs=pl.BlockSpec(memory_space=pl.ANY),
        scratch_shapes=[
            pltpu.VMEM((2, tile, dm), F32), pltpu.VMEM((2, tile, dm), F32),
            pltpu.VMEM((dm, de), BF16), pltpu.VMEM((dm, de), BF16), pltpu.VMEM((de, dm), BF16),
            pltpu.SemaphoreType.DMA((2,)), pltpu.SemaphoreType.DMA((2,)),
        ],
    )
    return pl.pallas_call(
        functools.partial(_expert_kernel, n_tok=n_tok),
        grid_spec=grid_spec,
        out_shape=jax.ShapeDtypeStruct((2 * n_tok, dm), F32),
        compiler_params=pltpu.CompilerParams(dimension_semantics=("arbitrary",), vmem_limit_bytes=48 * MIB),
        name="moe_experts",
    )(te, tb, tv, n_act, perm, x1, w_gate, w_up, w_down)


def _combine_kernel(x_ref, ya_ref, yb_ref, r_ref, g_ref, b_ref, o_ref, *, alpha):
    r = r_ref[...]
    moe = r[:, ROUTE_W1:ROUTE_W1 + 1] * ya_ref[...] + r[:, ROUTE_W2:ROUTE_W2 + 1] * yb_ref[...]
    o_ref[...] = _layer_norm(alpha * x_ref[...] + moe, g_ref[...], b_ref[...])


def _combine(x1, y_pairs, route, ln_g, ln_b, *, alpha):
    n, dm = x1.shape
    tm = _tile(n, 512)
    nb = n // tm
    const = lambda shape: pl.BlockSpec(shape, lambda i: (0, 0))
    return pl.pallas_call(
        functools.partial(_combine_kernel, alpha=alpha),
        grid=(nb,),
        in_specs=[
            pl.BlockSpec((tm, dm), lambda i: (i, 0)),
            pl.BlockSpec((tm, dm), lambda i: (i, 0)),
            pl.BlockSpec((tm, dm), lambda i: (nb + i, 0)),
            pl.BlockSpec((tm, LANES), lambda i: (i, 0)),
            const((1, dm)), const((1, dm)),
        ],
        out_specs=pl.BlockSpec((tm, dm), lambda i: (i, 0)),
        out_shape=jax.ShapeDtypeStruct((n, dm), F32),
        compiler_params=pltpu.CompilerParams(dimension_semantics=("parallel",), vmem_limit_bytes=48 * MIB),
        name="moe_combine_ln2",
    )(x1, y_pairs, y_pairs, route, ln_g, ln_b)


def _layer(x, conv_state, c0, n0, m0, past_k, past_v, p, *, chunk_pref):
    batch, seq, dm = x.shape
    n_tok = batch * seq
    xf = x.reshape(n_tok, dm)
    n_heads, d = p["n_heads"], p["head_dim"]
    da = n_heads * d
    cols = p["cols"]

    proj = _project(xf, p["w_main"], F32)
    gates = _project(xf, p["w_if"], F32)
    q_s = _project(xf, p["w_qs"], BF16)
    k_hm = _project(xf, p["w_ks"], F32, head_major=True)
    v_hm = _project(xf, p["w_vs"], F32, head_major=True)

    chunk = _tile(seq, chunk_pref)
    h_a, c_new, n_new, m_new = _mlstm(
        proj, gates, p["b_gate_row"], p["conv_w"], p["conv_b"], conv_state, c0, n0, m0,
        batch=batch, seq=seq, chunk=chunk, n_heads=n_heads, head_dim=d,
        q_col=cols["q"], k_col=cols["k"], v_col=cols["v"])

    if past_k is None:
        assert batch == 1
        h_s = _sb_prompt(q_s, k_hm, v_hm, seq=seq)
    else:
        h_s = _sb_decode(q_s, k_hm, v_hm, past_k, past_v, batch=batch, seq=seq)

    x1, route = _merge(xf, h_a, proj, h_s, p["w_bm"], p["w_bs"], p["w_out"], p["ln1_g"], p["ln1_b"],
                       p["w_router"], p["b_router"], o_col=cols["o"], gm_col=cols["gm"], gs_col=cols["gs"],
                       alpha=p["alpha"], n_experts=p["n_experts"], n_groups=p["n_groups"])
    y_pairs = _experts(x1, route, p["w_eg"], p["w_eu"], p["w_ed"])
    x2 = _combine(x1, y_pairs, route, p["ln2_g"], p["ln2_b"], alpha=p["alpha"])

    n_sb = k_hm.shape[0]
    sb_k = k_hm.reshape(n_sb, batch, seq, -1).transpose(1, 0, 2, 3)
    sb_v = v_hm.reshape(n_sb, batch, seq, -1).transpose(1, 0, 2, 3)
    new_conv = proj.reshape(batch, seq, -1)[:, seq - (CONV_W - 1):, cols["q"]:cols["q"] + 2 * da]
    return x2.reshape(batch, seq, dm), (sb_k, sb_v, c_new, n_new, m_new, new_conv)


def _prepare_layer(l, depth, w_in, b_gate, conv_w, conv_b, w_branch_m, w_branch_s, w_out, ln1_g, ln1_b,
                   w_router_group, b_router_group, w_router_expert, b_router_expert,
                   w_exp_gate, w_exp_up, w_exp_down, ln2_g, ln2_b, head_dim):
    dm = w_in.shape[1]
    da2 = conv_w.shape[-1]
    da = da2 // 2
    n_heads = b_gate.shape[-1] // 2
    ds = w_branch_s.shape[1]
    sizes = (da2, da, da, 2 * n_heads, ds, ds, ds, dm, dm)
    offs = [0]
    for s in sizes:
        offs.append(offs[-1] + s)
    w = w_in[l]
    part = lambda i: w[:, offs[i]:offs[i + 1]]
    w_main = jnp.concatenate([part(0), part(1), part(2), part(7), part(8)], axis=1).astype(BF16)
    cols = {"q": 0, "k": da, "v": da2, "o": da2 + da, "gm": da2 + 2 * da, "gs": da2 + 2 * da + dm}
    assert cols["o"] % da == 0 and cols["gm"] % dm == 0 and cols["gs"] % dm == 0
    pad_if = LANES - 2 * n_heads
    w_if = jnp.pad(part(3), ((0, 0), (0, pad_if))).astype(BF16)
    b_gate_row = jnp.pad(b_gate[l].astype(F32), (0, pad_if)).reshape(1, LANES)
    n_groups = w_router_group.shape[-1]
    n_experts = w_router_expert.shape[-1]
    pad_r = LANES - n_experts - n_groups
    w_router = jnp.pad(jnp.concatenate([w_router_expert[l], w_router_group[l]], axis=1), ((0, 0), (0, pad_r)))
    b_router = jnp.pad(jnp.concatenate([b_router_expert[l], b_router_group[l]]), (0, pad_r)).reshape(1, LANES)
    return {
        "n_heads": n_heads, "head_dim": head_dim, "cols": cols, "n_groups": n_groups, "n_experts": n_experts,
        "alpha": (2.0 * depth) ** 0.25,
        "w_main": w_main, "w_if": w_if, "b_gate_row": b_gate_row,
        "w_qs": part(4).astype(BF16), "w_ks": part(5).astype(BF16), "w_vs": part(6).astype(BF16),
        "conv_w": conv_w[l], "conv_b": conv_b[l],
        "w_bm": w_branch_m[l].astype(BF16), "w_bs": w_branch_s[l].astype(BF16), "w_out": w_out[l].astype(BF16),
        "ln1_g": ln1_g[l].reshape(1, dm), "ln1_b": ln1_b[l].reshape(1, dm),
        "w_router": w_router.astype(F32), "b_router": b_router.astype(F32),
        "w_eg": w_exp_gate[l], "w_eu": w_exp_up[l], "w_ed": w_exp_down[l],
        "ln2_g": ln2_g[l].reshape(1, dm), "ln2_b": ln2_b[l].reshape(1, dm),
    }


def kernel(x_prompt, x_sample, cache_sb_k, cache_sb_v, state_mlstm_C, state_mlstm_n, state_mlstm_m, state_conv,
           w_in, b_gate, conv_w, conv_b, w_branch_m, w_branch_s, w_out, ln1_g, ln1_b,
           w_router_group, b_router_group, w_router_expert, b_router_expert,
           w_exp_gate, w_exp_up, w_exp_down, ln2_g, ln2_b):
    depth = w_in.shape[0]
    bp = x_prompt.shape[0]
    n_heads, head_dim = state_mlstm_C.shape[2], state_mlstm_C.shape[3]
    da2 = conv_w.shape[-1]
    xp, xs = x_prompt, x_sample
    st_p_all, st_s_all = [], []
    for l in range(depth):
        p = _prepare_layer(l, depth, w_in, b_gate, conv_w, conv_b, w_branch_m, w_branch_s, w_out, ln1_g, ln1_b,
                           w_router_group, b_router_group, w_router_expert, b_router_expert,
                           w_exp_gate, w_exp_up, w_exp_down, ln2_g, ln2_b, head_dim)
        xp, st_p = _layer(
            xp, jnp.zeros((bp, CONV_W - 1, da2), F32), jnp.zeros((bp, n_heads, head_dim, head_dim), F32),
            jnp.zeros((bp, n_heads, head_dim), F32), jnp.full((bp, n_heads), NEG, F32), None, None, p,
            chunk_pref=256)
        xs, st_s = _layer(xs, state_conv[l], state_mlstm_C[l], state_mlstm_n[l], state_mlstm_m[l],
                          cache_sb_k[l], cache_sb_v[l], p, chunk_pref=256)
        st_p_all.append(st_p)
        st_s_all.append(st_s)
    stack = lambda states, i: jnp.stack([s[i] for s in states], axis=0)
    return (xp, xs,
            *(stack(st_p_all, i) for i in range(6)),
            *(stack(st_s_all, i) for i in range(6)))
```

```python
import functools
import math

import jax
import jax.numpy as jnp
from jax import lax
from jax.experimental import pallas as pl
from jax.experimental.pallas import tpu as pltpu

F32 = jnp.float32
BF16 = jnp.bfloat16

LANES = 128
SUBLANES = 8
LN_EPS = 1e-5
NEG = -1e30
CONV_W = 4
EXP_ZERO_BELOW = -104.0
MIB = 1024 * 1024

_NT = (((1,), (1,)), ((), ()))
_TN = (((0,), (0,)), ((), ()))


def _tile(n, pref):
    if n <= pref:
        return n
    t = pref
    while t >= SUBLANES:
        if n % t == 0 and t % SUBLANES == 0:
            return t
        t -= SUBLANES
    return n


def _log_sigmoid(x):
    return jnp.minimum(x, 0.0) - jnp.log1p(jnp.exp(-jnp.abs(x)))


def _layer_norm(x, g, b):
    mu = jnp.mean(x, axis=-1, keepdims=True)
    d = x - mu
    var = jnp.mean(d * d, axis=-1, keepdims=True)
    return d * lax.rsqrt(var + LN_EPS) * g + b


def _bf16_pieces(x, n):
    pieces = []
    rem = x
    for _ in range(n - 1):
        p = rem.astype(BF16)
        pieces.append(p)
        rem = rem - p.astype(F32)
    pieces.append(rem.astype(BF16))
    return pieces


def _dot_pieces(pieces, other, *, pieces_first, dims=None):
    total = None
    for p in pieces:
        a, b = (p, other) if pieces_first else (other, p)
        if dims is None:
            t = jnp.dot(a, b, preferred_element_type=F32)
        else:
            t = lax.dot_general(a, b, dims, preferred_element_type=F32)
        total = t if total is None else total + t
    return total


def _proj_kernel(x_ref, w_ref, main_ref, qs_ref, k32_ref, k16_ref, v32_ref, v16_ref, g_ref, xb_ref, *, ends):
    j = pl.program_id(1)
    end_main, end_qs, end_k, end_v = ends

    @pl.when(j == 0)
    def _():
        xb_ref[...] = x_ref[...].astype(BF16)

    acc = jnp.dot(xb_ref[...], w_ref[...], preferred_element_type=F32)

    def store_heads(f32_ref, bf16_ref):
        for h in range(f32_ref.shape[0]):
            piece = acc[:, h * LANES:(h + 1) * LANES]
            f32_ref[h] = piece
            bf16_ref[h] = piece.astype(BF16)

    @pl.when(j < end_main)
    def _():
        main_ref[...] = acc

    @pl.when(jnp.logical_and(j >= end_main, j < end_qs))
    def _():
        qs_ref[...] = acc.astype(BF16)

    @pl.when(jnp.logical_and(j >= end_qs, j < end_k))
    def _():
        store_heads(k32_ref, k16_ref)

    @pl.when(jnp.logical_and(j >= end_k, j < end_v))
    def _():
        store_heads(v32_ref, v16_ref)

    @pl.when(j >= end_v)
    def _():
        g_ref[...] = acc[:, :LANES]


def _project(x, w_all, *, main_cols, sb_cols, tn):
    n, k = x.shape
    tm = _tile(n, 1024)
    t_main, t_sb = main_cols // tn, sb_cols // tn
    ends = (t_main, t_main + t_sb, t_main + 2 * t_sb, t_main + 3 * t_sb)
    n_tiles = ends[-1] + 1
    assert w_all.shape[1] == n_tiles * tn
    local = lambda j, start, count: jnp.clip(j - start, 0, count - 1)
    heads = tn // LANES
    n_sb = sb_cols // LANES
    hm_spec = lambda start: pl.BlockSpec((heads, tm, LANES), lambda i, j: (local(j, start, t_sb), i, 0))
    return pl.pallas_call(
        functools.partial(_proj_kernel, ends=ends),
        grid=(n // tm, n_tiles),
        in_specs=[pl.BlockSpec((tm, k), lambda i, j: (i, 0), pipeline_mode=pl.Buffered(1)),
                  pl.BlockSpec((k, tn), lambda i, j: (0, j))],
        out_specs=[
            pl.BlockSpec((tm, tn), lambda i, j: (i, local(j, 0, t_main))),
            pl.BlockSpec((tm, tn), lambda i, j: (i, local(j, ends[0], t_sb))),
            hm_spec(ends[1]), hm_spec(ends[1]), hm_spec(ends[2]), hm_spec(ends[2]),
            pl.BlockSpec((tm, LANES), lambda i, j: (i, 0)),
        ],
        out_shape=[
            jax.ShapeDtypeStruct((n, main_cols), F32),
            jax.ShapeDtypeStruct((n, sb_cols), BF16),
            jax.ShapeDtypeStruct((n_sb, n, LANES), F32), jax.ShapeDtypeStruct((n_sb, n, LANES), BF16),
            jax.ShapeDtypeStruct((n_sb, n, LANES), F32), jax.ShapeDtypeStruct((n_sb, n, LANES), BF16),
            jax.ShapeDtypeStruct((n, LANES), F32),
        ],
        scratch_shapes=[pltpu.VMEM((tm, k), BF16)],
        compiler_params=pltpu.CompilerParams(
            dimension_semantics=("parallel", "arbitrary"),
            vmem_limit_bytes=48 * MIB),
        name="in_proj",
    )(x, w_all)


def _mlstm_kernel(q_ref, k_ref, v_ref, g_ref, bg_ref, cwq_ref, cwk_ref, cbq_ref, cbk_ref,
                  csq_ref, csk_ref, c0_ref, n0_ref, m0_ref,
                  h_ref, c_ref, n_ref, m_ref, qext_ref, kext_ref, *, chunk, n_heads, head_dim):
    step = pl.program_id(1)
    L, d = chunk, head_dim
    halo = CONV_W - 1
    base = SUBLANES - halo

    @pl.when(step == 0)
    def _():
        qext_ref[base:SUBLANES, :] = csq_ref[0]
        kext_ref[base:SUBLANES, :] = csk_ref[0]
        c_ref[...] = c0_ref[...]
        n_ref[...] = n0_ref[...]
        m_ref[...] = m0_ref[...]

    def conv_silu(raw_ref, ext_ref, cw_ref, cb_ref):
        ext_ref[SUBLANES:SUBLANES + L, :] = raw_ref[...]
        acc = cb_ref[...] + ext_ref[base:base + L, :] * cw_ref[0:1, :]
        for j in range(1, CONV_W):
            acc = acc + ext_ref[base + j:base + j + L, :] * cw_ref[j:j + 1, :]
        ext_ref[base:SUBLANES, :] = ext_ref[base + L:SUBLANES + L, :]
        return acc * jax.nn.sigmoid(acc)

    q_all = conv_silu(q_ref, qext_ref, cwq_ref, cbq_ref)
    k_all = conv_silu(k_ref, kext_ref, cwk_ref, cbk_ref) * (float(d) ** -0.5)

    lane = lax.broadcasted_iota(jnp.int32, (L, LANES), 1)
    pre = g_ref[...] + bg_ref[...]
    gates = jnp.where(lane < n_heads, pre, _log_sigmoid(pre))
    row_i = lax.broadcasted_iota(jnp.int32, (L, L), 0)
    col_i = lax.broadcasted_iota(jnp.int32, (L, L), 1)
    causal = col_i <= row_i
    tril = causal.astype(BF16)
    triu = (row_i <= col_i).astype(BF16)
    sel = (lax.broadcasted_iota(jnp.int32, (SUBLANES, LANES), 0)
           == lax.broadcasted_iota(jnp.int32, (SUBLANES, LANES), 1)).astype(BF16)
    gate_pieces = _bf16_pieces(gates, 3)
    cum_cols = _dot_pieces(gate_pieces, tril, pieces_first=False)
    gates_t = _dot_pieces(gate_pieces, sel, pieces_first=False, dims=_NT)
    cum_rows = _dot_pieces(_bf16_pieces(gates_t, 3), triu, pieces_first=True)

    for h in range(n_heads):
        cols = slice(h * d, (h + 1) * d)
        q = q_all[:, cols]
        k = k_all[:, cols]
        q_b = q.astype(BF16)
        k_b = k.astype(BF16)
        v_b = v_ref[:, cols].astype(BF16)
        i_col = gates[:, h:h + 1]
        b_col = cum_cols[:, n_heads + h:n_heads + h + 1]
        i_row = gates_t[h:h + 1, :]
        b_row = cum_rows[n_heads + h:n_heads + h + 1, :]
        c_prev = c_ref[0, h]
        n_prev = n_ref[0, h]
        m_prev = m_ref[0, h][:, 0:1]

        dmat = jnp.where(causal, b_col - b_row + i_row, NEG)
        inter = b_col + m_prev
        m_t = jnp.maximum(jnp.max(dmat, axis=-1, keepdims=True), inter)
        w = jnp.exp(dmat - m_t)
        s = lax.dot_general(q_b, k_b, _NT, preferred_element_type=F32) * w
        e_inter = jnp.exp(inter - m_t)
        num = (jnp.dot(s.astype(BF16), v_b, preferred_element_type=F32)
               + e_inter * jnp.dot(q_b, c_prev.astype(BF16), preferred_element_type=F32))
        den = jnp.sum(s, axis=-1, keepdims=True) + e_inter * jnp.sum(q * n_prev, axis=-1, keepdims=True)
        h_ref[:, cols] = num / jnp.maximum(jnp.abs(den), jnp.exp(-m_t))

        g_last = b_col[L - 1:L, :]
        w_last = g_last - b_col + i_col
        m_new = jnp.maximum(g_last + m_prev, jnp.max(w_last, axis=0, keepdims=True))
        decay = jnp.exp(g_last + m_prev - m_new)
        kw = jnp.exp(w_last - m_new) * k
        c_ref[0, h] = decay * c_prev + lax.dot_general(kw.astype(BF16), v_b, _TN, preferred_element_type=F32)
        n_ref[0, h] = decay * n_prev + jnp.sum(kw, axis=0, keepdims=True)
        m_ref[0, h] = jnp.broadcast_to(m_new, (1, LANES))


def _mlstm(proj, gates, b_gate_row, conv_w, conv_b, conv_state, c0, n0, m0, *, batch, seq, chunk,
           n_heads, head_dim, q_col, k_col, v_col):
    n_tok = batch * seq
    nc = seq // chunk
    d = head_dim
    da = n_heads * d
    qb, kb, vb = q_col // da, k_col // da, v_col // da
    kcw = (k_col - q_col) // da
    row = lambda b, c: b * nc + c
    n0r = n0.reshape(batch, n_heads, 1, d)
    m0r = jnp.broadcast_to(m0.reshape(batch, n_heads, 1, 1), (batch, n_heads, 1, LANES))
    cbr = conv_b.reshape(1, -1)
    state_spec = lambda shape: pl.BlockSpec(shape, lambda b, c: (b, 0, 0, 0))
    h, c_new, n_new, m_new = pl.pallas_call(
        functools.partial(_mlstm_kernel, chunk=chunk, n_heads=n_heads, head_dim=d),
        grid=(batch, nc),
        in_specs=[
            pl.BlockSpec((chunk, da), lambda b, c: (row(b, c), qb)),
            pl.BlockSpec((chunk, da), lambda b, c: (row(b, c), kb)),
            pl.BlockSpec((chunk, da), lambda b, c: (row(b, c), vb)),
            pl.BlockSpec((chunk, LANES), lambda b, c: (row(b, c), 0)),
            pl.BlockSpec((1, LANES), lambda b, c: (0, 0)),
            pl.BlockSpec((CONV_W, da), lambda b, c: (0, 0)),
            pl.BlockSpec((CONV_W, da), lambda b, c: (0, kcw)),
            pl.BlockSpec((1, da), lambda b, c: (0, 0)),
            pl.BlockSpec((1, da), lambda b, c: (0, kcw)),
            pl.BlockSpec((1, CONV_W - 1, da), lambda b, c: (b, 0, 0)),
            pl.BlockSpec((1, CONV_W - 1, da), lambda b, c: (b, 0, kcw)),
            state_spec((1, n_heads, d, d)),
            state_spec((1, n_heads, 1, d)),
            state_spec((1, n_heads, 1, LANES)),
        ],
        out_specs=[
            pl.BlockSpec((chunk, da), lambda b, c: (row(b, c), 0)),
            state_spec((1, n_heads, d, d)),
            state_spec((1, n_heads, 1, d)),
            state_spec((1, n_heads, 1, LANES)),
        ],
        out_shape=[
            jax.ShapeDtypeStruct((n_tok, da), F32),
            jax.ShapeDtypeStruct((batch, n_heads, d, d), F32),
            jax.ShapeDtypeStruct((batch, n_heads, 1, d), F32),
            jax.ShapeDtypeStruct((batch, n_heads, 1, LANES), F32),
        ],
        scratch_shapes=[pltpu.VMEM((chunk + SUBLANES, da), F32), pltpu.VMEM((chunk + SUBLANES, da), F32)],
        compiler_params=pltpu.CompilerParams(
            dimension_semantics=("parallel", "arbitrary"),
            vmem_limit_bytes=32 * MIB),
        name="mlstm",
    )(proj, proj, proj, gates, b_gate_row, conv_w, conv_w, cbr, cbr, conv_state, conv_state, c0, n0r, m0r)
    return h, c_new, n_new.reshape(batch, n_heads, d), m_new[:, :, 0, 0]


def _sb_block(q_b, k_b, v_b, carry, acc, *, scale, diagonal):
    tq, tk = q_b.shape[0], k_b.shape[0]
    z = lax.dot_general(q_b, k_b, _NT, preferred_element_type=F32) * scale
    softplus = jnp.maximum(z, 0.0) + jnp.log(1.0 + jnp.exp(-jnp.abs(z)))
    log_beta = z - softplus
    drop = softplus
    if diagonal:
        mask = (lax.broadcasted_iota(jnp.int32, (tq, tk), 1) < lax.broadcasted_iota(jnp.int32, (tq, tk), 0))
        drop = jnp.where(mask, softplus, 0.0)
    later_mat = (lax.broadcasted_iota(jnp.int32, (tk, tk), 0)
                 > lax.broadcasted_iota(jnp.int32, (tk, tk), 1)).astype(BF16)
    later = _dot_pieces(_bf16_pieces(drop, 2), later_mat, pieces_first=True)
    a = jnp.exp(log_beta - (later + carry))
    if diagonal:
        a = jnp.where(mask, a, 0.0)
    acc = acc + jnp.dot(a.astype(BF16), v_b, preferred_element_type=F32)
    carry = carry + (later[:, 0:1] + drop[:, 0:1])
    return carry, acc


def _sb_sweep(qs, k_ats, v_ats, first_block, carries, accs, *, tk, scale):
    def carry_min(cs):
        m = cs[0]
        for c in cs[1:]:
            m = jnp.minimum(m, c)
        return jnp.min(m)

    def cond(state):
        j, cmin, _, _ = state
        return jnp.logical_and(j >= 0, cmin < -EXP_ZERO_BELOW)

    def body(state):
        j, _, cs, acs = state
        start = pl.multiple_of(j * tk, tk)
        new = [_sb_block(q, k_at[pl.ds(start, tk), :].astype(BF16), v_at[pl.ds(start, tk), :].astype(BF16),
                         c, a, scale=scale, diagonal=False)
               for q, k_at, v_at, c, a in zip(qs, k_ats, v_ats, cs, acs)]
        cs = tuple(n[0] for n in new)
        return j - 1, carry_min(cs), cs, tuple(n[1] for n in new)

    carries = tuple(carries)
    _, _, _, accs = lax.while_loop(cond, body, (first_block, carry_min(carries), carries, tuple(accs)))
    return accs


def _sb_prompt_kernel(q_ref, k_ref, v_ref, o_ref, *, tq, scale):
    i = pl.program_id(1)
    n_heads, _, d = k_ref.shape
    start = pl.multiple_of(i * tq, tq)
    qs, carries, accs = [], [], []
    for h in range(n_heads):
        q_b = q_ref[:, h * d:(h + 1) * d]
        carry, acc = _sb_block(q_b, k_ref[h, pl.ds(start, tq), :], v_ref[h, pl.ds(start, tq), :],
                               jnp.zeros((tq, 1), F32), jnp.zeros((tq, d), F32), scale=scale, diagonal=True)
        qs.append(q_b)
        carries.append(carry)
        accs.append(acc)
    accs = _sb_sweep(qs, [k_ref.at[h] for h in range(n_heads)], [v_ref.at[h] for h in range(n_heads)],
                     i - 1, carries, accs, tk=tq, scale=scale)
    for h in range(n_heads):
        o_ref[:, h * d:(h + 1) * d] = accs[h].astype(o_ref.dtype)


def _sb_prompt(q, k_hm, v_hm, *, seq, heads_per_step=2):
    n_heads, _, d = k_hm.shape
    hb = heads_per_step if n_heads % heads_per_step == 0 else 1
    tq = _tile(seq, 256)
    kv_spec = pl.BlockSpec((hb, seq, d), lambda g, i: (g, 0, 0))
    return pl.pallas_call(
        functools.partial(_sb_prompt_kernel, tq=tq, scale=float(d) ** -0.5),
        grid=(n_heads // hb, seq // tq),
        in_specs=[pl.BlockSpec((tq, hb * d), lambda g, i: (i, g)), kv_spec, kv_spec],
        out_specs=pl.BlockSpec((tq, hb * d), lambda g, i: (i, g)),
        out_shape=jax.ShapeDtypeStruct((seq, n_heads * d), BF16),
        compiler_params=pltpu.CompilerParams(
            dimension_semantics=("parallel", "arbitrary"),
            vmem_limit_bytes=48 * MIB),
        name="sb_prompt",
    )(q, k_hm, v_hm)


def _sb_decode_kernel(q_ref, kn_ref, vn_ref, kp_ref, vp_ref, o_ref, *, tk, scale):
    n_heads, tq, d = kn_ref.shape
    qs, carries, accs = [], [], []
    for h in range(n_heads):
        q_b = q_ref[:, h * d:(h + 1) * d]
        carry, acc = _sb_block(q_b, kn_ref[h], vn_ref[h], jnp.zeros((tq, 1), F32), jnp.zeros((tq, d), F32),
                               scale=scale, diagonal=True)
        qs.append(q_b)
        carries.append(carry)
        accs.append(acc)
    past_len = kp_ref.shape[2]
    accs = _sb_sweep(qs, [kp_ref.at[0, h] for h in range(n_heads)], [vp_ref.at[0, h] for h in range(n_heads)],
                     past_len // tk - 1, carries, accs, tk=tk, scale=scale)
    for h in range(n_heads):
        o_ref[:, h * d:(h + 1) * d] = accs[h].astype(o_ref.dtype)


def _sb_decode(q, k_hm, v_hm, past_k, past_v, *, batch, seq, heads_per_step=4):
    n_heads, _, d = k_hm.shape
    hb = heads_per_step if n_heads % heads_per_step == 0 else 1
    past_len = past_k.shape[2]
    tk = _tile(past_len, 256)
    new_spec = pl.BlockSpec((hb, seq, d), lambda b, g: (g, b, 0))
    past_spec = pl.BlockSpec((1, hb, past_len, d), lambda b, g: (b, g, 0, 0))
    return pl.pallas_call(
        functools.partial(_sb_decode_kernel, tk=tk, scale=float(d) ** -0.5),
        grid=(batch, n_heads // hb),
        in_specs=[pl.BlockSpec((seq, hb * d), lambda b, g: (b, g)), new_spec, new_spec, past_spec, past_spec],
        out_specs=pl.BlockSpec((seq, hb * d), lambda b, g: (b, g)),
        out_shape=jax.ShapeDtypeStruct((batch * seq, n_heads * d), BF16),
        compiler_params=pltpu.CompilerParams(dimension_semantics=("parallel", "parallel")),
        name="sb_decode",
    )(q, k_hm, v_hm, past_k, past_v)


ROUTE_E1, ROUTE_E2, ROUTE_W1, ROUTE_W2 = 0, 1, 2, 3


def _route(logits, *, n_experts, n_groups):
    per_group = n_experts // n_groups
    lane = lax.broadcasted_iota(jnp.int32, logits.shape, 1)
    big = jnp.int32(LANES)
    is_group = jnp.logical_and(lane >= n_experts, lane < n_experts + n_groups)
    lg = jnp.where(is_group, logits, -jnp.inf)
    lg_max = jnp.max(lg, axis=-1, keepdims=True)
    p_sel = 1.0 / jnp.sum(jnp.exp(lg - lg_max), axis=-1, keepdims=True)
    grp = jnp.min(jnp.where(lg == lg_max, lane - n_experts, big), axis=-1, keepdims=True)
    in_grp = jnp.logical_and(lane >= grp * per_group, lane < (grp + 1) * per_group)
    le = jnp.where(in_grp, logits, -jnp.inf)
    v1 = jnp.max(le, axis=-1, keepdims=True)
    i1 = jnp.min(jnp.where(le == v1, lane, big), axis=-1, keepdims=True)
    le2 = jnp.where(lane == i1, -jnp.inf, le)
    v2 = jnp.max(le2, axis=-1, keepdims=True)
    i2 = jnp.min(jnp.where(le2 == v2, lane, big), axis=-1, keepdims=True)
    e2 = jnp.exp(v2 - v1)
    w1 = p_sel / (1.0 + e2)
    w2 = p_sel * e2 / (1.0 + e2)
    rec = jnp.where(lane == ROUTE_E1, i1.astype(F32), 0.0)
    rec = jnp.where(lane == ROUTE_E2, i2.astype(F32), rec)
    rec = jnp.where(lane == ROUTE_W1, w1, rec)
    return jnp.where(lane == ROUTE_W2, w2, rec)


def _merge_kernel(x_ref, ha_ref, om_ref, hs_ref, gm_ref, gs_ref, wbm_ref, wbs_ref, wout_ref, g_ref, b_ref,
                  wrh_ref, wrl_ref, br_ref, o_ref, route_ref, *, alpha, n_experts, n_groups):
    h_m = (jax.nn.sigmoid(om_ref[...]) * ha_ref[...]).astype(BF16)
    t_m = jnp.dot(h_m, wbm_ref[...], preferred_element_type=F32)
    t_s = jnp.dot(hs_ref[...], wbs_ref[...], preferred_element_type=F32)
    merged = jax.nn.sigmoid(gm_ref[...]) * t_m + jax.nn.sigmoid(gs_ref[...]) * t_s
    y = jnp.dot(merged.astype(BF16), wout_ref[...], preferred_element_type=F32)
    x1 = _layer_norm(alpha * x_ref[...] + y, g_ref[...], b_ref[...])
    o_ref[...] = x1
    x_hi, x_lo = _bf16_pieces(x1, 2)
    logits = (jnp.dot(x_hi, wrh_ref[...], preferred_element_type=F32)
              + (jnp.dot(x_hi, wrl_ref[...], preferred_element_type=F32)
                 + jnp.dot(x_lo, wrh_ref[...], preferred_element_type=F32))) + br_ref[...]
    route_ref[...] = _route(logits, n_experts=n_experts, n_groups=n_groups)


def _merge(x, h_a, proj, h_s, w_bm, w_bs, w_out, ln_g, ln_b, w_router_hi, w_router_lo, b_router, *,
           o_col, gm_col, gs_col, alpha, n_experts, n_groups):
    n, dm = x.shape
    da, ds = h_a.shape[1], h_s.shape[1]
    tm = _tile(n, 256)
    const = lambda shape: pl.BlockSpec(shape, lambda i: (0, 0), pipeline_mode=pl.Buffered(1))
    return pl.pallas_call(
        functools.partial(_merge_kernel, alpha=alpha, n_experts=n_experts, n_groups=n_groups),
        grid=(n // tm,),
        in_specs=[
            pl.BlockSpec((tm, dm), lambda i: (i, 0)),
            pl.BlockSpec((tm, da), lambda i: (i, 0)),
            pl.BlockSpec((tm, da), lambda i: (i, o_col // da)),
            pl.BlockSpec((tm, ds), lambda i: (i, 0)),
            pl.BlockSpec((tm, dm), lambda i: (i, gm_col // dm)),
            pl.BlockSpec((tm, dm), lambda i: (i, gs_col // dm)),
            const((da, dm)), const((ds, dm)), const((dm, dm)), const((1, dm)), const((1, dm)),
            const((dm, LANES)), const((dm, LANES)), const((1, LANES)),
        ],
        out_specs=[pl.BlockSpec((tm, dm), lambda i: (i, 0)), pl.BlockSpec((tm, LANES), lambda i: (i, 0))],
        out_shape=[jax.ShapeDtypeStruct((n, dm), F32), jax.ShapeDtypeStruct((n, LANES), F32)],
        compiler_params=pltpu.CompilerParams(dimension_semantics=("parallel",), vmem_limit_bytes=48 * MIB),
        name="merge_out_ln1",
    )(x, h_a, proj, h_s, proj, proj, w_bm, w_bs, w_out, ln_g, ln_b, w_router_hi, w_router_lo, b_router)


def _expert_kernel(te_ref, tb_ref, tv_ref, na_ref, perm_ref,
                   x_hbm, wg_ref, wu_ref, wd_ref, y_hbm,
                   xbuf, obuf, wgb, wub, wdb, gsem, ssem, *, n_tok):
    j = pl.program_id(0)
    n_act = na_ref[0]
    slot = lax.rem(j, 2)

    def gather_copy(s, i, token):
        return pltpu.make_async_copy(x_hbm.at[pl.ds(token, 1), :], xbuf.at[s, pl.ds(i, 1), :], gsem.at[s])

    def scatter_copy(s, i, row):
        return pltpu.make_async_copy(obuf.at[s, pl.ds(i, 1), :], y_hbm.at[pl.ds(row, 1), :], ssem.at[s])

    def for_rows(count, fn):
        groups = lax.shift_right_logical(count, 3)

        def group_body(g, c):
            for u in range(SUBLANES):
                fn(g * SUBLANES + u)
            return c

        def row_body(i, c):
            fn(i)
            return c

        lax.fori_loop(0, groups, group_body, 0)
        lax.fori_loop(groups * SUBLANES, count, row_body, 0)

    def start_gather(t, s):
        base = tb_ref[t]

        def start(i):
            pair = perm_ref[base + i]
            gather_copy(s, i, jnp.where(pair >= n_tok, pair - n_tok, pair)).start()

        for_rows(tv_ref[t], start)

    def start_scatter(t, s):
        base = tb_ref[t]
        for_rows(tv_ref[t], lambda i: scatter_copy(s, i, perm_ref[base + i]).start())

    def wait_rows(count, block_copy, row_copy):
        bulk = pl.multiple_of(lax.shift_right_logical(count, 3) * SUBLANES, SUBLANES)

        @pl.when(bulk > 0)
        def _():
            block_copy(pl.ds(0, bulk)).wait()

        def row_body(i, c):
            row_copy(i).wait()
            return c

        lax.fori_loop(bulk, count, row_body, 0)

    def wait_gather(t, s):
        wait_rows(tv_ref[t],
                  lambda rows: pltpu.make_async_copy(x_hbm.at[rows, :], xbuf.at[s, rows, :], gsem.at[s]),
                  lambda i: gather_copy(s, i, 0))

    def wait_scatter(t, s):
        wait_rows(tv_ref[t],
                  lambda rows: pltpu.make_async_copy(obuf.at[s, rows, :], y_hbm.at[rows, :], ssem.at[s]),
                  lambda i: scatter_copy(s, i, 0))

    @pl.when(j == 0)
    def _():
        xbuf[...] = jnp.zeros_like(xbuf)
        start_gather(0, 0)

    @pl.when(j + 1 < n_act)
    def _():
        start_gather(j + 1, 1 - slot)

    @pl.when(j < n_act)
    def _():
        wait_gather(j, slot)

        @pl.when(j >= 2)
        def _():
            wait_scatter(j - 2, slot)

        @pl.when(jnp.logical_or(j == 0, te_ref[j] != te_ref[jnp.maximum(j - 1, 0)]))
        def _():
            wgb[...] = wg_ref[0].astype(BF16)
            wub[...] = wu_ref[0].astype(BF16)
            wdb[...] = wd_ref[0].astype(BF16)

        x = xbuf[slot].astype(BF16)
        gate = jnp.dot(x, wgb[...], preferred_element_type=F32)
        up = jnp.dot(x, wub[...], preferred_element_type=F32)
        hid = (gate * jax.nn.sigmoid(gate) * up).astype(BF16)
        obuf[slot] = jnp.dot(hid, wdb[...], preferred_element_type=F32)
        start_scatter(j, slot)

        @pl.when(j == n_act - 1)
        def _():
            @pl.when(j >= 1)
            def _():
                wait_scatter(j - 1, 1 - slot)

            wait_scatter(j, slot)


def _expert_tables(route, n_experts, tile):
    n_tok = route.shape[0]
    i32 = jnp.int32
    e_flat = jnp.concatenate([route[:, ROUTE_E1], route[:, ROUTE_E2]]).astype(i32)
    perm = jnp.argsort(e_flat, stable=True).astype(i32)
    eids = jnp.arange(n_experts, dtype=i32)
    counts = jnp.sum((e_flat[:, None] == eids[None, :]).astype(i32), axis=0)
    tiles_e = (counts + (tile - 1)) // tile
    before = eids[None, :] < eids[:, None]
    pair_start = jnp.sum(jnp.where(before, counts[None, :], 0), axis=1)
    tile_start = jnp.sum(jnp.where(before, tiles_e[None, :], 0), axis=1)
    n_act = jnp.sum(tiles_e)
    t_max = (2 * n_tok) // tile + n_experts
    j = jnp.arange(t_max, dtype=i32)
    owner = jnp.logical_and(j[:, None] >= tile_start[None, :], j[:, None] < (tile_start + tiles_e)[None, :])
    pick = lambda table: jnp.sum(jnp.where(owner, table[None, :], 0), axis=1)
    active = j < n_act
    k = j - pick(tile_start)
    last_e = jnp.max(jnp.where(tiles_e > 0, eids, 0))
    te = jnp.where(active, pick(eids), last_e).astype(i32)
    tb = jnp.where(active, pick(pair_start) + k * tile, 0).astype(i32)
    tv = jnp.where(active, jnp.clip(pick(counts) - k * tile, 0, tile), 0).astype(i32)
    return te, tb, tv, n_act.reshape(1).astype(i32), perm, t_max


def _experts(x1, route, w_gate, w_up, w_down):
    n_tok, dm = x1.shape
    n_experts, _, de = w_gate.shape
    tile = 256
    te, tb, tv, n_act, perm, t_max = _expert_tables(route, n_experts, tile)
    w_in_spec = pl.BlockSpec((1, dm, de), lambda j, te, tb, tv, na, perm: (te[j], 0, 0))
    w_out_spec = pl.BlockSpec((1, de, dm), lambda j, te, tb, tv, na, perm: (te[j], 0, 0))
    grid_spec = pltpu.PrefetchScalarGridSpec(
        num_scalar_prefetch=5,
        grid=(t_max,),
        in_specs=[pl.BlockSpec(memory_space=pl.ANY), w_in_spec, w_in_spec, w_out_spec],
        out_specs=pl.BlockSpec(memory_space=pl.ANY),
        scratch_shapes=[
            pltpu.VMEM((2, tile, dm), F32), pltpu.VMEM((2, tile, dm), F32),
            pltpu.VMEM((dm, de), BF16), pltpu.VMEM((dm, de), BF16), pltpu.VMEM((de, dm), BF16),
            pltpu.SemaphoreType.DMA((2,)), pltpu.SemaphoreType.DMA((2,)),
        ],
    )
    return pl.pallas_call(
        functools.partial(_expert_kernel, n_tok=n_tok),
        grid_spec=grid_spec,
        out_shape=jax.ShapeDtypeStruct((2 * n_tok, dm), F32),
        compiler_params=pltpu.CompilerParams(dimension_semantics=("arbitrary",), vmem_limit_bytes=48 * MIB),
        name="moe_experts",
    )(te, tb, tv, n_act, perm, x1, w_gate, w_up, w_down)


def _combine_kernel(x_ref, ya_ref, yb_ref, r_ref, g_ref, b_ref, o_ref, *, alpha):
    r = r_ref[...]
    moe = r[:, ROUTE_W1:ROUTE_W1 + 1] * ya_ref[...] + r[:, ROUTE_W2:ROUTE_W2 + 1] * yb_ref[...]
    o_ref[...] = _layer_norm(alpha * x_ref[...] + moe, g_ref[...], b_ref[...])


def _combine(x1, y_pairs, route, ln_g, ln_b, *, alpha):
    n, dm = x1.shape
    tm = _tile(n, 512)
    nb = n // tm
    const = lambda shape: pl.BlockSpec(shape, lambda i: (0, 0))
    return pl.pallas_call(
        functools.partial(_combine_kernel, alpha=alpha),
        grid=(nb,),
        in_specs=[
            pl.BlockSpec((tm, dm), lambda i: (i, 0)),
            pl.BlockSpec((tm, dm), lambda i: (i, 0)),
            pl.BlockSpec((tm, dm), lambda i: (nb + i, 0)),
            pl.BlockSpec((tm, LANES), lambda i: (i, 0)),
            const((1, dm)), const((1, dm)),
        ],
        out_specs=pl.BlockSpec((tm, dm), lambda i: (i, 0)),
        out_shape=jax.ShapeDtypeStruct((n, dm), F32),
        compiler_params=pltpu.CompilerParams(dimension_semantics=("parallel",), vmem_limit_bytes=48 * MIB),
        name="moe_combine_ln2",
    )(x1, y_pairs, y_pairs, route, ln_g, ln_b)


def _layer(x, conv_state, c0, n0, m0, past_k, past_v, p, *, chunk_pref):
    batch, seq, dm = x.shape
    n_tok = batch * seq
    xf = x.reshape(n_tok, dm)
    n_heads, d = p["n_heads"], p["head_dim"]
    da = n_heads * d
    cols = p["cols"]

    proj, q_s, k_hm, k_hm16, v_hm, v_hm16, gates = _project(
        xf, p["w_all"], main_cols=p["main_cols"], sb_cols=p["sb_cols"], tn=p["tn"])

    chunk = _tile(seq, chunk_pref)
    h_a, c_new, n_new, m_new = _mlstm(
        proj, gates, p["b_gate_row"], p["conv_w"], p["conv_b"], conv_state, c0, n0, m0,
        batch=batch, seq=seq, chunk=chunk, n_heads=n_heads, head_dim=d,
        q_col=cols["q"], k_col=cols["k"], v_col=cols["v"])

    if past_k is None:
        assert batch == 1
        h_s = _sb_prompt(q_s, k_hm16, v_hm16, seq=seq)
    else:
        h_s = _sb_decode(q_s, k_hm16, v_hm16, past_k, past_v, batch=batch, seq=seq)

    x1, route = _merge(xf, h_a, proj, h_s, p["w_bm"], p["w_bs"], p["w_out"], p["ln1_g"], p["ln1_b"],
                       p["w_router_hi"], p["w_router_lo"], p["b_router"],
                       o_col=cols["o"], gm_col=cols["gm"], gs_col=cols["gs"],
                       alpha=p["alpha"], n_experts=p["n_experts"], n_groups=p["n_groups"])
    y_pairs = _experts(x1, route, p["w_eg"], p["w_eu"], p["w_ed"])
    x2 = _combine(x1, y_pairs, route, p["ln2_g"], p["ln2_b"], alpha=p["alpha"])

    n_sb = k_hm.shape[0]
    sb_k = k_hm.reshape(n_sb, batch, seq, -1).transpose(1, 0, 2, 3)
    sb_v = v_hm.reshape(n_sb, batch, seq, -1).transpose(1, 0, 2, 3)
    new_conv = proj.reshape(batch, seq, -1)[:, seq - (CONV_W - 1):, cols["q"]:cols["q"] + 2 * da]
    return x2.reshape(batch, seq, dm), (sb_k, sb_v, c_new, n_new, m_new, new_conv)


def _prepare_layer(l, depth, w_in, b_gate, conv_w, conv_b, w_branch_m, w_branch_s, w_out, ln1_g, ln1_b,
                   w_router_group, b_router_group, w_router_expert, b_router_expert,
                   w_exp_gate, w_exp_up, w_exp_down, ln2_g, ln2_b, head_dim):
    dm = w_in.shape[1]
    da2 = conv_w.shape[-1]
    da = da2 // 2
    n_heads = b_gate.shape[-1] // 2
    ds = w_branch_s.shape[1]
    sizes = (da2, da, da, 2 * n_heads, ds, ds, ds, dm, dm)
    offs = [0]
    for s in sizes:
        offs.append(offs[-1] + s)
    w = w_in[l]
    part = lambda i: w[:, offs[i]:offs[i + 1]]
    main_cols = da2 + 2 * da + 2 * dm
    tn = math.gcd(math.gcd(512, main_cols), ds)
    w_all = jnp.concatenate(
        [part(0), part(1), part(2), part(7), part(8), part(4), part(5), part(6),
         jnp.pad(part(3), ((0, 0), (0, tn - 2 * n_heads)))], axis=1).astype(BF16)
    cols = {"q": 0, "k": da, "v": da2, "o": da2 + da, "gm": da2 + 2 * da, "gs": da2 + 2 * da + dm}
    assert cols["k"] % da == 0 and cols["o"] % da == 0 and cols["gm"] % dm == 0 and cols["gs"] % dm == 0
    b_gate_row = jnp.pad(b_gate[l].astype(F32), (0, LANES - 2 * n_heads)).reshape(1, LANES)
    n_groups = w_router_group.shape[-1]
    n_experts = w_router_expert.shape[-1]
    pad_r = LANES - n_experts - n_groups
    w_router = jnp.pad(jnp.concatenate([w_router_expert[l], w_router_group[l]], axis=1),
                       ((0, 0), (0, pad_r))).astype(F32)
    w_router_hi = w_router.astype(BF16)
    w_router_lo = (w_router - w_router_hi.astype(F32)).astype(BF16)
    b_router = jnp.pad(jnp.concatenate([b_router_expert[l], b_router_group[l]]), (0, pad_r)).reshape(1, LANES)
    return {
        "n_heads": n_heads, "head_dim": head_dim, "cols": cols, "n_groups": n_groups, "n_experts": n_experts,
        "alpha": (2.0 * depth) ** 0.25, "main_cols": main_cols, "sb_cols": ds, "tn": tn,
        "w_all": w_all, "b_gate_row": b_gate_row,
        "conv_w": conv_w[l], "conv_b": conv_b[l],
        "w_bm": w_branch_m[l].astype(BF16), "w_bs": w_branch_s[l].astype(BF16), "w_out": w_out[l].astype(BF16),
        "ln1_g": ln1_g[l].reshape(1, dm), "ln1_b": ln1_b[l].reshape(1, dm),
        "w_router_hi": w_router_hi, "w_router_lo": w_router_lo, "b_router": b_router.astype(F32),
        "w_eg": w_exp_gate[l], "w_eu": w_exp_up[l], "w_ed": w_exp_down[l],
        "ln2_g": ln2_g[l].reshape(1, dm), "ln2_b": ln2_b[l].reshape(1, dm),
    }


def kernel(x_prompt, x_sample, cache_sb_k, cache_sb_v, state_mlstm_C, state_mlstm_n, state_mlstm_m, state_conv,
           w_in, b_gate, conv_w, conv_b, w_branch_m, w_branch_s, w_out, ln1_g, ln1_b,
           w_router_group, b_router_group, w_router_expert, b_router_expert,
           w_exp_gate, w_exp_up, w_exp_down, ln2_g, ln2_b):
    depth = w_in.shape[0]
    bp = x_prompt.shape[0]
    n_heads, head_dim = state_mlstm_C.shape[2], state_mlstm_C.shape[3]
    da2 = conv_w.shape[-1]
    xp, xs = x_prompt, x_sample
    st_p_all, st_s_all = [], []
    for l in range(depth):
        p = _prepare_layer(l, depth, w_in, b_gate, conv_w, conv_b, w_branch_m, w_branch_s, w_out, ln1_g, ln1_b,
                           w_router_group, b_router_group, w_router_expert, b_router_expert,
                           w_exp_gate, w_exp_up, w_exp_down, ln2_g, ln2_b, head_dim)
        xp, st_p = _layer(
            xp, jnp.zeros((bp, CONV_W - 1, da2), F32), jnp.zeros((bp, n_heads, head_dim, head_dim), F32),
            jnp.zeros((bp, n_heads, head_dim), F32), jnp.full((bp, n_heads), NEG, F32), None, None, p,
            chunk_pref=256)
        xs, st_s = _layer(xs, state_conv[l], state_mlstm_C[l], state_mlstm_n[l], state_mlstm_m[l],
                          cache_sb_k[l], cache_sb_v[l], p, chunk_pref=256)
        st_p_all.append(st_p)
        st_s_all.append(st_s)
    stack = lambda states, i: jnp.stack([s[i] for s in states], axis=0)
    return (xp, xs,
            *(stack(st_p_all, i) for i in range(6)),
            *(stack(st_s_all, i) for i in range(6)))
```

```python
import functools
import math

import jax
import jax.numpy as jnp
from jax import lax
from jax.experimental import pallas as pl
from jax.experimental.pallas import tpu as pltpu

F32 = jnp.float32
BF16 = jnp.bfloat16

LANES = 128
SUBLANES = 8
LN_EPS = 1e-5
NEG = -1e30
CONV_W = 4
EXP_ZERO_BELOW = -104.0
MIB = 1024 * 1024

_NT = (((1,), (1,)), ((), ()))
_TN = (((0,), (0,)), ((), ()))


def _tile(n, pref):
    if n <= pref:
        return n
    t = pref
    while t >= SUBLANES:
        if n % t == 0 and t % SUBLANES == 0:
            return t
        t -= SUBLANES
    return n


def _log_sigmoid(x):
    return jnp.minimum(x, 0.0) - jnp.log1p(jnp.exp(-jnp.abs(x)))


def _layer_norm(x, g, b):
    mu = jnp.mean(x, axis=-1, keepdims=True)
    d = x - mu
    var = jnp.mean(d * d, axis=-1, keepdims=True)
    return d * lax.rsqrt(var + LN_EPS) * g + b


def _bf16_pieces(x, n):
    pieces = []
    rem = x
    for _ in range(n - 1):
        p = rem.astype(BF16)
        pieces.append(p)
        rem = rem - p.astype(F32)
    pieces.append(rem.astype(BF16))
    return pieces


def _dot_pieces(pieces, other, *, pieces_first, dims=None):
    total = None
    for p in pieces:
        a, b = (p, other) if pieces_first else (other, p)
        if dims is None:
            t = jnp.dot(a, b, preferred_element_type=F32)
        else:
            t = lax.dot_general(a, b, dims, preferred_element_type=F32)
        total = t if total is None else total + t
    return total


def _proj_kernel(x_ref, w_ref, main_ref, qs_ref, k32_ref, k16_ref, v32_ref, v16_ref, g_ref, xb_ref, *, ends):
    j = pl.program_id(1)
    end_main, end_qs, end_k, end_v = ends

    @pl.when(j == 0)
    def _():
        xb_ref[...] = x_ref[...].astype(BF16)

    acc = jnp.dot(xb_ref[...], w_ref[...], preferred_element_type=F32)

    def store_heads(f32_ref, bf16_ref):
        for h in range(f32_ref.shape[0]):
            piece = acc[:, h * LANES:(h + 1) * LANES]
            f32_ref[h] = piece
            bf16_ref[h] = piece.astype(BF16)

    @pl.when(j < end_main)
    def _():
        main_ref[...] = acc

    @pl.when(jnp.logical_and(j >= end_main, j < end_qs))
    def _():
        qs_ref[...] = acc.astype(BF16)

    @pl.when(jnp.logical_and(j >= end_qs, j < end_k))
    def _():
        store_heads(k32_ref, k16_ref)

    @pl.when(jnp.logical_and(j >= end_k, j < end_v))
    def _():
        store_heads(v32_ref, v16_ref)

    @pl.when(j >= end_v)
    def _():
        g_ref[...] = acc[:, :LANES]


def _project(x, w_all, *, main_cols, sb_cols, tn):
    n, k = x.shape
    tm = _tile(n, 1024)
    t_main, t_sb = main_cols // tn, sb_cols // tn
    ends = (t_main, t_main + t_sb, t_main + 2 * t_sb, t_main + 3 * t_sb)
    n_tiles = ends[-1] + 1
    assert w_all.shape[1] == n_tiles * tn
    local = lambda j, start, count: jnp.clip(j - start, 0, count - 1)
    heads = tn // LANES
    n_sb = sb_cols // LANES
    hm_spec = lambda start: pl.BlockSpec((heads, tm, LANES), lambda i, j: (local(j, start, t_sb), i, 0))
    return pl.pallas_call(
        functools.partial(_proj_kernel, ends=ends),
        grid=(n // tm, n_tiles),
        in_specs=[pl.BlockSpec((tm, k), lambda i, j: (i, 0), pipeline_mode=pl.Buffered(1)),
                  pl.BlockSpec((k, tn), lambda i, j: (0, j))],
        out_specs=[
            pl.BlockSpec((tm, tn), lambda i, j: (i, local(j, 0, t_main))),
            pl.BlockSpec((tm, tn), lambda i, j: (i, local(j, ends[0], t_sb))),
            hm_spec(ends[1]), hm_spec(ends[1]), hm_spec(ends[2]), hm_spec(ends[2]),
            pl.BlockSpec((tm, LANES), lambda i, j: (i, 0)),
        ],
        out_shape=[
            jax.ShapeDtypeStruct((n, main_cols), F32),
            jax.ShapeDtypeStruct((n, sb_cols), BF16),
            jax.ShapeDtypeStruct((n_sb, n, LANES), F32), jax.ShapeDtypeStruct((n_sb, n, LANES), BF16),
            jax.ShapeDtypeStruct((n_sb, n, LANES), F32), jax.ShapeDtypeStruct((n_sb, n, LANES), BF16),
            jax.ShapeDtypeStruct((n, LANES), F32),
        ],
        scratch_shapes=[pltpu.VMEM((tm, k), BF16)],
        compiler_params=pltpu.CompilerParams(
            dimension_semantics=("parallel", "arbitrary"),
            vmem_limit_bytes=48 * MIB),
        name="in_proj",
    )(x, w_all)


def _mlstm_kernel(q_ref, k_ref, v_ref, g_ref, bg_ref, cwq_ref, cwk_ref, cbq_ref, cbk_ref,
                  csq_ref, csk_ref, c0_ref, n0_ref, m0_ref,
                  h_ref, c_ref, n_ref, m_ref, qext_ref, kext_ref, *, chunk, n_heads, head_dim):
    step = pl.program_id(1)
    L, d = chunk, head_dim
    halo = CONV_W - 1
    base = SUBLANES - halo

    @pl.when(step == 0)
    def _():
        qext_ref[base:SUBLANES, :] = csq_ref[0]
        kext_ref[base:SUBLANES, :] = csk_ref[0]
        c_ref[...] = c0_ref[...]
        n_ref[...] = n0_ref[...]
        m_ref[...] = m0_ref[...]

    def conv_silu(raw_ref, ext_ref, cw_ref, cb_ref):
        ext_ref[SUBLANES:SUBLANES + L, :] = raw_ref[...]
        acc = cb_ref[...] + ext_ref[base:base + L, :] * cw_ref[0:1, :]
        for j in range(1, CONV_W):
            acc = acc + ext_ref[base + j:base + j + L, :] * cw_ref[j:j + 1, :]
        ext_ref[base:SUBLANES, :] = ext_ref[base + L:SUBLANES + L, :]
        return acc * jax.nn.sigmoid(acc)

    q_all = conv_silu(q_ref, qext_ref, cwq_ref, cbq_ref)
    k_all = conv_silu(k_ref, kext_ref, cwk_ref, cbk_ref) * (float(d) ** -0.5)

    lane = lax.broadcasted_iota(jnp.int32, (L, LANES), 1)
    pre = g_ref[...] + bg_ref[...]
    gates = jnp.where(lane < n_heads, pre, _log_sigmoid(pre))
    row_i = lax.broadcasted_iota(jnp.int32, (L, L), 0)
    col_i = lax.broadcasted_iota(jnp.int32, (L, L), 1)
    causal = col_i <= row_i
    tril = causal.astype(BF16)
    triu = (row_i <= col_i).astype(BF16)
    sel = (lax.broadcasted_iota(jnp.int32, (SUBLANES, LANES), 0)
           == lax.broadcasted_iota(jnp.int32, (SUBLANES, LANES), 1)).astype(BF16)
    gate_pieces = _bf16_pieces(gates, 3)
    cum_cols = _dot_pieces(gate_pieces, tril, pieces_first=False)
    gates_t = _dot_pieces(gate_pieces, sel, pieces_first=False, dims=_NT)
    cum_rows = _dot_pieces(_bf16_pieces(gates_t, 3), triu, pieces_first=True)

    for h in range(n_heads):
        cols = slice(h * d, (h + 1) * d)
        q = q_all[:, cols]
        k = k_all[:, cols]
        q_b = q.astype(BF16)
        k_b = k.astype(BF16)
        v_b = v_ref[:, cols].astype(BF16)
        i_col = gates[:, h:h + 1]
        b_col = cum_cols[:, n_heads + h:n_heads + h + 1]
        i_row = gates_t[h:h + 1, :]
        b_row = cum_rows[n_heads + h:n_heads + h + 1, :]
        c_prev = c_ref[0, h]
        n_prev = n_ref[0, h]
        m_prev = m_ref[0, h][:, 0:1]

        dmat = jnp.where(causal, b_col - b_row + i_row, NEG)
        inter = b_col + m_prev
        m_t = jnp.maximum(jnp.max(dmat, axis=-1, keepdims=True), inter)
        w = jnp.exp(dmat - m_t)
        s = lax.dot_general(q_b, k_b, _NT, preferred_element_type=F32) * w
        e_inter = jnp.exp(inter - m_t)
        num = (jnp.dot(s.astype(BF16), v_b, preferred_element_type=F32)
               + e_inter * jnp.dot(q_b, c_prev.astype(BF16), preferred_element_type=F32))
        den = jnp.sum(s, axis=-1, keepdims=True) + e_inter * jnp.sum(q * n_prev, axis=-1, keepdims=True)
        h_ref[:, cols] = num / jnp.maximum(jnp.abs(den), jnp.exp(-m_t))

        g_last = b_col[L - 1:L, :]
        w_last = g_last - b_col + i_col
        m_new = jnp.maximum(g_last + m_prev, jnp.max(w_last, axis=0, keepdims=True))
        decay = jnp.exp(g_last + m_prev - m_new)
        kw = jnp.exp(w_last - m_new) * k
        c_ref[0, h] = decay * c_prev + lax.dot_general(kw.astype(BF16), v_b, _TN, preferred_element_type=F32)
        n_ref[0, h] = decay * n_prev + jnp.sum(kw, axis=0, keepdims=True)
        m_ref[0, h] = jnp.broadcast_to(m_new, (1, LANES))


def _mlstm(proj, gates, b_gate_row, conv_w, conv_b, conv_state, c0, n0, m0, *, batch, seq, chunk,
           n_heads, head_dim, q_col, k_col, v_col):
    n_tok = batch * seq
    nc = seq // chunk
    d = head_dim
    da = n_heads * d
    qb, kb, vb = q_col // da, k_col // da, v_col // da
    kcw = (k_col - q_col) // da
    row = lambda b, c: b * nc + c
    n0r = n0.reshape(batch, n_heads, 1, d)
    m0r = jnp.broadcast_to(m0.reshape(batch, n_heads, 1, 1), (batch, n_heads, 1, LANES))
    cbr = conv_b.reshape(1, -1)
    state_spec = lambda shape: pl.BlockSpec(shape, lambda b, c: (b, 0, 0, 0))
    h, c_new, n_new, m_new = pl.pallas_call(
        functools.partial(_mlstm_kernel, chunk=chunk, n_heads=n_heads, head_dim=d),
        grid=(batch, nc),
        in_specs=[
            pl.BlockSpec((chunk, da), lambda b, c: (row(b, c), qb)),
            pl.BlockSpec((chunk, da), lambda b, c: (row(b, c), kb)),
            pl.BlockSpec((chunk, da), lambda b, c: (row(b, c), vb)),
            pl.BlockSpec((chunk, LANES), lambda b, c: (row(b, c), 0)),
            pl.BlockSpec((1, LANES), lambda b, c: (0, 0)),
            pl.BlockSpec((CONV_W, da), lambda b, c: (0, 0)),
            pl.BlockSpec((CONV_W, da), lambda b, c: (0, kcw)),
            pl.BlockSpec((1, da), lambda b, c: (0, 0)),
            pl.BlockSpec((1, da), lambda b, c: (0, kcw)),
            pl.BlockSpec((1, CONV_W - 1, da), lambda b, c: (b, 0, 0)),
            pl.BlockSpec((1, CONV_W - 1, da), lambda b, c: (b, 0, kcw)),
            state_spec((1, n_heads, d, d)),
            state_spec((1, n_heads, 1, d)),
            state_spec((1, n_heads, 1, LANES)),
        ],
        out_specs=[
            pl.BlockSpec((chunk, da), lambda b, c: (row(b, c), 0)),
            state_spec((1, n_heads, d, d)),
            state_spec((1, n_heads, 1, d)),
            state_spec((1, n_heads, 1, LANES)),
        ],
        out_shape=[
            jax.ShapeDtypeStruct((n_tok, da), F32),
            jax.ShapeDtypeStruct((batch, n_heads, d, d), F32),
            jax.ShapeDtypeStruct((batch, n_heads, 1, d), F32),
            jax.ShapeDtypeStruct((batch, n_heads, 1, LANES), F32),
        ],
        scratch_shapes=[pltpu.VMEM((chunk + SUBLANES, da), F32), pltpu.VMEM((chunk + SUBLANES, da), F32)],
        compiler_params=pltpu.CompilerParams(
            dimension_semantics=("parallel", "arbitrary"),
            vmem_limit_bytes=32 * MIB),
        name="mlstm",
    )(proj, proj, proj, gates, b_gate_row, conv_w, conv_w, cbr, cbr, conv_state, conv_state, c0, n0r, m0r)
    return h, c_new, n_new.reshape(batch, n_heads, d), m_new[:, :, 0, 0]


def _sb_block(q_b, k_b, v_b, carry, acc, *, scale, diagonal):
    tq, tk = q_b.shape[0], k_b.shape[0]
    z = lax.dot_general(q_b, k_b, _NT, preferred_element_type=F32) * scale
    softplus = jnp.maximum(z, 0.0) + jnp.log(1.0 + jnp.exp(-jnp.abs(z)))
    log_beta = z - softplus
    drop = softplus
    if diagonal:
        mask = (lax.broadcasted_iota(jnp.int32, (tq, tk), 1) < lax.broadcasted_iota(jnp.int32, (tq, tk), 0))
        drop = jnp.where(mask, softplus, 0.0)
    later_mat = (lax.broadcasted_iota(jnp.int32, (tk, tk), 0)
                 > lax.broadcasted_iota(jnp.int32, (tk, tk), 1)).astype(BF16)
    later = _dot_pieces(_bf16_pieces(drop, 2), later_mat, pieces_first=True)
    a = jnp.exp(log_beta - (later + carry))
    if diagonal:
        a = jnp.where(mask, a, 0.0)
    acc = acc + jnp.dot(a.astype(BF16), v_b, preferred_element_type=F32)
    carry = carry + (later[:, 0:1] + drop[:, 0:1])
    return carry, acc


def _sb_sweep(qs, k_ats, v_ats, first_block, carries, accs, *, tk, scale):
    def carry_min(cs):
        m = cs[0]
        for c in cs[1:]:
            m = jnp.minimum(m, c)
        return jnp.min(m)

    def cond(state):
        j, cmin, _, _ = state
        return jnp.logical_and(j >= 0, cmin < -EXP_ZERO_BELOW)

    def body(state):
        j, _, cs, acs = state
        start = pl.multiple_of(j * tk, tk)
        new = [_sb_block(q, k_at[pl.ds(start, tk), :].astype(BF16), v_at[pl.ds(start, tk), :].astype(BF16),
                         c, a, scale=scale, diagonal=False)
               for q, k_at, v_at, c, a in zip(qs, k_ats, v_ats, cs, acs)]
        cs = tuple(n[0] for n in new)
        return j - 1, carry_min(cs), cs, tuple(n[1] for n in new)

    carries = tuple(carries)
    _, _, _, accs = lax.while_loop(cond, body, (first_block, carry_min(carries), carries, tuple(accs)))
    return accs


def _sb_prompt_kernel(q_ref, k_ref, v_ref, o_ref, *, tq, scale):
    i = pl.program_id(1)
    n_heads, _, d = k_ref.shape
    start = pl.multiple_of(i * tq, tq)
    qs, carries, accs = [], [], []
    for h in range(n_heads):
        q_b = q_ref[:, h * d:(h + 1) * d]
        carry, acc = _sb_block(q_b, k_ref[h, pl.ds(start, tq), :], v_ref[h, pl.ds(start, tq), :],
                               jnp.zeros((tq, 1), F32), jnp.zeros((tq, d), F32), scale=scale, diagonal=True)
        qs.append(q_b)
        carries.append(carry)
        accs.append(acc)
    accs = _sb_sweep(qs, [k_ref.at[h] for h in range(n_heads)], [v_ref.at[h] for h in range(n_heads)],
                     i - 1, carries, accs, tk=tq, scale=scale)
    for h in range(n_heads):
        o_ref[:, h * d:(h + 1) * d] = accs[h].astype(o_ref.dtype)


def _sb_prompt(q, k_hm, v_hm, *, seq, heads_per_step=2):
    n_heads, _, d = k_hm.shape
    hb = heads_per_step if n_heads % heads_per_step == 0 else 1
    tq = _tile(seq, 256)
    kv_spec = pl.BlockSpec((hb, seq, d), lambda g, i: (g, 0, 0))
    return pl.pallas_call(
        functools.partial(_sb_prompt_kernel, tq=tq, scale=float(d) ** -0.5),
        grid=(n_heads // hb, seq // tq),
        in_specs=[pl.BlockSpec((tq, hb * d), lambda g, i: (i, g)), kv_spec, kv_spec],
        out_specs=pl.BlockSpec((tq, hb * d), lambda g, i: (i, g)),
        out_shape=jax.ShapeDtypeStruct((seq, n_heads * d), BF16),
        compiler_params=pltpu.CompilerParams(
            dimension_semantics=("parallel", "arbitrary"),
            vmem_limit_bytes=48 * MIB),
        name="sb_prompt",
    )(q, k_hm, v_hm)


def _sb_decode_kernel(q_ref, kn_ref, vn_ref, kp_ref, vp_ref, o_ref, *, tk, scale):
    n_heads, tq, d = kn_ref.shape
    qs, carries, accs = [], [], []
    for h in range(n_heads):
        q_b = q_ref[:, h * d:(h + 1) * d]
        carry, acc = _sb_block(q_b, kn_ref[h], vn_ref[h], jnp.zeros((tq, 1), F32), jnp.zeros((tq, d), F32),
                               scale=scale, diagonal=True)
        qs.append(q_b)
        carries.append(carry)
        accs.append(acc)
    past_len = kp_ref.shape[2]
    accs = _sb_sweep(qs, [kp_ref.at[0, h] for h in range(n_heads)], [vp_ref.at[0, h] for h in range(n_heads)],
                     past_len // tk - 1, carries, accs, tk=tk, scale=scale)
    for h in range(n_heads):
        o_ref[:, h * d:(h + 1) * d] = accs[h].astype(o_ref.dtype)


def _sb_decode(q, k_hm, v_hm, past_k, past_v, *, batch, seq, heads_per_step=4):
    n_heads, _, d = k_hm.shape
    hb = heads_per_step if n_heads % heads_per_step == 0 else 1
    past_len = past_k.shape[2]
    tk = _tile(past_len, 256)
    new_spec = pl.BlockSpec((hb, seq, d), lambda b, g: (g, b, 0))
    past_spec = pl.BlockSpec((1, hb, past_len, d), lambda b, g: (b, g, 0, 0))
    return pl.pallas_call(
        functools.partial(_sb_decode_kernel, tk=tk, scale=float(d) ** -0.5),
        grid=(batch, n_heads // hb),
        in_specs=[pl.BlockSpec((seq, hb * d), lambda b, g: (b, g)), new_spec, new_spec, past_spec, past_spec],
        out_specs=pl.BlockSpec((seq, hb * d), lambda b, g: (b, g)),
        out_shape=jax.ShapeDtypeStruct((batch * seq, n_heads * d), BF16),
        compiler_params=pltpu.CompilerParams(dimension_semantics=("parallel", "parallel")),
        name="sb_decode",
    )(q, k_hm, v_hm, past_k, past_v)


ROUTE_E1, ROUTE_E2, ROUTE_W1, ROUTE_W2 = 0, 1, 2, 3


def _route(logits, *, n_experts, n_groups):
    per_group = n_experts // n_groups
    lane = lax.broadcasted_iota(jnp.int32, logits.shape, 1)
    big = jnp.int32(LANES)
    is_group = jnp.logical_and(lane >= n_experts, lane < n_experts + n_groups)
    lg = jnp.where(is_group, logits, -jnp.inf)
    lg_max = jnp.max(lg, axis=-1, keepdims=True)
    p_sel = 1.0 / jnp.sum(jnp.exp(lg - lg_max), axis=-1, keepdims=True)
    grp = jnp.min(jnp.where(lg == lg_max, lane - n_experts, big), axis=-1, keepdims=True)
    in_grp = jnp.logical_and(lane >= grp * per_group, lane < (grp + 1) * per_group)
    le = jnp.where(in_grp, logits, -jnp.inf)
    v1 = jnp.max(le, axis=-1, keepdims=True)
    i1 = jnp.min(jnp.where(le == v1, lane, big), axis=-1, keepdims=True)
    le2 = jnp.where(lane == i1, -jnp.inf, le)
    v2 = jnp.max(le2, axis=-1, keepdims=True)
    i2 = jnp.min(jnp.where(le2 == v2, lane, big), axis=-1, keepdims=True)
    e2 = jnp.exp(v2 - v1)
    w1 = p_sel / (1.0 + e2)
    w2 = p_sel * e2 / (1.0 + e2)
    rec = jnp.where(lane == ROUTE_E1, i1.astype(F32), 0.0)
    rec = jnp.where(lane == ROUTE_E2, i2.astype(F32), rec)
    rec = jnp.where(lane == ROUTE_W1, w1, rec)
    return jnp.where(lane == ROUTE_W2, w2, rec)


def _merge_kernel(x_ref, ha_ref, om_ref, hs_ref, gm_ref, gs_ref, wbm_ref, wbs_ref, wout_ref, g_ref, b_ref,
                  wrh_ref, wrl_ref, br_ref, *refs, alpha, n_experts, n_groups, own_blocks):
    o_ref, route_ref = refs[-2:]
    i = pl.program_id(0)

    @pl.when(i < own_blocks)
    def _():
        h_m = (jax.nn.sigmoid(om_ref[...]) * ha_ref[...]).astype(BF16)
        t_m = jnp.dot(h_m, wbm_ref[...], preferred_element_type=F32)
        t_s = jnp.dot(hs_ref[...], wbs_ref[...], preferred_element_type=F32)
        merged = jax.nn.sigmoid(gm_ref[...]) * t_m + jax.nn.sigmoid(gs_ref[...]) * t_s
        y = jnp.dot(merged.astype(BF16), wout_ref[...], preferred_element_type=F32)
        x1 = _layer_norm(alpha * x_ref[...] + y, g_ref[...], b_ref[...])
        o_ref[...] = x1
        x_hi, x_lo = _bf16_pieces(x1, 2)
        logits = (jnp.dot(x_hi, wrh_ref[...], preferred_element_type=F32)
                  + (jnp.dot(x_hi, wrl_ref[...], preferred_element_type=F32)
                     + jnp.dot(x_lo, wrh_ref[...], preferred_element_type=F32))) + br_ref[...]
        route_ref[...] = _route(logits, n_experts=n_experts, n_groups=n_groups)

    @pl.when(i >= own_blocks)
    def _():
        o_ref[...] = jnp.zeros_like(o_ref)
        route_ref[...] = jnp.zeros_like(route_ref)


def _merge(x, h_a, proj, h_s, w_bm, w_bs, w_out, ln_g, ln_b, w_router_hi, w_router_lo, b_router, *,
           o_col, gm_col, gs_col, alpha, n_experts, n_groups, total_rows, row0, prev):
    n, dm = x.shape
    da, ds = h_a.shape[1], h_s.shape[1]
    tm = _tile(math.gcd(n, total_rows - n) if total_rows > n else n, 256)
    assert row0 % tm == 0 and total_rows % tm == 0 and (prev is not None or row0 == 0)
    blk0 = row0 // tm
    own = n // tm
    steps = own if prev is not None else total_rows // tm
    const = lambda shape: pl.BlockSpec(shape, lambda i: (0, 0), pipeline_mode=pl.Buffered(1))
    rows = lambda i: jnp.minimum(i, own - 1)
    prev = () if prev is None else tuple(prev)
    n_in = 14
    return pl.pallas_call(
        functools.partial(_merge_kernel, alpha=alpha, n_experts=n_experts, n_groups=n_groups, own_blocks=own),
        grid=(steps,),
        in_specs=[
            pl.BlockSpec((tm, dm), lambda i: (rows(i), 0)),
            pl.BlockSpec((tm, da), lambda i: (rows(i), 0)),
            pl.BlockSpec((tm, da), lambda i: (rows(i), o_col // da)),
            pl.BlockSpec((tm, ds), lambda i: (rows(i), 0)),
            pl.BlockSpec((tm, dm), lambda i: (rows(i), gm_col // dm)),
            pl.BlockSpec((tm, dm), lambda i: (rows(i), gs_col // dm)),
            const((da, dm)), const((ds, dm)), const((dm, dm)), const((1, dm)), const((1, dm)),
            const((dm, LANES)), const((dm, LANES)), const((1, LANES)),
        ] + [pl.BlockSpec(memory_space=pl.ANY)] * len(prev),
        out_specs=[pl.BlockSpec((tm, dm), lambda i: (blk0 + i, 0)),
                   pl.BlockSpec((tm, LANES), lambda i: (blk0 + i, 0))],
        out_shape=[jax.ShapeDtypeStruct((total_rows, dm), F32), jax.ShapeDtypeStruct((total_rows, LANES), F32)],
        input_output_aliases={n_in + k: k for k in range(len(prev))},
        compiler_params=pltpu.CompilerParams(dimension_semantics=("parallel",), vmem_limit_bytes=48 * MIB),
        name="merge_out_ln1",
    )(x, h_a, proj, h_s, proj, proj, w_bm, w_bs, w_out, ln_g, ln_b, w_router_hi, w_router_lo, b_router, *prev)


def _expert_kernel(te_ref, tb_ref, tv_ref, na_ref, perm_ref,
                   x_hbm, wg_ref, wu_ref, wd_ref, y_hbm,
                   xbuf, obuf, wgb, wub, wdb, gsem, ssem, *, n_tok):
    j = pl.program_id(0)
    n_act = na_ref[0]
    slot = lax.rem(j, 2)

    def gather_copy(s, i, token):
        return pltpu.make_async_copy(x_hbm.at[pl.ds(token, 1), :], xbuf.at[s, pl.ds(i, 1), :], gsem.at[s])

    def scatter_copy(s, i, row):
        return pltpu.make_async_copy(obuf.at[s, pl.ds(i, 1), :], y_hbm.at[pl.ds(row, 1), :], ssem.at[s])

    def for_rows(count, fn):
        groups = lax.shift_right_logical(count, 3)

        def group_body(g, c):
            for u in range(SUBLANES):
                fn(g * SUBLANES + u)
            return c

        def row_body(i, c):
            fn(i)
            return c

        lax.fori_loop(0, groups, group_body, 0)
        lax.fori_loop(groups * SUBLANES, count, row_body, 0)

    def start_gather(t, s):
        base = tb_ref[t]

        def start(i):
            pair = perm_ref[base + i]
            gather_copy(s, i, jnp.where(pair >= n_tok, pair - n_tok, pair)).start()

        for_rows(tv_ref[t], start)

    def start_scatter(t, s):
        base = tb_ref[t]
        for_rows(tv_ref[t], lambda i: scatter_copy(s, i, perm_ref[base + i]).start())

    def wait_rows(count, block_copy, row_copy):
        bulk = pl.multiple_of(lax.shift_right_logical(count, 3) * SUBLANES, SUBLANES)

        @pl.when(bulk > 0)
        def _():
            block_copy(pl.ds(0, bulk)).wait()

        def row_body(i, c):
            row_copy(i).wait()
            return c

        lax.fori_loop(bulk, count, row_body, 0)

    def wait_gather(t, s):
        wait_rows(tv_ref[t],
                  lambda rows: pltpu.make_async_copy(x_hbm.at[rows, :], xbuf.at[s, rows, :], gsem.at[s]),
                  lambda i: gather_copy(s, i, 0))

    def wait_scatter(t, s):
        wait_rows(tv_ref[t],
                  lambda rows: pltpu.make_async_copy(obuf.at[s, rows, :], y_hbm.at[rows, :], ssem.at[s]),
                  lambda i: scatter_copy(s, i, 0))

    @pl.when(j == 0)
    def _():
        xbuf[...] = jnp.zeros_like(xbuf)
        start_gather(0, 0)

    @pl.when(j + 1 < n_act)
    def _():
        start_gather(j + 1, 1 - slot)

    @pl.when(j < n_act)
    def _():
        wait_gather(j, slot)

        @pl.when(j >= 2)
        def _():
            wait_scatter(j - 2, slot)

        @pl.when(jnp.logical_or(j == 0, te_ref[j] != te_ref[jnp.maximum(j - 1, 0)]))
        def _():
            wgb[...] = wg_ref[0].astype(BF16)
            wub[...] = wu_ref[0].astype(BF16)
            wdb[...] = wd_ref[0].astype(BF16)

        x = xbuf[slot].astype(BF16)
        gate = jnp.dot(x, wgb[...], preferred_element_type=F32)
        up = jnp.dot(x, wub[...], preferred_element_type=F32)
        hid = (gate * jax.nn.sigmoid(gate) * up).astype(BF16)
        obuf[slot] = jnp.dot(hid, wdb[...], preferred_element_type=F32)
        start_scatter(j, slot)

        @pl.when(j == n_act - 1)
        def _():
            @pl.when(j >= 1)
            def _():
                wait_scatter(j - 1, 1 - slot)

            wait_scatter(j, slot)


def _expert_tables(route, n_experts, tile):
    n_tok = route.shape[0]
    i32 = jnp.int32
    e_flat = jnp.concatenate([route[:, ROUTE_E1], route[:, ROUTE_E2]]).astype(i32)
    n_pad = (1 << (2 * n_tok - 1).bit_length()) - 2 * n_tok
    keys = jnp.concatenate([e_flat, jnp.full((n_pad,), n_experts, i32)])
    perm = jnp.argsort(keys, stable=True)[:2 * n_tok].astype(i32)
    eids = jnp.arange(n_experts, dtype=i32)
    counts = jnp.sum((e_flat[:, None] == eids[None, :]).astype(i32), axis=0)
    tiles_e = (counts + (tile - 1)) // tile
    before = eids[None, :] < eids[:, None]
    pair_start = jnp.sum(jnp.where(before, counts[None, :], 0), axis=1)
    tile_start = jnp.sum(jnp.where(before, tiles_e[None, :], 0), axis=1)
    n_act = jnp.sum(tiles_e)
    t_max = (2 * n_tok) // tile + n_experts
    j = jnp.arange(t_max, dtype=i32)
    owner = jnp.logical_and(j[:, None] >= tile_start[None, :], j[:, None] < (tile_start + tiles_e)[None, :])
    pick = lambda table: jnp.sum(jnp.where(owner, table[None, :], 0), axis=1)
    active = j < n_act
    k = j - pick(tile_start)
    last_e = jnp.max(jnp.where(tiles_e > 0, eids, 0))
    te = jnp.where(active, pick(eids), last_e).astype(i32)
    tb = jnp.where(active, pick(pair_start) + k * tile, 0).astype(i32)
    tv = jnp.where(active, jnp.clip(pick(counts) - k * tile, 0, tile), 0).astype(i32)
    return te, tb, tv, n_act.reshape(1).astype(i32), perm, t_max


def _experts(x1, route, w_gate, w_up, w_down):
    n_tok, dm = x1.shape
    n_experts, _, de = w_gate.shape
    tile = 256
    te, tb, tv, n_act, perm, t_max = _expert_tables(route, n_experts, tile)
    w_in_spec = pl.BlockSpec((1, dm, de), lambda j, te, tb, tv, na, perm: (te[j], 0, 0))
    w_out_spec = pl.BlockSpec((1, de, dm), lambda j, te, tb, tv, na, perm: (te[j], 0, 0))
    grid_spec = pltpu.PrefetchScalarGridSpec(
        num_scalar_prefetch=5,
        grid=(t_max,),
        in_specs=[pl.BlockSpec(memory_space=pl.ANY), w_in_spec, w_in_spec, w_out_spec],
        out_specs=pl.BlockSpec(memory_space=pl.ANY),
        scratch_shapes=[
            pltpu.VMEM((2, tile, dm), F32), pltpu.VMEM((2, tile, dm), F32),
            pltpu.VMEM((dm, de), BF16), pltpu.VMEM((dm, de), BF16), pltpu.VMEM((de, dm), BF16),
            pltpu.SemaphoreType.DMA((2,)), pltpu.SemaphoreType.DMA((2,)),
        ],
    )
    return pl.pallas_call(
        functools.partial(_expert_kernel, n_tok=n_tok),
        grid_spec=grid_spec,
        out_shape=jax.ShapeDtypeStruct((2 * n_tok, dm), F32),
        compiler_params=pltpu.CompilerParams(dimension_semantics=("arbitrary",), vmem_limit_bytes=48 * MIB),
        name="moe_experts",
    )(te, tb, tv, n_act, perm, x1, w_gate, w_up, w_down)


def _combine_kernel(x_ref, ya_ref, yb_ref, r_ref, g_ref, b_ref, first_ref, second_ref, *, alpha, first_blocks):
    r = r_ref[...]
    moe = r[:, ROUTE_W1:ROUTE_W1 + 1] * ya_ref[...] + r[:, ROUTE_W2:ROUTE_W2 + 1] * yb_ref[...]
    out = _layer_norm(alpha * x_ref[...] + moe, g_ref[...], b_ref[...])
    i = pl.program_id(0)

    @pl.when(i < first_blocks)
    def _():
        first_ref[...] = out

    @pl.when(i >= first_blocks)
    def _():
        second_ref[...] = out


def _combine(x1, y_pairs, route, ln_g, ln_b, *, alpha, n_first):
    n, dm = x1.shape
    tm = _tile(math.gcd(n_first, n - n_first), 256)
    nb, nb_first = n // tm, n_first // tm
    const = lambda shape: pl.BlockSpec(shape, lambda i: (0, 0))
    return pl.pallas_call(
        functools.partial(_combine_kernel, alpha=alpha, first_blocks=nb_first),
        grid=(nb,),
        in_specs=[
            pl.BlockSpec((tm, dm), lambda i: (i, 0)),
            pl.BlockSpec((tm, dm), lambda i: (i, 0)),
            pl.BlockSpec((tm, dm), lambda i: (nb + i, 0)),
            pl.BlockSpec((tm, LANES), lambda i: (i, 0)),
            const((1, dm)), const((1, dm)),
        ],
        out_specs=[pl.BlockSpec((tm, dm), lambda i: (jnp.minimum(i, nb_first - 1), 0)),
                   pl.BlockSpec((tm, dm), lambda i: (jnp.maximum(i - nb_first, 0), 0))],
        out_shape=[jax.ShapeDtypeStruct((n_first, dm), F32), jax.ShapeDtypeStruct((n - n_first, dm), F32)],
        compiler_params=pltpu.CompilerParams(dimension_semantics=("arbitrary",), vmem_limit_bytes=48 * MIB),
        name="moe_combine_ln2",
    )(x1, y_pairs, y_pairs, route, ln_g, ln_b)


def _mixer(x, conv_state, c0, n0, m0, past_k, past_v, p, *, chunk_pref, total_rows, row0, prev):
    batch, seq, dm = x.shape
    n_tok = batch * seq
    xf = x.reshape(n_tok, dm)
    n_heads, d = p["n_heads"], p["head_dim"]
    da = n_heads * d
    cols = p["cols"]

    proj, q_s, k_hm, k_hm16, v_hm, v_hm16, gates = _project(
        xf, p["w_all"], main_cols=p["main_cols"], sb_cols=p["sb_cols"], tn=p["tn"])

    chunk = _tile(seq, chunk_pref)
    h_a, c_new, n_new, m_new = _mlstm(
        proj, gates, p["b_gate_row"], p["conv_w"], p["conv_b"], conv_state, c0, n0, m0,
        batch=batch, seq=seq, chunk=chunk, n_heads=n_heads, head_dim=d,
        q_col=cols["q"], k_col=cols["k"], v_col=cols["v"])

    if past_k is None:
        assert batch == 1
        h_s = _sb_prompt(q_s, k_hm16, v_hm16, seq=seq)
    else:
        h_s = _sb_decode(q_s, k_hm16, v_hm16, past_k, past_v, batch=batch, seq=seq)

    x1, route = _merge(xf, h_a, proj, h_s, p["w_bm"], p["w_bs"], p["w_out"], p["ln1_g"], p["ln1_b"],
                       p["w_router_hi"], p["w_router_lo"], p["b_router"],
                       o_col=cols["o"], gm_col=cols["gm"], gs_col=cols["gs"],
                       alpha=p["alpha"], n_experts=p["n_experts"], n_groups=p["n_groups"],
                       total_rows=total_rows, row0=row0, prev=prev)

    n_sb = k_hm.shape[0]
    sb_k = k_hm.reshape(n_sb, batch, seq, -1).transpose(1, 0, 2, 3)
    sb_v = v_hm.reshape(n_sb, batch, seq, -1).transpose(1, 0, 2, 3)
    new_conv = proj.reshape(batch, seq, -1)[:, seq - (CONV_W - 1):, cols["q"]:cols["q"] + 2 * da]
    return x1, route, (sb_k, sb_v, c_new, n_new, m_new, new_conv)


def _prepare_layer(l, depth, w_in, b_gate, conv_w, conv_b, w_branch_m, w_branch_s, w_out, ln1_g, ln1_b,
                   w_router_group, b_router_group, w_router_expert, b_router_expert,
                   w_exp_gate, w_exp_up, w_exp_down, ln2_g, ln2_b, head_dim):
    dm = w_in.shape[1]
    da2 = conv_w.shape[-1]
    da = da2 // 2
    n_heads = b_gate.shape[-1] // 2
    ds = w_branch_s.shape[1]
    sizes = (da2, da, da, 2 * n_heads, ds, ds, ds, dm, dm)
    offs = [0]
    for s in sizes:
        offs.append(offs[-1] + s)
    w = w_in[l]
    part = lambda i: w[:, offs[i]:offs[i + 1]]
    main_cols = da2 + 2 * da + 2 * dm
    tn = math.gcd(math.gcd(512, main_cols), ds)
    w_all = jnp.concatenate(
        [part(i).astype(BF16) for i in (0, 1, 2, 7, 8, 4, 5, 6)]
        + [jnp.pad(part(3).astype(BF16), ((0, 0), (0, tn - 2 * n_heads)))], axis=1)
    cols = {"q": 0, "k": da, "v": da2, "o": da2 + da, "gm": da2 + 2 * da, "gs": da2 + 2 * da + dm}
    assert cols["k"] % da == 0 and cols["o"] % da == 0 and cols["gm"] % dm == 0 and cols["gs"] % dm == 0
    b_gate_row = jnp.pad(b_gate[l].astype(F32), (0, LANES - 2 * n_heads)).reshape(1, LANES)
    n_groups = w_router_group.shape[-1]
    n_experts = w_router_expert.shape[-1]
    pad_r = LANES - n_experts - n_groups
    w_router = jnp.pad(jnp.concatenate([w_router_expert[l], w_router_group[l]], axis=1),
                       ((0, 0), (0, pad_r))).astype(F32)
    w_router_hi = w_router.astype(BF16)
    w_router_lo = (w_router - w_router_hi.astype(F32)).astype(BF16)
    b_router = jnp.pad(jnp.concatenate([b_router_expert[l], b_router_group[l]]), (0, pad_r)).reshape(1, LANES)
    return {
        "n_heads": n_heads, "head_dim": head_dim, "cols": cols, "n_groups": n_groups, "n_experts": n_experts,
        "alpha": (2.0 * depth) ** 0.25, "main_cols": main_cols, "sb_cols": ds, "tn": tn,
        "w_all": w_all, "b_gate_row": b_gate_row,
        "conv_w": conv_w[l], "conv_b": conv_b[l],
        "w_bm": w_branch_m[l].astype(BF16), "w_bs": w_branch_s[l].astype(BF16), "w_out": w_out[l].astype(BF16),
        "ln1_g": ln1_g[l].reshape(1, dm), "ln1_b": ln1_b[l].reshape(1, dm),
        "w_router_hi": w_router_hi, "w_router_lo": w_router_lo, "b_router": b_router.astype(F32),
        "w_eg": w_exp_gate[l], "w_eu": w_exp_up[l], "w_ed": w_exp_down[l],
        "ln2_g": ln2_g[l].reshape(1, dm), "ln2_b": ln2_b[l].reshape(1, dm),
    }


def kernel(x_prompt, x_sample, cache_sb_k, cache_sb_v, state_mlstm_C, state_mlstm_n, state_mlstm_m, state_conv,
           w_in, b_gate, conv_w, conv_b, w_branch_m, w_branch_s, w_out, ln1_g, ln1_b,
           w_router_group, b_router_group, w_router_expert, b_router_expert,
           w_exp_gate, w_exp_up, w_exp_down, ln2_g, ln2_b):
    depth = w_in.shape[0]
    bp = x_prompt.shape[0]
    n_heads, head_dim = state_mlstm_C.shape[2], state_mlstm_C.shape[3]
    da2 = conv_w.shape[-1]
    xp, xs = x_prompt, x_sample
    st_p_all, st_s_all = [], []
    for l in range(depth):
        p = _prepare_layer(l, depth, w_in, b_gate, conv_w, conv_b, w_branch_m, w_branch_s, w_out, ln1_g, ln1_b,
                           w_router_group, b_router_group, w_router_expert, b_router_expert,
                           w_exp_gate, w_exp_up, w_exp_down, ln2_g, ln2_b, head_dim)
        n_p, n_s = xp.shape[0] * xp.shape[1], xs.shape[0] * xs.shape[1]
        x1, route, st_p = _mixer(
            xp, jnp.zeros((bp, CONV_W - 1, da2), F32), jnp.zeros((bp, n_heads, head_dim, head_dim), F32),
            jnp.zeros((bp, n_heads, head_dim), F32), jnp.full((bp, n_heads), NEG, F32), None, None, p,
            chunk_pref=256, total_rows=n_p + n_s, row0=0, prev=None)
        x1, route, st_s = _mixer(xs, state_conv[l], state_mlstm_C[l], state_mlstm_n[l], state_mlstm_m[l],
                                 cache_sb_k[l], cache_sb_v[l], p, chunk_pref=256,
                                 total_rows=n_p + n_s, row0=n_p, prev=(x1, route))
        y_pairs = _experts(x1, route, p["w_eg"], p["w_eu"], p["w_ed"])
        xp2, xs2 = _combine(x1, y_pairs, route, p["ln2_g"], p["ln2_b"], alpha=p["alpha"], n_first=n_p)
        xp, xs = xp2.reshape(xp.shape), xs2.reshape(xs.shape)
        st_p_all.append(st_p)
        st_s_all.append(st_s)
    stack = lambda states, i: jnp.stack([s[i] for s in states], axis=0)
    return (xp, xs,
            *(stack(st_p_all, i) for i in range(6)),
            *(stack(st_s_all, i) for i in range(6)))
```

```python
import functools
import math

import jax
import jax.numpy as jnp
from jax import lax
from jax.experimental import pallas as pl
from jax.experimental.pallas import tpu as pltpu

F32 = jnp.float32
BF16 = jnp.bfloat16

LANES = 128
SUBLANES = 8
LN_EPS = 1e-5
NEG = -1e30
CONV_W = 4
EXP_ZERO_BELOW = -104.0
MIB = 1024 * 1024

_NT = (((1,), (1,)), ((), ()))
_TN = (((0,), (0,)), ((), ()))


def _tile(n, pref):
    if n <= pref:
        return n
    t = pref
    while t >= SUBLANES:
        if n % t == 0 and t % SUBLANES == 0:
            return t
        t -= SUBLANES
    return n


def _log_sigmoid(x):
    return jnp.minimum(x, 0.0) - jnp.log1p(jnp.exp(-jnp.abs(x)))


def _layer_norm(x, g, b):
    mu = jnp.mean(x, axis=-1, keepdims=True)
    d = x - mu
    var = jnp.mean(d * d, axis=-1, keepdims=True)
    return d * lax.rsqrt(var + LN_EPS) * g + b


def _bf16_pieces(x, n):
    pieces = []
    rem = x
    for _ in range(n - 1):
        p = rem.astype(BF16)
        pieces.append(p)
        rem = rem - p.astype(F32)
    pieces.append(rem.astype(BF16))
    return pieces


def _dot_pieces(pieces, other, *, pieces_first, dims=None):
    total = None
    for p in pieces:
        a, b = (p, other) if pieces_first else (other, p)
        if dims is None:
            t = jnp.dot(a, b, preferred_element_type=F32)
        else:
            t = lax.dot_general(a, b, dims, preferred_element_type=F32)
        total = t if total is None else total + t
    return total


def _proj_kernel(x_ref, w_ref, main_ref, qs_ref, k32_ref, k16_ref, v32_ref, v16_ref, g_ref, xb_ref, *, ends):
    j = pl.program_id(1)
    end_main, end_qs, end_k, end_v = ends

    @pl.when(j == 0)
    def _():
        xb_ref[...] = x_ref[...].astype(BF16)

    acc = jnp.dot(xb_ref[...], w_ref[...], preferred_element_type=F32)

    def store_heads(f32_ref, bf16_ref):
        for h in range(f32_ref.shape[0]):
            piece = acc[:, h * LANES:(h + 1) * LANES]
            f32_ref[h] = piece
            bf16_ref[h] = piece.astype(BF16)

    @pl.when(j < end_main)
    def _():
        main_ref[...] = acc

    @pl.when(jnp.logical_and(j >= end_main, j < end_qs))
    def _():
        qs_ref[...] = acc.astype(BF16)

    @pl.when(jnp.logical_and(j >= end_qs, j < end_k))
    def _():
        store_heads(k32_ref, k16_ref)

    @pl.when(jnp.logical_and(j >= end_k, j < end_v))
    def _():
        store_heads(v32_ref, v16_ref)

    @pl.when(j >= end_v)
    def _():
        g_ref[...] = acc[:, :LANES]


def _project(x, w_all, *, main_cols, sb_cols, tn):
    n, k = x.shape
    tm = _tile(n, 1024)
    t_main, t_sb = main_cols // tn, sb_cols // tn
    ends = (t_main, t_main + t_sb, t_main + 2 * t_sb, t_main + 3 * t_sb)
    n_tiles = ends[-1] + 1
    assert w_all.shape[1] == n_tiles * tn
    local = lambda j, start, count: jnp.clip(j - start, 0, count - 1)
    heads = tn // LANES
    n_sb = sb_cols // LANES
    hm_spec = lambda start: pl.BlockSpec((heads, tm, LANES), lambda i, j: (local(j, start, t_sb), i, 0))
    return pl.pallas_call(
        functools.partial(_proj_kernel, ends=ends),
        grid=(n // tm, n_tiles),
        in_specs=[pl.BlockSpec((tm, k), lambda i, j: (i, 0), pipeline_mode=pl.Buffered(1)),
                  pl.BlockSpec((k, tn), lambda i, j: (0, j))],
        out_specs=[
            pl.BlockSpec((tm, tn), lambda i, j: (i, local(j, 0, t_main))),
            pl.BlockSpec((tm, tn), lambda i, j: (i, local(j, ends[0], t_sb))),
            hm_spec(ends[1]), hm_spec(ends[1]), hm_spec(ends[2]), hm_spec(ends[2]),
            pl.BlockSpec((tm, LANES), lambda i, j: (i, 0)),
        ],
        out_shape=[
            jax.ShapeDtypeStruct((n, main_cols), F32),
            jax.ShapeDtypeStruct((n, sb_cols), BF16),
            jax.ShapeDtypeStruct((n_sb, n, LANES), F32), jax.ShapeDtypeStruct((n_sb, n, LANES), BF16),
            jax.ShapeDtypeStruct((n_sb, n, LANES), F32), jax.ShapeDtypeStruct((n_sb, n, LANES), BF16),
            jax.ShapeDtypeStruct((n, LANES), F32),
        ],
        scratch_shapes=[pltpu.VMEM((tm, k), BF16)],
        compiler_params=pltpu.CompilerParams(
            dimension_semantics=("parallel", "arbitrary"),
            vmem_limit_bytes=48 * MIB),
        name="in_proj",
    )(x, w_all)


def _mlstm_kernel(q_ref, k_ref, v_ref, g_ref, bg_ref, cwq_ref, cwk_ref, cbq_ref, cbk_ref,
                  csq_ref, csk_ref, c0_ref, n0_ref, m0_ref,
                  h_ref, c_ref, n_ref, m_ref, qext_ref, kext_ref, *, chunk, n_heads, head_dim):
    step = pl.program_id(1)
    L, d = chunk, head_dim
    halo = CONV_W - 1
    base = SUBLANES - halo

    @pl.when(step == 0)
    def _():
        qext_ref[base:SUBLANES, :] = csq_ref[0]
        kext_ref[base:SUBLANES, :] = csk_ref[0]
        c_ref[...] = c0_ref[...]
        n_ref[...] = n0_ref[...]
        m_ref[...] = m0_ref[...]

    def conv_silu(raw_ref, ext_ref, cw_ref, cb_ref):
        ext_ref[SUBLANES:SUBLANES + L, :] = raw_ref[...]
        acc = cb_ref[...] + ext_ref[base:base + L, :] * cw_ref[0:1, :]
        for j in range(1, CONV_W):
            acc = acc + ext_ref[base + j:base + j + L, :] * cw_ref[j:j + 1, :]
        ext_ref[base:SUBLANES, :] = ext_ref[base + L:SUBLANES + L, :]
        return acc * jax.nn.sigmoid(acc)

    q_all = conv_silu(q_ref, qext_ref, cwq_ref, cbq_ref)
    k_all = conv_silu(k_ref, kext_ref, cwk_ref, cbk_ref) * (float(d) ** -0.5)

    lane = lax.broadcasted_iota(jnp.int32, (L, LANES), 1)
    pre = g_ref[...] + bg_ref[...]
    gates = jnp.where(lane < n_heads, pre, _log_sigmoid(pre))
    row_i = lax.broadcasted_iota(jnp.int32, (L, L), 0)
    col_i = lax.broadcasted_iota(jnp.int32, (L, L), 1)
    causal = col_i <= row_i
    tril = causal.astype(BF16)
    triu = (row_i <= col_i).astype(BF16)
    sel = (lax.broadcasted_iota(jnp.int32, (SUBLANES, LANES), 0)
           == lax.broadcasted_iota(jnp.int32, (SUBLANES, LANES), 1)).astype(BF16)
    gate_pieces = _bf16_pieces(gates, 3)
    cum_cols = _dot_pieces(gate_pieces, tril, pieces_first=False)
    gates_t = _dot_pieces(gate_pieces, sel, pieces_first=False, dims=_NT)
    cum_rows = _dot_pieces(_bf16_pieces(gates_t, 3), triu, pieces_first=True)

    for h in range(n_heads):
        cols = slice(h * d, (h + 1) * d)
        q = q_all[:, cols]
        k = k_all[:, cols]
        q_b = q.astype(BF16)
        k_b = k.astype(BF16)
        v_b = v_ref[:, cols].astype(BF16)
        i_col = gates[:, h:h + 1]
        b_col = cum_cols[:, n_heads + h:n_heads + h + 1]
        i_row = gates_t[h:h + 1, :]
        b_row = cum_rows[n_heads + h:n_heads + h + 1, :]
        c_prev = c_ref[0, h]
        n_prev = n_ref[0, h]
        m_prev = m_ref[0, h][:, 0:1]

        dmat = jnp.where(causal, b_col - b_row + i_row, NEG)
        inter = b_col + m_prev
        m_t = jnp.maximum(jnp.max(dmat, axis=-1, keepdims=True), inter)
        w = jnp.exp(dmat - m_t)
        s = lax.dot_general(q_b, k_b, _NT, preferred_element_type=F32) * w
        e_inter = jnp.exp(inter - m_t)
        num = (jnp.dot(s.astype(BF16), v_b, preferred_element_type=F32)
               + e_inter * jnp.dot(q_b, c_prev.astype(BF16), preferred_element_type=F32))
        den = jnp.sum(s, axis=-1, keepdims=True) + e_inter * jnp.sum(q * n_prev, axis=-1, keepdims=True)
        h_ref[:, cols] = num / jnp.maximum(jnp.abs(den), jnp.exp(-m_t))

        g_last = b_col[L - 1:L, :]
        w_last = g_last - b_col + i_col
        m_new = jnp.maximum(g_last + m_prev, jnp.max(w_last, axis=0, keepdims=True))
        decay = jnp.exp(g_last + m_prev - m_new)
        kw = jnp.exp(w_last - m_new) * k
        c_ref[0, h] = decay * c_prev + lax.dot_general(kw.astype(BF16), v_b, _TN, preferred_element_type=F32)
        n_ref[0, h] = decay * n_prev + jnp.sum(kw, axis=0, keepdims=True)
        m_ref[0, h] = jnp.broadcast_to(m_new, (1, LANES))


def _mlstm(proj, gates, b_gate_row, conv_w, conv_b, conv_state, c0, n0, m0, *, batch, seq, chunk,
           n_heads, head_dim, q_col, k_col, v_col):
    n_tok = batch * seq
    nc = seq // chunk
    d = head_dim
    da = n_heads * d
    qb, kb, vb = q_col // da, k_col // da, v_col // da
    kcw = (k_col - q_col) // da
    row = lambda b, c: b * nc + c
    n0r = n0.reshape(batch, n_heads, 1, d)
    m0r = jnp.broadcast_to(m0.reshape(batch, n_heads, 1, 1), (batch, n_heads, 1, LANES))
    cbr = conv_b.reshape(1, -1)
    state_spec = lambda shape: pl.BlockSpec(shape, lambda b, c: (b, 0, 0, 0))
    h, c_new, n_new, m_new = pl.pallas_call(
        functools.partial(_mlstm_kernel, chunk=chunk, n_heads=n_heads, head_dim=d),
        grid=(batch, nc),
        in_specs=[
            pl.BlockSpec((chunk, da), lambda b, c: (row(b, c), qb)),
            pl.BlockSpec((chunk, da), lambda b, c: (row(b, c), kb)),
            pl.BlockSpec((chunk, da), lambda b, c: (row(b, c), vb)),
            pl.BlockSpec((chunk, LANES), lambda b, c: (row(b, c), 0)),
            pl.BlockSpec((1, LANES), lambda b, c: (0, 0)),
            pl.BlockSpec((CONV_W, da), lambda b, c: (0, 0)),
            pl.BlockSpec((CONV_W, da), lambda b, c: (0, kcw)),
            pl.BlockSpec((1, da), lambda b, c: (0, 0)),
            pl.BlockSpec((1, da), lambda b, c: (0, kcw)),
            pl.BlockSpec((1, CONV_W - 1, da), lambda b, c: (b, 0, 0)),
            pl.BlockSpec((1, CONV_W - 1, da), lambda b, c: (b, 0, kcw)),
            state_spec((1, n_heads, d, d)),
            state_spec((1, n_heads, 1, d)),
            state_spec((1, n_heads, 1, LANES)),
        ],
        out_specs=[
            pl.BlockSpec((chunk, da), lambda b, c: (row(b, c), 0)),
            state_spec((1, n_heads, d, d)),
            state_spec((1, n_heads, 1, d)),
            state_spec((1, n_heads, 1, LANES)),
        ],
        out_shape=[
            jax.ShapeDtypeStruct((n_tok, da), F32),
            jax.ShapeDtypeStruct((batch, n_heads, d, d), F32),
            jax.ShapeDtypeStruct((batch, n_heads, 1, d), F32),
            jax.ShapeDtypeStruct((batch, n_heads, 1, LANES), F32),
        ],
        scratch_shapes=[pltpu.VMEM((chunk + SUBLANES, da), F32), pltpu.VMEM((chunk + SUBLANES, da), F32)],
        compiler_params=pltpu.CompilerParams(
            dimension_semantics=("parallel", "arbitrary"),
            vmem_limit_bytes=32 * MIB),
        name="mlstm",
    )(proj, proj, proj, gates, b_gate_row, conv_w, conv_w, cbr, cbr, conv_state, conv_state, c0, n0r, m0r)
    return h, c_new, n_new.reshape(batch, n_heads, d), m_new[:, :, 0, 0]


def _sb_scores(q_b, k_b, *, scale, diagonal):
    tq, tk = q_b.shape[0], k_b.shape[0]
    z = lax.dot_general(q_b, k_b, _NT, preferred_element_type=F32) * scale
    softplus = jnp.maximum(z, 0.0) + jnp.log(1.0 + jnp.exp(-jnp.abs(z)))
    log_beta = z - softplus
    drop, mask = softplus, None
    if diagonal:
        mask = (lax.broadcasted_iota(jnp.int32, (tq, tk), 1) < lax.broadcasted_iota(jnp.int32, (tq, tk), 0))
        drop = jnp.where(mask, softplus, 0.0)
    later_mat = (lax.broadcasted_iota(jnp.int32, (tk, tk), 0)
                 > lax.broadcasted_iota(jnp.int32, (tk, tk), 1)).astype(BF16)
    later = _dot_pieces(_bf16_pieces(drop, 2), later_mat, pieces_first=True)
    total = later[:, 0:1] + drop[:, 0:1]
    return log_beta, later, total, mask


def _sb_weights(scores, v_b, carry, acc):
    log_beta, later, total, mask = scores
    a = jnp.exp(log_beta - (later + carry))
    if mask is not None:
        a = jnp.where(mask, a, 0.0)
    return carry + total, acc + jnp.dot(a.astype(BF16), v_b, preferred_element_type=F32)


def _sb_block(q_b, k_b, v_b, carry, acc, *, scale, diagonal):
    return _sb_weights(_sb_scores(q_b, k_b, scale=scale, diagonal=diagonal), v_b, carry, acc)


def _sb_sweep(qs, k_ats, v_ats, first_block, carries, accs, *, tk, scale):
    def carry_min(cs):
        m = cs[0]
        for c in cs[1:]:
            m = jnp.minimum(m, c)
        return jnp.min(m)

    def cond(state):
        j, cmin, _, _ = state
        return jnp.logical_and(j >= 0, cmin < -EXP_ZERO_BELOW)

    def body(state):
        j, _, cs, acs = state
        start = pl.multiple_of(j * tk, tk)
        new = [_sb_block(q, k_at[pl.ds(start, tk), :].astype(BF16), v_at[pl.ds(start, tk), :].astype(BF16),
                         c, a, scale=scale, diagonal=False)
               for q, k_at, v_at, c, a in zip(qs, k_ats, v_ats, cs, acs)]
        cs = tuple(n[0] for n in new)
        return j - 1, carry_min(cs), cs, tuple(n[1] for n in new)

    carries = tuple(carries)
    _, _, _, accs = lax.while_loop(cond, body, (first_block, carry_min(carries), carries, tuple(accs)))
    return accs


def _sb_prompt_kernel(q_ref, k_ref, v_ref, o_ref, *, tq, scale):
    i = pl.program_id(1)
    n_heads, _, d = k_ref.shape
    start = pl.multiple_of(i * tq, tq)
    qs = [q_ref[:, h * d:(h + 1) * d] for h in range(n_heads)]
    zero_carry, zero_acc = jnp.zeros((tq, 1), F32), jnp.zeros((tq, d), F32)

    def finish(states, first_block):
        accs = _sb_sweep(qs, [k_ref.at[h] for h in range(n_heads)], [v_ref.at[h] for h in range(n_heads)],
                         first_block, [s[0] for s in states], [s[1] for s in states], tk=tq, scale=scale)
        for h in range(n_heads):
            o_ref[:, h * d:(h + 1) * d] = accs[h].astype(o_ref.dtype)

    @pl.when(i == 0)
    def _():
        finish([_sb_block(qs[h], k_ref[h, pl.ds(start, tq), :], v_ref[h, pl.ds(start, tq), :],
                          zero_carry, zero_acc, scale=scale, diagonal=True) for h in range(n_heads)], i - 1)

    @pl.when(i > 0)
    def _():
        prev = pl.multiple_of(start - tq, tq)
        states = []
        for h in range(n_heads):
            diag = _sb_scores(qs[h], k_ref[h, pl.ds(start, tq), :], scale=scale, diagonal=True)
            full = _sb_scores(qs[h], k_ref[h, pl.ds(prev, tq), :], scale=scale, diagonal=False)
            carry, acc = _sb_weights(diag, v_ref[h, pl.ds(start, tq), :], zero_carry, zero_acc)
            states.append(_sb_weights(full, v_ref[h, pl.ds(prev, tq), :], carry, acc))
        finish(states, i - 2)


def _sb_prompt(q, k_hm, v_hm, *, seq, heads_per_step=2):
    n_heads, _, d = k_hm.shape
    hb = heads_per_step if n_heads % heads_per_step == 0 else 1
    tq = _tile(seq, 256)
    kv_spec = pl.BlockSpec((hb, seq, d), lambda g, i: (g, 0, 0))
    return pl.pallas_call(
        functools.partial(_sb_prompt_kernel, tq=tq, scale=float(d) ** -0.5),
        grid=(n_heads // hb, seq // tq),
        in_specs=[pl.BlockSpec((tq, hb * d), lambda g, i: (i, g)), kv_spec, kv_spec],
        out_specs=pl.BlockSpec((tq, hb * d), lambda g, i: (i, g)),
        out_shape=jax.ShapeDtypeStruct((seq, n_heads * d), BF16),
        compiler_params=pltpu.CompilerParams(
            dimension_semantics=("parallel", "arbitrary"),
            vmem_limit_bytes=48 * MIB),
        name="sb_prompt",
    )(q, k_hm, v_hm)


def _sb_decode_kernel(q_ref, kn_ref, vn_ref, kp_ref, vp_ref, o_ref, *, tk, scale):
    n_heads, tq, d = kn_ref.shape
    qs, carries, accs = [], [], []
    for h in range(n_heads):
        q_b = q_ref[:, h * d:(h + 1) * d]
        carry, acc = _sb_block(q_b, kn_ref[h], vn_ref[h], jnp.zeros((tq, 1), F32), jnp.zeros((tq, d), F32),
                               scale=scale, diagonal=True)
        qs.append(q_b)
        carries.append(carry)
        accs.append(acc)
    past_len = kp_ref.shape[2]
    accs = _sb_sweep(qs, [kp_ref.at[0, h] for h in range(n_heads)], [vp_ref.at[0, h] for h in range(n_heads)],
                     past_len // tk - 1, carries, accs, tk=tk, scale=scale)
    for h in range(n_heads):
        o_ref[:, h * d:(h + 1) * d] = accs[h].astype(o_ref.dtype)


def _sb_decode(q, k_hm, v_hm, past_k, past_v, *, batch, seq, heads_per_step=4):
    n_heads, _, d = k_hm.shape
    hb = heads_per_step if n_heads % heads_per_step == 0 else 1
    past_len = past_k.shape[2]
    tk = _tile(past_len, 256)
    new_spec = pl.BlockSpec((hb, seq, d), lambda b, g: (g, b, 0))
    past_spec = pl.BlockSpec((1, hb, past_len, d), lambda b, g: (b, g, 0, 0))
    return pl.pallas_call(
        functools.partial(_sb_decode_kernel, tk=tk, scale=float(d) ** -0.5),
        grid=(batch, n_heads // hb),
        in_specs=[pl.BlockSpec((seq, hb * d), lambda b, g: (b, g)), new_spec, new_spec, past_spec, past_spec],
        out_specs=pl.BlockSpec((seq, hb * d), lambda b, g: (b, g)),
        out_shape=jax.ShapeDtypeStruct((batch * seq, n_heads * d), BF16),
        compiler_params=pltpu.CompilerParams(dimension_semantics=("parallel", "parallel")),
        name="sb_decode",
    )(q, k_hm, v_hm, past_k, past_v)


ROUTE_E1, ROUTE_E2, ROUTE_W1, ROUTE_W2 = 0, 1, 2, 3


def _route(logits, *, n_experts, n_groups):
    per_group = n_experts // n_groups
    lane = lax.broadcasted_iota(jnp.int32, logits.shape, 1)
    big = jnp.int32(LANES)
    is_group = jnp.logical_and(lane >= n_experts, lane < n_experts + n_groups)
    lg = jnp.where(is_group, logits, -jnp.inf)
    lg_max = jnp.max(lg, axis=-1, keepdims=True)
    p_sel = 1.0 / jnp.sum(jnp.exp(lg - lg_max), axis=-1, keepdims=True)
    grp = jnp.min(jnp.where(lg == lg_max, lane - n_experts, big), axis=-1, keepdims=True)
    in_grp = jnp.logical_and(lane >= grp * per_group, lane < (grp + 1) * per_group)
    le = jnp.where(in_grp, logits, -jnp.inf)
    v1 = jnp.max(le, axis=-1, keepdims=True)
    i1 = jnp.min(jnp.where(le == v1, lane, big), axis=-1, keepdims=True)
    le2 = jnp.where(lane == i1, -jnp.inf, le)
    v2 = jnp.max(le2, axis=-1, keepdims=True)
    i2 = jnp.min(jnp.where(le2 == v2, lane, big), axis=-1, keepdims=True)
    e2 = jnp.exp(v2 - v1)
    w1 = p_sel / (1.0 + e2)
    w2 = p_sel * e2 / (1.0 + e2)
    rec = jnp.where(lane == ROUTE_E1, i1.astype(F32), 0.0)
    rec = jnp.where(lane == ROUTE_E2, i2.astype(F32), rec)
    rec = jnp.where(lane == ROUTE_W1, w1, rec)
    return jnp.where(lane == ROUTE_W2, w2, rec)


def _merge_kernel(x_ref, ha_ref, om_ref, hs_ref, gm_ref, gs_ref, wbm_ref, wbs_ref, wout_ref, g_ref, b_ref,
                  wr_ref, br_ref, *refs, alpha, n_experts, n_groups, own_blocks):
    o_ref, route_ref = refs[-2:]
    i = pl.program_id(0)

    @pl.when(i < own_blocks)
    def _():
        h_m = (jax.nn.sigmoid(om_ref[...]) * ha_ref[...]).astype(BF16)
        t_m = jnp.dot(h_m, wbm_ref[...], preferred_element_type=F32)
        t_s = jnp.dot(hs_ref[...], wbs_ref[...], preferred_element_type=F32)
        merged = jax.nn.sigmoid(gm_ref[...]) * t_m + jax.nn.sigmoid(gs_ref[...]) * t_s
        y = jnp.dot(merged.astype(BF16), wout_ref[...], preferred_element_type=F32)
        x1 = _layer_norm(alpha * x_ref[...] + y, g_ref[...], b_ref[...])
        o_ref[...] = x1
        x_hi, x_lo = _bf16_pieces(x1, 2)
        hi_both = jnp.dot(x_hi, wr_ref[...], preferred_element_type=F32)
        lo_hi = jnp.dot(x_lo, wr_ref[:, :LANES], preferred_element_type=F32)
        logits = (hi_both[:, :LANES] + (hi_both[:, LANES:] + lo_hi)) + br_ref[...]
        route_ref[...] = _route(logits, n_experts=n_experts, n_groups=n_groups)

    @pl.when(i >= own_blocks)
    def _():
        o_ref[...] = jnp.zeros_like(o_ref)
        route_ref[...] = jnp.zeros_like(route_ref)


def _merge(x, h_a, proj, h_s, w_bm, w_bs, w_out, ln_g, ln_b, w_router, b_router, *,
           o_col, gm_col, gs_col, alpha, n_experts, n_groups, total_rows, row0, prev):
    n, dm = x.shape
    da, ds = h_a.shape[1], h_s.shape[1]
    tm = _tile(math.gcd(n, total_rows - n) if total_rows > n else n, 256)
    assert row0 % tm == 0 and total_rows % tm == 0 and (prev is not None or row0 == 0)
    blk0 = row0 // tm
    own = n // tm
    steps = own if prev is not None else total_rows // tm
    const = lambda shape: pl.BlockSpec(shape, lambda i: (0, 0), pipeline_mode=pl.Buffered(1))
    rows = lambda i: jnp.minimum(i, own - 1)
    prev = () if prev is None else tuple(prev)
    n_in = 13
    return pl.pallas_call(
        functools.partial(_merge_kernel, alpha=alpha, n_experts=n_experts, n_groups=n_groups, own_blocks=own),
        grid=(steps,),
        in_specs=[
            pl.BlockSpec((tm, dm), lambda i: (rows(i), 0)),
            pl.BlockSpec((tm, da), lambda i: (rows(i), 0)),
            pl.BlockSpec((tm, da), lambda i: (rows(i), o_col // da)),
            pl.BlockSpec((tm, ds), lambda i: (rows(i), 0)),
            pl.BlockSpec((tm, dm), lambda i: (rows(i), gm_col // dm)),
            pl.BlockSpec((tm, dm), lambda i: (rows(i), gs_col // dm)),
            const((da, dm)), const((ds, dm)), const((dm, dm)), const((1, dm)), const((1, dm)),
            const((dm, 2 * LANES)), const((1, LANES)),
        ] + [pl.BlockSpec(memory_space=pl.ANY)] * len(prev),
        out_specs=[pl.BlockSpec((tm, dm), lambda i: (blk0 + i, 0)),
                   pl.BlockSpec((tm, LANES), lambda i: (blk0 + i, 0))],
        out_shape=[jax.ShapeDtypeStruct((total_rows, dm), F32), jax.ShapeDtypeStruct((total_rows, LANES), F32)],
        input_output_aliases={n_in + k: k for k in range(len(prev))},
        compiler_params=pltpu.CompilerParams(dimension_semantics=("parallel",), vmem_limit_bytes=48 * MIB),
        name="merge_out_ln1",
    )(x, h_a, proj, h_s, proj, proj, w_bm, w_bs, w_out, ln_g, ln_b, w_router, b_router, *prev)


def _expert_kernel(te_ref, tb_ref, tv_ref, na_ref, perm_ref,
                   x_hbm, wg_ref, wu_ref, wd_ref, y_hbm,
                   xbuf, obuf, wgb, wub, wdb, gsem, ssem, *, n_tok):
    j = pl.program_id(0)
    n_act = na_ref[0]
    slot = lax.rem(j, 2)

    def gather_copy(s, i, token):
        return pltpu.make_async_copy(x_hbm.at[pl.ds(token, 1), :], xbuf.at[s, pl.ds(i, 1), :], gsem.at[s])

    def scatter_copy(s, i, row):
        return pltpu.make_async_copy(obuf.at[s, pl.ds(i, 1), :], y_hbm.at[pl.ds(row, 1), :], ssem.at[s])

    def for_rows(count, fn):
        groups = lax.shift_right_logical(count, 3)

        def group_body(g, c):
            for u in range(SUBLANES):
                fn(g * SUBLANES + u)
            return c

        def row_body(i, c):
            fn(i)
            return c

        lax.fori_loop(0, groups, group_body, 0)
        lax.fori_loop(groups * SUBLANES, count, row_body, 0)

    def start_gather(t, s):
        base = tb_ref[t]

        def start(i):
            pair = perm_ref[base + i]
            gather_copy(s, i, jnp.where(pair >= n_tok, pair - n_tok, pair)).start()

        for_rows(tv_ref[t], start)

    def start_scatter(t, s):
        base = tb_ref[t]
        for_rows(tv_ref[t], lambda i: scatter_copy(s, i, perm_ref[base + i]).start())

    def wait_rows(count, block_copy, row_copy):
        bulk = pl.multiple_of(lax.shift_right_logical(count, 3) * SUBLANES, SUBLANES)

        @pl.when(bulk > 0)
        def _():
            block_copy(pl.ds(0, bulk)).wait()

        def row_body(i, c):
            row_copy(i).wait()
            return c

        lax.fori_loop(bulk, count, row_body, 0)

    def wait_gather(t, s):
        wait_rows(tv_ref[t],
                  lambda rows: pltpu.make_async_copy(x_hbm.at[rows, :], xbuf.at[s, rows, :], gsem.at[s]),
                  lambda i: gather_copy(s, i, 0))

    def wait_scatter(t, s):
        wait_rows(tv_ref[t],
                  lambda rows: pltpu.make_async_copy(obuf.at[s, rows, :], y_hbm.at[rows, :], ssem.at[s]),
                  lambda i: scatter_copy(s, i, 0))

    @pl.when(j == 0)
    def _():
        xbuf[...] = jnp.zeros_like(xbuf)
        start_gather(0, 0)

    @pl.when(j + 1 < n_act)
    def _():
        start_gather(j + 1, 1 - slot)

    @pl.when(j < n_act)
    def _():
        wait_gather(j, slot)

        @pl.when(j >= 2)
        def _():
            wait_scatter(j - 2, slot)

        @pl.when(jnp.logical_or(j == 0, te_ref[j] != te_ref[jnp.maximum(j - 1, 0)]))
        def _():
            wgb[...] = wg_ref[0].astype(BF16)
            wub[...] = wu_ref[0].astype(BF16)
            wdb[...] = wd_ref[0].astype(BF16)

        x = xbuf[slot].astype(BF16)
        gate = jnp.dot(x, wgb[...], preferred_element_type=F32)
        up = jnp.dot(x, wub[...], preferred_element_type=F32)
        hid = (gate * jax.nn.sigmoid(gate) * up).astype(BF16)
        obuf[slot] = jnp.dot(hid, wdb[...], preferred_element_type=F32)
        start_scatter(j, slot)

        @pl.when(j == n_act - 1)
        def _():
            @pl.when(j >= 1)
            def _():
                wait_scatter(j - 1, 1 - slot)

            wait_scatter(j, slot)


def _expert_tables(route, n_experts, tile):
    n_tok = route.shape[0]
    i32 = jnp.int32
    e_flat = jnp.concatenate([route[:, ROUTE_E1], route[:, ROUTE_E2]]).astype(i32)
    n_pad = (1 << (2 * n_tok - 1).bit_length()) - 2 * n_tok
    keys = jnp.concatenate([e_flat, jnp.full((n_pad,), n_experts, i32)])
    perm = jnp.argsort(keys, stable=True)[:2 * n_tok].astype(i32)
    eids = jnp.arange(n_experts, dtype=i32)
    counts = jnp.sum((e_flat[:, None] == eids[None, :]).astype(i32), axis=0)
    tiles_e = (counts + (tile - 1)) // tile
    before = eids[None, :] < eids[:, None]
    pair_start = jnp.sum(jnp.where(before, counts[None, :], 0), axis=1)
    tile_start = jnp.sum(jnp.where(before, tiles_e[None, :], 0), axis=1)
    n_act = jnp.sum(tiles_e)
    t_max = (2 * n_tok) // tile + n_experts
    j = jnp.arange(t_max, dtype=i32)
    owner = jnp.logical_and(j[:, None] >= tile_start[None, :], j[:, None] < (tile_start + tiles_e)[None, :])
    pick = lambda table: jnp.sum(jnp.where(owner, table[None, :], 0), axis=1)
    active = j < n_act
    k = j - pick(tile_start)
    last_e = jnp.max(jnp.where(tiles_e > 0, eids, 0))
    te = jnp.where(active, pick(eids), last_e).astype(i32)
    tb = jnp.where(active, pick(pair_start) + k * tile, 0).astype(i32)
    tv = jnp.where(active, jnp.clip(pick(counts) - k * tile, 0, tile), 0).astype(i32)
    return te, tb, tv, n_act.reshape(1).astype(i32), perm, t_max


def _experts(x1, route, w_gate, w_up, w_down):
    n_tok, dm = x1.shape
    n_experts, _, de = w_gate.shape
    tile = 256
    te, tb, tv, n_act, perm, t_max = _expert_tables(route, n_experts, tile)
    w_in_spec = pl.BlockSpec((1, dm, de), lambda j, te, tb, tv, na, perm: (te[j], 0, 0))
    w_out_spec = pl.BlockSpec((1, de, dm), lambda j, te, tb, tv, na, perm: (te[j], 0, 0))
    grid_spec = pltpu.PrefetchScalarGridSpec(
        num_scalar_prefetch=5,
        grid=(t_max,),
        in_specs=[pl.BlockSpec(memory_space=pl.ANY), w_in_spec, w_in_spec, w_out_spec],
        out_specs=pl.BlockSpec(memory_space=pl.ANY),
        scratch_shapes=[
            pltpu.VMEM((2, tile, dm), F32), pltpu.VMEM((2, tile, dm), F32),
            pltpu.VMEM((dm, de), BF16), pltpu.VMEM((dm, de), BF16), pltpu.VMEM((de, dm), BF16),
            pltpu.SemaphoreType.DMA((2,)), pltpu.SemaphoreType.DMA((2,)),
        ],
    )
    return pl.pallas_call(
        functools.partial(_expert_kernel, n_tok=n_tok),
        grid_spec=grid_spec,
        out_shape=jax.ShapeDtypeStruct((2 * n_tok, dm), F32),
        compiler_params=pltpu.CompilerParams(dimension_semantics=("arbitrary",), vmem_limit_bytes=48 * MIB),
        name="moe_experts",
    )(te, tb, tv, n_act, perm, x1, w_gate, w_up, w_down)


def _combine_kernel(x_ref, ya_ref, yb_ref, r_ref, g_ref, b_ref, first_ref, second_ref, *, alpha, first_blocks):
    r = r_ref[...]
    moe = r[:, ROUTE_W1:ROUTE_W1 + 1] * ya_ref[...] + r[:, ROUTE_W2:ROUTE_W2 + 1] * yb_ref[...]
    out = _layer_norm(alpha * x_ref[...] + moe, g_ref[...], b_ref[...])
    i = pl.program_id(0)

    @pl.when(i < first_blocks)
    def _():
        first_ref[...] = out

    @pl.when(i >= first_blocks)
    def _():
        second_ref[...] = out


def _combine(x1, y_pairs, route, ln_g, ln_b, *, alpha, n_first):
    n, dm = x1.shape
    tm = _tile(math.gcd(n_first, n - n_first), 256)
    nb, nb_first = n // tm, n_first // tm
    const = lambda shape: pl.BlockSpec(shape, lambda i: (0, 0))
    return pl.pallas_call(
        functools.partial(_combine_kernel, alpha=alpha, first_blocks=nb_first),
        grid=(nb,),
        in_specs=[
            pl.BlockSpec((tm, dm), lambda i: (i, 0)),
            pl.BlockSpec((tm, dm), lambda i: (i, 0)),
            pl.BlockSpec((tm, dm), lambda i: (nb + i, 0)),
            pl.BlockSpec((tm, LANES), lambda i: (i, 0)),
            const((1, dm)), const((1, dm)),
        ],
        out_specs=[pl.BlockSpec((tm, dm), lambda i: (jnp.minimum(i, nb_first - 1), 0)),
                   pl.BlockSpec((tm, dm), lambda i: (jnp.maximum(i - nb_first, 0), 0))],
        out_shape=[jax.ShapeDtypeStruct((n_first, dm), F32), jax.ShapeDtypeStruct((n - n_first, dm), F32)],
        compiler_params=pltpu.CompilerParams(dimension_semantics=("arbitrary",), vmem_limit_bytes=48 * MIB),
        name="moe_combine_ln2",
    )(x1, y_pairs, y_pairs, route, ln_g, ln_b)


def _mixer(x, conv_state, c0, n0, m0, past_k, past_v, p, *, chunk_pref, total_rows, row0, prev):
    batch, seq, dm = x.shape
    n_tok = batch * seq
    xf = x.reshape(n_tok, dm)
    n_heads, d = p["n_heads"], p["head_dim"]
    da = n_heads * d
    cols = p["cols"]

    proj, q_s, k_hm, k_hm16, v_hm, v_hm16, gates = _project(
        xf, p["w_all"], main_cols=p["main_cols"], sb_cols=p["sb_cols"], tn=p["tn"])

    chunk = _tile(seq, chunk_pref)
    h_a, c_new, n_new, m_new = _mlstm(
        proj, gates, p["b_gate_row"], p["conv_w"], p["conv_b"], conv_state, c0, n0, m0,
        batch=batch, seq=seq, chunk=chunk, n_heads=n_heads, head_dim=d,
        q_col=cols["q"], k_col=cols["k"], v_col=cols["v"])

    if past_k is None:
        assert batch == 1
        h_s = _sb_prompt(q_s, k_hm16, v_hm16, seq=seq)
    else:
        h_s = _sb_decode(q_s, k_hm16, v_hm16, past_k, past_v, batch=batch, seq=seq)

    x1, route = _merge(xf, h_a, proj, h_s, p["w_bm"], p["w_bs"], p["w_out"], p["ln1_g"], p["ln1_b"],
                       p["w_router"], p["b_router"],
                       o_col=cols["o"], gm_col=cols["gm"], gs_col=cols["gs"],
                       alpha=p["alpha"], n_experts=p["n_experts"], n_groups=p["n_groups"],
                       total_rows=total_rows, row0=row0, prev=prev)

    n_sb = k_hm.shape[0]
    sb_k = k_hm.reshape(n_sb, batch, seq, -1).transpose(1, 0, 2, 3)
    sb_v = v_hm.reshape(n_sb, batch, seq, -1).transpose(1, 0, 2, 3)
    new_conv = proj.reshape(batch, seq, -1)[:, seq - (CONV_W - 1):, cols["q"]:cols["q"] + 2 * da]
    return x1, route, (sb_k, sb_v, c_new, n_new, m_new, new_conv)


def _prepare_layer(l, depth, w_in, b_gate, conv_w, conv_b, w_branch_m, w_branch_s, w_out, ln1_g, ln1_b,
                   w_router_group, b_router_group, w_router_expert, b_router_expert,
                   w_exp_gate, w_exp_up, w_exp_down, ln2_g, ln2_b, head_dim):
    dm = w_in.shape[1]
    da2 = conv_w.shape[-1]
    da = da2 // 2
    n_heads = b_gate.shape[-1] // 2
    ds = w_branch_s.shape[1]
    sizes = (da2, da, da, 2 * n_heads, ds, ds, ds, dm, dm)
    offs = [0]
    for s in sizes:
        offs.append(offs[-1] + s)
    w = w_in[l]
    part = lambda i: w[:, offs[i]:offs[i + 1]]
    main_cols = da2 + 2 * da + 2 * dm
    tn = math.gcd(math.gcd(512, main_cols), ds)
    w_all = jnp.concatenate(
        [part(i).astype(BF16) for i in (0, 1, 2, 7, 8, 4, 5, 6)]
        + [jnp.pad(part(3).astype(BF16), ((0, 0), (0, tn - 2 * n_heads)))], axis=1)
    cols = {"q": 0, "k": da, "v": da2, "o": da2 + da, "gm": da2 + 2 * da, "gs": da2 + 2 * da + dm}
    assert cols["k"] % da == 0 and cols["o"] % da == 0 and cols["gm"] % dm == 0 and cols["gs"] % dm == 0
    b_gate_row = jnp.pad(b_gate[l].astype(F32), (0, LANES - 2 * n_heads)).reshape(1, LANES)
    n_groups = w_router_group.shape[-1]
    n_experts = w_router_expert.shape[-1]
    pad_r = LANES - n_experts - n_groups
    w_router = jnp.pad(jnp.concatenate([w_router_expert[l], w_router_group[l]], axis=1),
                       ((0, 0), (0, pad_r))).astype(F32)
    w_router_hi = w_router.astype(BF16)
    w_router_lo = (w_router - w_router_hi.astype(F32)).astype(BF16)
    b_router = jnp.pad(jnp.concatenate([b_router_expert[l], b_router_group[l]]), (0, pad_r)).reshape(1, LANES)
    return {
        "n_heads": n_heads, "head_dim": head_dim, "cols": cols, "n_groups": n_groups, "n_experts": n_experts,
        "alpha": (2.0 * depth) ** 0.25, "main_cols": main_cols, "sb_cols": ds, "tn": tn,
        "w_all": w_all, "b_gate_row": b_gate_row,
        "conv_w": conv_w[l], "conv_b": conv_b[l],
        "w_bm": w_branch_m[l].astype(BF16), "w_bs": w_branch_s[l].astype(BF16), "w_out": w_out[l].astype(BF16),
        "ln1_g": ln1_g[l].reshape(1, dm), "ln1_b": ln1_b[l].reshape(1, dm),
        "w_router": jnp.concatenate([w_router_hi, w_router_lo], axis=1), "b_router": b_router.astype(F32),
        "w_eg": w_exp_gate[l], "w_eu": w_exp_up[l], "w_ed": w_exp_down[l],
        "ln2_g": ln2_g[l].reshape(1, dm), "ln2_b": ln2_b[l].reshape(1, dm),
    }


def kernel(x_prompt, x_sample, cache_sb_k, cache_sb_v, state_mlstm_C, state_mlstm_n, state_mlstm_m, state_conv,
           w_in, b_gate, conv_w, conv_b, w_branch_m, w_branch_s, w_out, ln1_g, ln1_b,
           w_router_group, b_router_group, w_router_expert, b_router_expert,
           w_exp_gate, w_exp_up, w_exp_down, ln2_g, ln2_b):
    depth = w_in.shape[0]
    bp = x_prompt.shape[0]
    n_heads, head_dim = state_mlstm_C.shape[2], state_mlstm_C.shape[3]
    da2 = conv_w.shape[-1]
    xp, xs = x_prompt, x_sample
    st_p_all, st_s_all = [], []
    for l in range(depth):
        p = _prepare_layer(l, depth, w_in, b_gate, conv_w, conv_b, w_branch_m, w_branch_s, w_out, ln1_g, ln1_b,
                           w_router_group, b_router_group, w_router_expert, b_router_expert,
                           w_exp_gate, w_exp_up, w_exp_down, ln2_g, ln2_b, head_dim)
        n_p, n_s = xp.shape[0] * xp.shape[1], xs.shape[0] * xs.shape[1]
        x1, route, st_p = _mixer(
            xp, jnp.zeros((bp, CONV_W - 1, da2), F32), jnp.zeros((bp, n_heads, head_dim, head_dim), F32),
            jnp.zeros((bp, n_heads, head_dim), F32), jnp.full((bp, n_heads), NEG, F32), None, None, p,
            chunk_pref=256, total_rows=n_p + n_s, row0=0, prev=None)
        x1, route, st_s = _mixer(xs, state_conv[l], state_mlstm_C[l], state_mlstm_n[l], state_mlstm_m[l],
                                 cache_sb_k[l], cache_sb_v[l], p, chunk_pref=256,
                                 total_rows=n_p + n_s, row0=n_p, prev=(x1, route))
        y_pairs = _experts(x1, route, p["w_eg"], p["w_eu"], p["w_ed"])
        xp2, xs2 = _combine(x1, y_pairs, route, p["ln2_g"], p["ln2_b"], alpha=p["alpha"], n_first=n_p)
        xp, xs = xp2.reshape(xp.shape), xs2.reshape(xs.shape)
        st_p_all.append(st_p)
        st_s_all.append(st_s)
    stack = lambda states, i: jnp.stack([s[i] for s in states], axis=0)
    return (xp, xs,
            *(stack(st_p_all, i) for i in range(6)),
            *(stack(st_s_all, i) for i in range(6)))
```

```python
import functools
import math

import jax
import jax.numpy as jnp
from jax import lax
from jax.experimental import pallas as pl
from jax.experimental.pallas import tpu as pltpu

F32 = jnp.float32
BF16 = jnp.bfloat16

LANES = 128
SUBLANES = 8
LN_EPS = 1e-5
NEG = -1e30
CONV_W = 4
EXP_ZERO_BELOW = -104.0
MIB = 1024 * 1024

_NT = (((1,), (1,)), ((), ()))
_TN = (((0,), (0,)), ((), ()))


def _tile(n, pref):
    if n <= pref:
        return n
    t = pref
    while t >= SUBLANES:
        if n % t == 0 and t % SUBLANES == 0:
            return t
        t -= SUBLANES
    return n


def _log_sigmoid(x):
    return jnp.minimum(x, 0.0) - jnp.log1p(jnp.exp(-jnp.abs(x)))


def _layer_norm(x, g, b):
    mu = jnp.mean(x, axis=-1, keepdims=True)
    d = x - mu
    var = jnp.mean(d * d, axis=-1, keepdims=True)
    return d * lax.rsqrt(var + LN_EPS) * g + b


def _bf16_pieces(x, n):
    pieces = []
    rem = x
    for _ in range(n - 1):
        p = rem.astype(BF16)
        pieces.append(p)
        rem = rem - p.astype(F32)
    pieces.append(rem.astype(BF16))
    return pieces


def _dot_pieces(pieces, other, *, pieces_first, dims=None):
    total = None
    for p in pieces:
        a, b = (p, other) if pieces_first else (other, p)
        if dims is None:
            t = jnp.dot(a, b, preferred_element_type=F32)
        else:
            t = lax.dot_general(a, b, dims, preferred_element_type=F32)
        total = t if total is None else total + t
    return total


def _proj_kernel(x_ref, w_ref, wg_ref, main_ref, qs_ref, k32_ref, k16_ref, v32_ref, v16_ref, g_ref, xb_ref, *, ends):
    j = pl.program_id(1)
    end_main, end_qs, end_k = ends

    @pl.when(j == 0)
    def _():
        xb_ref[...] = x_ref[...].astype(BF16)
        g_ref[...] = jnp.dot(xb_ref[...], wg_ref[...], preferred_element_type=F32)

    acc = jnp.dot(xb_ref[...], w_ref[...], preferred_element_type=F32)

    def store_heads(f32_ref, bf16_ref):
        for h in range(f32_ref.shape[0]):
            piece = acc[:, h * LANES:(h + 1) * LANES]
            f32_ref[h] = piece
            bf16_ref[h] = piece.astype(BF16)

    @pl.when(j < end_main)
    def _():
        main_ref[...] = acc

    @pl.when(jnp.logical_and(j >= end_main, j < end_qs))
    def _():
        qs_ref[...] = acc.astype(BF16)

    @pl.when(jnp.logical_and(j >= end_qs, j < end_k))
    def _():
        store_heads(k32_ref, k16_ref)

    @pl.when(j >= end_k)
    def _():
        store_heads(v32_ref, v16_ref)


def _project(x, w_all, w_gate, *, main_cols, sb_cols, tn):
    n, k = x.shape
    tm = _tile(n, 1024)
    t_main, t_sb = main_cols // tn, sb_cols // tn
    ends = (t_main, t_main + t_sb, t_main + 2 * t_sb)
    n_tiles = t_main + 3 * t_sb
    assert w_all.shape[1] == n_tiles * tn
    local = lambda j, start, count: jnp.clip(j - start, 0, count - 1)
    heads = tn // LANES
    n_sb = sb_cols // LANES
    once = pl.Buffered(1)
    hm_spec = lambda start: pl.BlockSpec((heads, tm, LANES), lambda i, j: (local(j, start, t_sb), i, 0),
                                         pipeline_mode=once)
    return pl.pallas_call(
        functools.partial(_proj_kernel, ends=ends),
        grid=(n // tm, n_tiles),
        in_specs=[pl.BlockSpec((tm, k), lambda i, j: (i, 0), pipeline_mode=once),
                  pl.BlockSpec((k, tn), lambda i, j: (0, j)),
                  pl.BlockSpec((k, LANES), lambda i, j: (0, 0), pipeline_mode=once)],
        out_specs=[
            pl.BlockSpec((tm, tn), lambda i, j: (i, local(j, 0, t_main))),
            pl.BlockSpec((tm, tn), lambda i, j: (i, local(j, ends[0], t_sb)), pipeline_mode=once),
            hm_spec(ends[1]), hm_spec(ends[1]), hm_spec(ends[2]), hm_spec(ends[2]),
            pl.BlockSpec((tm, LANES), lambda i, j: (i, 0), pipeline_mode=once),
        ],
        out_shape=[
            jax.ShapeDtypeStruct((n, main_cols), F32),
            jax.ShapeDtypeStruct((n, sb_cols), BF16),
            jax.ShapeDtypeStruct((n_sb, n, LANES), F32), jax.ShapeDtypeStruct((n_sb, n, LANES), BF16),
            jax.ShapeDtypeStruct((n_sb, n, LANES), F32), jax.ShapeDtypeStruct((n_sb, n, LANES), BF16),
            jax.ShapeDtypeStruct((n, LANES), F32),
        ],
        scratch_shapes=[pltpu.VMEM((tm, k), BF16)],
        compiler_params=pltpu.CompilerParams(
            dimension_semantics=("parallel", "arbitrary"),
            vmem_limit_bytes=56 * MIB),
        name="in_proj",
    )(x, w_all, w_gate)


def _mlstm_kernel(q_ref, k_ref, v_ref, g_ref, bg_ref, cwq_ref, cwk_ref, cbq_ref, cbk_ref,
                  csq_ref, csk_ref, c0_ref, n0_ref, m0_ref,
                  h_ref, c_ref, n_ref, m_ref, qext_ref, kext_ref, *, chunk, n_heads, head_dim):
    step = pl.program_id(1)
    L, d = chunk, head_dim
    halo = CONV_W - 1
    base = SUBLANES - halo

    @pl.when(step == 0)
    def _():
        qext_ref[base:SUBLANES, :] = csq_ref[0]
        kext_ref[base:SUBLANES, :] = csk_ref[0]
        c_ref[...] = c0_ref[...]
        n_ref[...] = n0_ref[...]
        m_ref[...] = m0_ref[...]

    def conv_silu(raw_ref, ext_ref, cw_ref, cb_ref):
        ext_ref[SUBLANES:SUBLANES + L, :] = raw_ref[...]
        acc = cb_ref[...] + ext_ref[base:base + L, :] * cw_ref[0:1, :]
        for j in range(1, CONV_W):
            acc = acc + ext_ref[base + j:base + j + L, :] * cw_ref[j:j + 1, :]
        ext_ref[base:SUBLANES, :] = ext_ref[base + L:SUBLANES + L, :]
        return acc * jax.nn.sigmoid(acc)

    q_all = conv_silu(q_ref, qext_ref, cwq_ref, cbq_ref)
    k_all = conv_silu(k_ref, kext_ref, cwk_ref, cbk_ref) * (float(d) ** -0.5)

    lane = lax.broadcasted_iota(jnp.int32, (L, LANES), 1)
    pre = g_ref[...] + bg_ref[...]
    gates = jnp.where(lane < n_heads, pre, _log_sigmoid(pre))
    row_i = lax.broadcasted_iota(jnp.int32, (L, L), 0)
    col_i = lax.broadcasted_iota(jnp.int32, (L, L), 1)
    causal = col_i <= row_i
    tril = causal.astype(BF16)
    triu = (row_i <= col_i).astype(BF16)
    sel = (lax.broadcasted_iota(jnp.int32, (SUBLANES, LANES), 0)
           == lax.broadcasted_iota(jnp.int32, (SUBLANES, LANES), 1)).astype(BF16)
    gate_pieces = _bf16_pieces(gates, 3)
    cum_cols = _dot_pieces(gate_pieces, tril, pieces_first=False)
    gates_t = _dot_pieces(gate_pieces, sel, pieces_first=False, dims=_NT)
    cum_rows = _dot_pieces(_bf16_pieces(gates_t, 3), triu, pieces_first=True)

    for h in range(n_heads):
        cols = slice(h * d, (h + 1) * d)
        q = q_all[:, cols]
        k = k_all[:, cols]
        q_b = q.astype(BF16)
        k_b = k.astype(BF16)
        v_b = v_ref[:, cols].astype(BF16)
        i_col = gates[:, h:h + 1]
        b_col = cum_cols[:, n_heads + h:n_heads + h + 1]
        i_row = gates_t[h:h + 1, :]
        b_row = cum_rows[n_heads + h:n_heads + h + 1, :]
        c_prev = c_ref[0, h]
        n_prev = n_ref[0, h]
        m_prev = m_ref[0, h][:, 0:1]

        dmat = jnp.where(causal, b_col - b_row + i_row, NEG)
        inter = b_col + m_prev
        m_t = jnp.maximum(jnp.max(dmat, axis=-1, keepdims=True), inter)
        w = jnp.exp(dmat - m_t)
        s = lax.dot_general(q_b, k_b, _NT, preferred_element_type=F32) * w
        e_inter = jnp.exp(inter - m_t)
        num = (jnp.dot(s.astype(BF16), v_b, preferred_element_type=F32)
               + e_inter * jnp.dot(q_b, c_prev.astype(BF16), preferred_element_type=F32))
        den = jnp.sum(s, axis=-1, keepdims=True) + e_inter * jnp.sum(q * n_prev, axis=-1, keepdims=True)
        h_ref[:, cols] = num / jnp.maximum(jnp.abs(den), jnp.exp(-m_t))

        g_last = b_col[L - 1:L, :]
        w_last = g_last - b_col + i_col
        m_new = jnp.maximum(g_last + m_prev, jnp.max(w_last, axis=0, keepdims=True))
        decay = jnp.exp(g_last + m_prev - m_new)
        kw = jnp.exp(w_last - m_new) * k
        c_ref[0, h] = decay * c_prev + lax.dot_general(kw.astype(BF16), v_b, _TN, preferred_element_type=F32)
        n_ref[0, h] = decay * n_prev + jnp.sum(kw, axis=0, keepdims=True)
        m_ref[0, h] = jnp.broadcast_to(m_new, (1, LANES))


def _mlstm(proj, gates, b_gate_row, conv_w, conv_b, conv_state, c0, n0, m0, *, batch, seq, chunk,
           n_heads, head_dim, q_col, k_col, v_col):
    n_tok = batch * seq
    nc = seq // chunk
    d = head_dim
    da = n_heads * d
    qb, kb, vb = q_col // da, k_col // da, v_col // da
    kcw = (k_col - q_col) // da
    row = lambda b, c: b * nc + c
    n0r = n0.reshape(batch, n_heads, 1, d)
    m0r = jnp.broadcast_to(m0.reshape(batch, n_heads, 1, 1), (batch, n_heads, 1, LANES))
    cbr = conv_b.reshape(1, -1)
    state_spec = lambda shape: pl.BlockSpec(shape, lambda b, c: (b, 0, 0, 0))
    h, c_new, n_new, m_new = pl.pallas_call(
        functools.partial(_mlstm_kernel, chunk=chunk, n_heads=n_heads, head_dim=d),
        grid=(batch, nc),
        in_specs=[
            pl.BlockSpec((chunk, da), lambda b, c: (row(b, c), qb)),
            pl.BlockSpec((chunk, da), lambda b, c: (row(b, c), kb)),
            pl.BlockSpec((chunk, da), lambda b, c: (row(b, c), vb)),
            pl.BlockSpec((chunk, LANES), lambda b, c: (row(b, c), 0)),
            pl.BlockSpec((1, LANES), lambda b, c: (0, 0)),
            pl.BlockSpec((CONV_W, da), lambda b, c: (0, 0)),
            pl.BlockSpec((CONV_W, da), lambda b, c: (0, kcw)),
            pl.BlockSpec((1, da), lambda b, c: (0, 0)),
            pl.BlockSpec((1, da), lambda b, c: (0, kcw)),
            pl.BlockSpec((1, CONV_W - 1, da), lambda b, c: (b, 0, 0)),
            pl.BlockSpec((1, CONV_W - 1, da), lambda b, c: (b, 0, kcw)),
            state_spec((1, n_heads, d, d)),
            state_spec((1, n_heads, 1, d)),
            state_spec((1, n_heads, 1, LANES)),
        ],
        out_specs=[
            pl.BlockSpec((chunk, da), lambda b, c: (row(b, c), 0)),
            state_spec((1, n_heads, d, d)),
            state_spec((1, n_heads, 1, d)),
            state_spec((1, n_heads, 1, LANES)),
        ],
        out_shape=[
            jax.ShapeDtypeStruct((n_tok, da), F32),
            jax.ShapeDtypeStruct((batch, n_heads, d, d), F32),
            jax.ShapeDtypeStruct((batch, n_heads, 1, d), F32),
            jax.ShapeDtypeStruct((batch, n_heads, 1, LANES), F32),
        ],
        scratch_shapes=[pltpu.VMEM((chunk + SUBLANES, da), F32), pltpu.VMEM((chunk + SUBLANES, da), F32)],
        compiler_params=pltpu.CompilerParams(
            dimension_semantics=("parallel", "arbitrary"),
            vmem_limit_bytes=32 * MIB),
        name="mlstm",
    )(proj, proj, proj, gates, b_gate_row, conv_w, conv_w, cbr, cbr, conv_state, conv_state, c0, n0r, m0r)
    return h, c_new, n_new.reshape(batch, n_heads, d), m_new[:, :, 0, 0]


def _sb_scores(q_b, k_b, *, scale, diagonal):
    tq, tk = q_b.shape[0], k_b.shape[0]
    z = lax.dot_general(q_b, k_b, _NT, preferred_element_type=F32) * scale
    softplus = jnp.maximum(z, 0.0) + jnp.log(1.0 + jnp.exp(-jnp.abs(z)))
    log_beta = z - softplus
    drop, mask = softplus, None
    if diagonal:
        mask = (lax.broadcasted_iota(jnp.int32, (tq, tk), 1) < lax.broadcasted_iota(jnp.int32, (tq, tk), 0))
        drop = jnp.where(mask, softplus, 0.0)
    later_mat = (lax.broadcasted_iota(jnp.int32, (tk, tk), 0)
                 > lax.broadcasted_iota(jnp.int32, (tk, tk), 1)).astype(BF16)
    later = _dot_pieces(_bf16_pieces(drop, 2), later_mat, pieces_first=True)
    total = later[:, 0:1] + drop[:, 0:1]
    return log_beta, later, total, mask


def _sb_weights(scores, v_b, carry, acc):
    log_beta, later, total, mask = scores
    a = jnp.exp(log_beta - (later + carry))
    if mask is not None:
        a = jnp.where(mask, a, 0.0)
    return carry + total, acc + jnp.dot(a.astype(BF16), v_b, preferred_element_type=F32)


def _sb_block(q_b, k_b, v_b, carry, acc, *, scale, diagonal):
    return _sb_weights(_sb_scores(q_b, k_b, scale=scale, diagonal=diagonal), v_b, carry, acc)


def _sb_sweep(qs, k_ats, v_ats, first_block, carries, accs, *, tk, scale):
    def carry_min(cs):
        m = cs[0]
        for c in cs[1:]:
            m = jnp.minimum(m, c)
        return jnp.min(m)

    def cond(state):
        j, cmin, _, _ = state
        return jnp.logical_and(j >= 0, cmin < -EXP_ZERO_BELOW)

    def body(state):
        j, _, cs, acs = state
        start = pl.multiple_of(j * tk, tk)
        new = [_sb_block(q, k_at[pl.ds(start, tk), :].astype(BF16), v_at[pl.ds(start, tk), :].astype(BF16),
                         c, a, scale=scale, diagonal=False)
               for q, k_at, v_at, c, a in zip(qs, k_ats, v_ats, cs, acs)]
        cs = tuple(n[0] for n in new)
        return j - 1, carry_min(cs), cs, tuple(n[1] for n in new)

    carries = tuple(carries)
    _, _, _, accs = lax.while_loop(cond, body, (first_block, carry_min(carries), carries, tuple(accs)))
    return accs


def _sb_prompt_kernel(q_ref, k_ref, v_ref, o_ref, *, tq, scale):
    i = pl.program_id(1)
    n_heads, _, d = k_ref.shape
    start = pl.multiple_of(i * tq, tq)
    qs = [q_ref[:, h * d:(h + 1) * d] for h in range(n_heads)]
    zero_carry, zero_acc = jnp.zeros((tq, 1), F32), jnp.zeros((tq, d), F32)

    def finish(states, first_block):
        accs = _sb_sweep(qs, [k_ref.at[h] for h in range(n_heads)], [v_ref.at[h] for h in range(n_heads)],
                         first_block, [s[0] for s in states], [s[1] for s in states], tk=tq, scale=scale)
        for h in range(n_heads):
            o_ref[:, h * d:(h + 1) * d] = accs[h].astype(o_ref.dtype)

    @pl.when(i == 0)
    def _():
        finish([_sb_block(qs[h], k_ref[h, pl.ds(start, tq), :], v_ref[h, pl.ds(start, tq), :],
                          zero_carry, zero_acc, scale=scale, diagonal=True) for h in range(n_heads)], i - 1)

    @pl.when(i > 0)
    def _():
        prev = pl.multiple_of(start - tq, tq)
        states = []
        for h in range(n_heads):
            diag = _sb_scores(qs[h], k_ref[h, pl.ds(start, tq), :], scale=scale, diagonal=True)
            full = _sb_scores(qs[h], k_ref[h, pl.ds(prev, tq), :], scale=scale, diagonal=False)
            carry, acc = _sb_weights(diag, v_ref[h, pl.ds(start, tq), :], zero_carry, zero_acc)
            states.append(_sb_weights(full, v_ref[h, pl.ds(prev, tq), :], carry, acc))
        finish(states, i - 2)


def _sb_prompt(q, k_hm, v_hm, *, seq, heads_per_step=2):
    n_heads, _, d = k_hm.shape
    hb = heads_per_step if n_heads % heads_per_step == 0 else 1
    tq = _tile(seq, 256)
    kv_spec = pl.BlockSpec((hb, seq, d), lambda g, i: (g, 0, 0))
    return pl.pallas_call(
        functools.partial(_sb_prompt_kernel, tq=tq, scale=float(d) ** -0.5),
        grid=(n_heads // hb, seq // tq),
        in_specs=[pl.BlockSpec((tq, hb * d), lambda g, i: (i, g)), kv_spec, kv_spec],
        out_specs=pl.BlockSpec((tq, hb * d), lambda g, i: (i, g)),
        out_shape=jax.ShapeDtypeStruct((seq, n_heads * d), BF16),
        compiler_params=pltpu.CompilerParams(
            dimension_semantics=("parallel", "arbitrary"),
            vmem_limit_bytes=48 * MIB),
        name="sb_prompt",
    )(q, k_hm, v_hm)


def _sb_decode_kernel(q_ref, kn_ref, vn_ref, kp_ref, vp_ref, o_ref, *, tk, scale):
    n_heads, tq, d = kn_ref.shape
    qs, carries, accs = [], [], []
    for h in range(n_heads):
        q_b = q_ref[:, h * d:(h + 1) * d]
        carry, acc = _sb_block(q_b, kn_ref[h], vn_ref[h], jnp.zeros((tq, 1), F32), jnp.zeros((tq, d), F32),
                               scale=scale, diagonal=True)
        qs.append(q_b)
        carries.append(carry)
        accs.append(acc)
    past_len = kp_ref.shape[2]
    accs = _sb_sweep(qs, [kp_ref.at[0, h] for h in range(n_heads)], [vp_ref.at[0, h] for h in range(n_heads)],
                     past_len // tk - 1, carries, accs, tk=tk, scale=scale)
    for h in range(n_heads):
        o_ref[:, h * d:(h + 1) * d] = accs[h].astype(o_ref.dtype)


def _sb_decode(q, k_hm, v_hm, past_k, past_v, *, batch, seq, heads_per_step=4):
    n_heads, _, d = k_hm.shape
    hb = heads_per_step if n_heads % heads_per_step == 0 else 1
    past_len = past_k.shape[2]
    tk = _tile(past_len, 256)
    new_spec = pl.BlockSpec((hb, seq, d), lambda b, g: (g, b, 0))
    past_spec = pl.BlockSpec((1, hb, past_len, d), lambda b, g: (b, g, 0, 0))
    return pl.pallas_call(
        functools.partial(_sb_decode_kernel, tk=tk, scale=float(d) ** -0.5),
        grid=(batch, n_heads // hb),
        in_specs=[pl.BlockSpec((seq, hb * d), lambda b, g: (b, g)), new_spec, new_spec, past_spec, past_spec],
        out_specs=pl.BlockSpec((seq, hb * d), lambda b, g: (b, g)),
        out_shape=jax.ShapeDtypeStruct((batch * seq, n_heads * d), BF16),
        compiler_params=pltpu.CompilerParams(dimension_semantics=("parallel", "parallel")),
        name="sb_decode",
    )(q, k_hm, v_hm, past_k, past_v)


ROUTE_E1, ROUTE_E2, ROUTE_W1, ROUTE_W2 = 0, 1, 2, 3


def _route(logits, *, n_experts, n_groups):
    per_group = n_experts // n_groups
    lane = lax.broadcasted_iota(jnp.int32, logits.shape, 1)
    big = jnp.int32(LANES)
    is_group = jnp.logical_and(lane >= n_experts, lane < n_experts + n_groups)
    lg = jnp.where(is_group, logits, -jnp.inf)
    lg_max = jnp.max(lg, axis=-1, keepdims=True)
    p_sel = 1.0 / jnp.sum(jnp.exp(lg - lg_max), axis=-1, keepdims=True)
    grp = jnp.min(jnp.where(lg == lg_max, lane - n_experts, big), axis=-1, keepdims=True)
    in_grp = jnp.logical_and(lane >= grp * per_group, lane < (grp + 1) * per_group)
    le = jnp.where(in_grp, logits, -jnp.inf)
    v1 = jnp.max(le, axis=-1, keepdims=True)
    i1 = jnp.min(jnp.where(le == v1, lane, big), axis=-1, keepdims=True)
    le2 = jnp.where(lane == i1, -jnp.inf, le)
    v2 = jnp.max(le2, axis=-1, keepdims=True)
    i2 = jnp.min(jnp.where(le2 == v2, lane, big), axis=-1, keepdims=True)
    e2 = jnp.exp(v2 - v1)
    w1 = p_sel / (1.0 + e2)
    w2 = p_sel * e2 / (1.0 + e2)
    rec = jnp.where(lane == ROUTE_E1, i1.astype(F32), 0.0)
    rec = jnp.where(lane == ROUTE_E2, i2.astype(F32), rec)
    rec = jnp.where(lane == ROUTE_W1, w1, rec)
    return jnp.where(lane == ROUTE_W2, w2, rec)


def _merge_kernel(x_ref, ha_ref, om_ref, hs_ref, gm_ref, gs_ref, wbm_ref, wbs_ref, wout_ref, g_ref, b_ref,
                  wr_ref, br_ref, *refs, alpha, n_experts, n_groups, own_blocks):
    o_ref, route_ref = refs[-2:]
    i = pl.program_id(0)

    @pl.when(i < own_blocks)
    def _():
        h_m = (jax.nn.sigmoid(om_ref[...]) * ha_ref[...]).astype(BF16)
        t_m = jnp.dot(h_m, wbm_ref[...], preferred_element_type=F32)
        t_s = jnp.dot(hs_ref[...], wbs_ref[...], preferred_element_type=F32)
        merged = jax.nn.sigmoid(gm_ref[...]) * t_m + jax.nn.sigmoid(gs_ref[...]) * t_s
        y = jnp.dot(merged.astype(BF16), wout_ref[...], preferred_element_type=F32)
        x1 = _layer_norm(alpha * x_ref[...] + y, g_ref[...], b_ref[...])
        o_ref[...] = x1
        x_hi, x_lo = _bf16_pieces(x1, 2)
        hi_both = jnp.dot(x_hi, wr_ref[...], preferred_element_type=F32)
        lo_hi = jnp.dot(x_lo, wr_ref[:, :LANES], preferred_element_type=F32)
        logits = (hi_both[:, :LANES] + (hi_both[:, LANES:] + lo_hi)) + br_ref[...]
        route_ref[...] = _route(logits, n_experts=n_experts, n_groups=n_groups)

    @pl.when(i >= own_blocks)
    def _():
        o_ref[...] = jnp.zeros_like(o_ref)
        route_ref[...] = jnp.zeros_like(route_ref)


def _merge(x, h_a, proj, h_s, w_bm, w_bs, w_out, ln_g, ln_b, w_router, b_router, *,
           o_col, gm_col, gs_col, alpha, n_experts, n_groups, total_rows, row0, prev):
    n, dm = x.shape
    da, ds = h_a.shape[1], h_s.shape[1]
    tm = _tile(math.gcd(n, total_rows - n) if total_rows > n else n, 256)
    assert row0 % tm == 0 and total_rows % tm == 0 and (prev is not None or row0 == 0)
    blk0 = row0 // tm
    own = n // tm
    steps = own if prev is not None else total_rows // tm
    const = lambda shape: pl.BlockSpec(shape, lambda i: (0, 0), pipeline_mode=pl.Buffered(1))
    rows = lambda i: jnp.minimum(i, own - 1)
    prev = () if prev is None else tuple(prev)
    n_in = 13
    return pl.pallas_call(
        functools.partial(_merge_kernel, alpha=alpha, n_experts=n_experts, n_groups=n_groups, own_blocks=own),
        grid=(steps,),
        in_specs=[
            pl.BlockSpec((tm, dm), lambda i: (rows(i), 0)),
            pl.BlockSpec((tm, da), lambda i: (rows(i), 0)),
            pl.BlockSpec((tm, da), lambda i: (rows(i), o_col // da)),
            pl.BlockSpec((tm, ds), lambda i: (rows(i), 0)),
            pl.BlockSpec((tm, dm), lambda i: (rows(i), gm_col // dm)),
            pl.BlockSpec((tm, dm), lambda i: (rows(i), gs_col // dm)),
            const((da, dm)), const((ds, dm)), const((dm, dm)), const((1, dm)), const((1, dm)),
            const((dm, 2 * LANES)), const((1, LANES)),
        ] + [pl.BlockSpec(memory_space=pl.ANY)] * len(prev),
        out_specs=[pl.BlockSpec((tm, dm), lambda i: (blk0 + i, 0)),
                   pl.BlockSpec((tm, LANES), lambda i: (blk0 + i, 0))],
        out_shape=[jax.ShapeDtypeStruct((total_rows, dm), F32), jax.ShapeDtypeStruct((total_rows, LANES), F32)],
        input_output_aliases={n_in + k: k for k in range(len(prev))},
        compiler_params=pltpu.CompilerParams(dimension_semantics=("parallel",), vmem_limit_bytes=48 * MIB),
        name="merge_out_ln1",
    )(x, h_a, proj, h_s, proj, proj, w_bm, w_bs, w_out, ln_g, ln_b, w_router, b_router, *prev)


def _expert_kernel(te_ref, tb_ref, tv_ref, na_ref, perm_ref,
                   x_hbm, wg_ref, wu_ref, wd_ref, y_hbm,
                   xbuf, obuf, wgb, wub, wdb, gsem, ssem, *, n_tok):
    j = pl.program_id(0)
    n_act = na_ref[0]
    slot = lax.rem(j, 2)

    def gather_copy(s, i, token):
        return pltpu.make_async_copy(x_hbm.at[pl.ds(token, 1), :], xbuf.at[s, pl.ds(i, 1), :], gsem.at[s])

    def scatter_copy(s, i, row):
        return pltpu.make_async_copy(obuf.at[s, pl.ds(i, 1), :], y_hbm.at[pl.ds(row, 1), :], ssem.at[s])

    def for_rows(count, fn):
        groups = lax.shift_right_logical(count, 3)

        def group_body(g, c):
            for u in range(SUBLANES):
                fn(g * SUBLANES + u)
            return c

        def row_body(i, c):
            fn(i)
            return c

        lax.fori_loop(0, groups, group_body, 0)
        lax.fori_loop(groups * SUBLANES, count, row_body, 0)

    def start_gather(t, s):
        base = tb_ref[t]

        def start(i):
            pair = perm_ref[base + i]
            gather_copy(s, i, jnp.where(pair >= n_tok, pair - n_tok, pair)).start()

        for_rows(tv_ref[t], start)

    def start_scatter(t, s):
        base = tb_ref[t]
        for_rows(tv_ref[t], lambda i: scatter_copy(s, i, perm_ref[base + i]).start())

    def wait_rows(count, block_copy, row_copy):
        bulk = pl.multiple_of(lax.shift_right_logical(count, 3) * SUBLANES, SUBLANES)

        @pl.when(bulk > 0)
        def _():
            block_copy(pl.ds(0, bulk)).wait()

        def row_body(i, c):
            row_copy(i).wait()
            return c

        lax.fori_loop(bulk, count, row_body, 0)

    def wait_gather(t, s):
        wait_rows(tv_ref[t],
                  lambda rows: pltpu.make_async_copy(x_hbm.at[rows, :], xbuf.at[s, rows, :], gsem.at[s]),
                  lambda i: gather_copy(s, i, 0))

    def wait_scatter(t, s):
        wait_rows(tv_ref[t],
                  lambda rows: pltpu.make_async_copy(obuf.at[s, rows, :], y_hbm.at[rows, :], ssem.at[s]),
                  lambda i: scatter_copy(s, i, 0))

    @pl.when(j == 0)
    def _():
        xbuf[...] = jnp.zeros_like(xbuf)
        start_gather(0, 0)

    @pl.when(j + 1 < n_act)
    def _():
        start_gather(j + 1, 1 - slot)

    @pl.when(j < n_act)
    def _():
        wait_gather(j, slot)

        @pl.when(j >= 2)
        def _():
            wait_scatter(j - 2, slot)

        @pl.when(jnp.logical_or(j == 0, te_ref[j] != te_ref[jnp.maximum(j - 1, 0)]))
        def _():
            wgb[...] = wg_ref[0].astype(BF16)
            wub[...] = wu_ref[0].astype(BF16)
            wdb[...] = wd_ref[0].astype(BF16)

        x = xbuf[slot].astype(BF16)
        gate = jnp.dot(x, wgb[...], preferred_element_type=F32)
        up = jnp.dot(x, wub[...], preferred_element_type=F32)
        hid = (gate * jax.nn.sigmoid(gate) * up).astype(BF16)
        obuf[slot] = jnp.dot(hid, wdb[...], preferred_element_type=F32)
        start_scatter(j, slot)

        @pl.when(j == n_act - 1)
        def _():
            @pl.when(j >= 1)
            def _():
                wait_scatter(j - 1, 1 - slot)

            wait_scatter(j, slot)


def _expert_tables(route, n_experts, tile):
    n_tok = route.shape[0]
    i32 = jnp.int32
    e_flat = jnp.concatenate([route[:, ROUTE_E1], route[:, ROUTE_E2]]).astype(i32)
    n_pad = (1 << (2 * n_tok - 1).bit_length()) - 2 * n_tok
    keys = jnp.concatenate([e_flat, jnp.full((n_pad,), n_experts, i32)])
    perm = jnp.argsort(keys, stable=True)[:2 * n_tok].astype(i32)
    eids = jnp.arange(n_experts, dtype=i32)
    counts = jnp.sum((e_flat[:, None] == eids[None, :]).astype(i32), axis=0)
    tiles_e = (counts + (tile - 1)) // tile
    before = eids[None, :] < eids[:, None]
    pair_start = jnp.sum(jnp.where(before, counts[None, :], 0), axis=1)
    tile_start = jnp.sum(jnp.where(before, tiles_e[None, :], 0), axis=1)
    n_act = jnp.sum(tiles_e)
    t_max = (2 * n_tok) // tile + n_experts
    j = jnp.arange(t_max, dtype=i32)
    owner = jnp.logical_and(j[:, None] >= tile_start[None, :], j[:, None] < (tile_start + tiles_e)[None, :])
    pick = lambda table: jnp.sum(jnp.where(owner, table[None, :], 0), axis=1)
    active = j < n_act
    k = j - pick(tile_start)
    last_e = jnp.max(jnp.where(tiles_e > 0, eids, 0))
    te = jnp.where(active, pick(eids), last_e).astype(i32)
    tb = jnp.where(active, pick(pair_start) + k * tile, 0).astype(i32)
    tv = jnp.where(active, jnp.clip(pick(counts) - k * tile, 0, tile), 0).astype(i32)
    return te, tb, tv, n_act.reshape(1).astype(i32), perm, t_max


def _experts(x1, route, w_gate, w_up, w_down):
    n_tok, dm = x1.shape
    n_experts, _, de = w_gate.shape
    tile = 256
    te, tb, tv, n_act, perm, t_max = _expert_tables(route, n_experts, tile)
    w_in_spec = pl.BlockSpec((1, dm, de), lambda j, te, tb, tv, na, perm: (te[j], 0, 0))
    w_out_spec = pl.BlockSpec((1, de, dm), lambda j, te, tb, tv, na, perm: (te[j], 0, 0))
    grid_spec = pltpu.PrefetchScalarGridSpec(
        num_scalar_prefetch=5,
        grid=(t_max,),
        in_specs=[pl.BlockSpec(memory_space=pl.ANY), w_in_spec, w_in_spec, w_out_spec],
        out_specs=pl.BlockSpec(memory_space=pl.ANY),
        scratch_shapes=[
            pltpu.VMEM((2, tile, dm), F32), pltpu.VMEM((2, tile, dm), F32),
            pltpu.VMEM((dm, de), BF16), pltpu.VMEM((dm, de), BF16), pltpu.VMEM((de, dm), BF16),
            pltpu.SemaphoreType.DMA((2,)), pltpu.SemaphoreType.DMA((2,)),
        ],
    )
    return pl.pallas_call(
        functools.partial(_expert_kernel, n_tok=n_tok),
        grid_spec=grid_spec,
        out_shape=jax.ShapeDtypeStruct((2 * n_tok, dm), F32),
        compiler_params=pltpu.CompilerParams(dimension_semantics=("arbitrary",), vmem_limit_bytes=48 * MIB),
        name="moe_experts",
    )(te, tb, tv, n_act, perm, x1, w_gate, w_up, w_down)


def _combine_kernel(x_ref, ya_ref, yb_ref, r_ref, g_ref, b_ref, first_ref, second_ref, *, alpha, first_blocks):
    r = r_ref[...]
    moe = r[:, ROUTE_W1:ROUTE_W1 + 1] * ya_ref[...] + r[:, ROUTE_W2:ROUTE_W2 + 1] * yb_ref[...]
    out = _layer_norm(alpha * x_ref[...] + moe, g_ref[...], b_ref[...])
    i = pl.program_id(0)

    @pl.when(i < first_blocks)
    def _():
        first_ref[...] = out

    @pl.when(i >= first_blocks)
    def _():
        second_ref[...] = out


def _combine(x1, y_pairs, route, ln_g, ln_b, *, alpha, n_first):
    n, dm = x1.shape
    tm = _tile(math.gcd(n_first, n - n_first), 256)
    nb, nb_first = n // tm, n_first // tm
    const = lambda shape: pl.BlockSpec(shape, lambda i: (0, 0))
    return pl.pallas_call(
        functools.partial(_combine_kernel, alpha=alpha, first_blocks=nb_first),
        grid=(nb,),
        in_specs=[
            pl.BlockSpec((tm, dm), lambda i: (i, 0)),
            pl.BlockSpec((tm, dm), lambda i: (i, 0)),
            pl.BlockSpec((tm, dm), lambda i: (nb + i, 0)),
            pl.BlockSpec((tm, LANES), lambda i: (i, 0)),
            const((1, dm)), const((1, dm)),
        ],
        out_specs=[pl.BlockSpec((tm, dm), lambda i: (jnp.minimum(i, nb_first - 1), 0)),
                   pl.BlockSpec((tm, dm), lambda i: (jnp.maximum(i - nb_first, 0), 0))],
        out_shape=[jax.ShapeDtypeStruct((n_first, dm), F32), jax.ShapeDtypeStruct((n - n_first, dm), F32)],
        compiler_params=pltpu.CompilerParams(dimension_semantics=("arbitrary",), vmem_limit_bytes=48 * MIB),
        name="moe_combine_ln2",
    )(x1, y_pairs, y_pairs, route, ln_g, ln_b)


def _mixer(x, conv_state, c0, n0, m0, past_k, past_v, p, *, chunk_pref, total_rows, row0, prev):
    batch, seq, dm = x.shape
    n_tok = batch * seq
    xf = x.reshape(n_tok, dm)
    n_heads, d = p["n_heads"], p["head_dim"]
    da = n_heads * d
    cols = p["cols"]

    proj, q_s, k_hm, k_hm16, v_hm, v_hm16, gates = _project(
        xf, p["w_all"], p["w_gate"], main_cols=p["main_cols"], sb_cols=p["sb_cols"], tn=p["tn"])

    chunk = _tile(seq, chunk_pref)
    h_a, c_new, n_new, m_new = _mlstm(
        proj, gates, p["b_gate_row"], p["conv_w"], p["conv_b"], conv_state, c0, n0, m0,
        batch=batch, seq=seq, chunk=chunk, n_heads=n_heads, head_dim=d,
        q_col=cols["q"], k_col=cols["k"], v_col=cols["v"])

    if past_k is None:
        assert batch == 1
        h_s = _sb_prompt(q_s, k_hm16, v_hm16, seq=seq)
    else:
        h_s = _sb_decode(q_s, k_hm16, v_hm16, past_k, past_v, batch=batch, seq=seq)

    x1, route = _merge(xf, h_a, proj, h_s, p["w_bm"], p["w_bs"], p["w_out"], p["ln1_g"], p["ln1_b"],
                       p["w_router"], p["b_router"],
                       o_col=cols["o"], gm_col=cols["gm"], gs_col=cols["gs"],
                       alpha=p["alpha"], n_experts=p["n_experts"], n_groups=p["n_groups"],
                       total_rows=total_rows, row0=row0, prev=prev)

    n_sb = k_hm.shape[0]
    sb_k = k_hm.reshape(n_sb, batch, seq, -1).transpose(1, 0, 2, 3)
    sb_v = v_hm.reshape(n_sb, batch, seq, -1).transpose(1, 0, 2, 3)
    new_conv = proj.reshape(batch, seq, -1)[:, seq - (CONV_W - 1):, cols["q"]:cols["q"] + 2 * da]
    return x1, route, (sb_k, sb_v, c_new, n_new, m_new, new_conv)


def _prepare_layer(l, depth, w_in, b_gate, conv_w, conv_b, w_branch_m, w_branch_s, w_out, ln1_g, ln1_b,
                   w_router_group, b_router_group, w_router_expert, b_router_expert,
                   w_exp_gate, w_exp_up, w_exp_down, ln2_g, ln2_b, head_dim):
    dm = w_in.shape[1]
    da2 = conv_w.shape[-1]
    da = da2 // 2
    n_heads = b_gate.shape[-1] // 2
    ds = w_branch_s.shape[1]
    sizes = (da2, da, da, 2 * n_heads, ds, ds, ds, dm, dm)
    offs = [0]
    for s in sizes:
        offs.append(offs[-1] + s)
    w = w_in[l]
    part = lambda i: w[:, offs[i]:offs[i + 1]]
    main_cols = da2 + 2 * da + 2 * dm
    tn = math.gcd(math.gcd(1024, main_cols), ds)
    w_all = jnp.concatenate([part(i).astype(BF16) for i in (0, 1, 2, 7, 8, 4, 5, 6)], axis=1)
    w_gate = jnp.pad(part(3).astype(BF16), ((0, 0), (0, LANES - 2 * n_heads)))
    cols = {"q": 0, "k": da, "v": da2, "o": da2 + da, "gm": da2 + 2 * da, "gs": da2 + 2 * da + dm}
    assert cols["k"] % da == 0 and cols["o"] % da == 0 and cols["gm"] % dm == 0 and cols["gs"] % dm == 0
    b_gate_row = jnp.pad(b_gate[l].astype(F32), (0, LANES - 2 * n_heads)).reshape(1, LANES)
    n_groups = w_router_group.shape[-1]
    n_experts = w_router_expert.shape[-1]
    pad_r = LANES - n_experts - n_groups
    w_router = jnp.pad(jnp.concatenate([w_router_expert[l], w_router_group[l]], axis=1),
                       ((0, 0), (0, pad_r))).astype(F32)
    w_router_hi = w_router.astype(BF16)
    w_router_lo = (w_router - w_router_hi.astype(F32)).astype(BF16)
    b_router = jnp.pad(jnp.concatenate([b_router_expert[l], b_router_group[l]]), (0, pad_r)).reshape(1, LANES)
    return {
        "n_heads": n_heads, "head_dim": head_dim, "cols": cols, "n_groups": n_groups, "n_experts": n_experts,
        "alpha": (2.0 * depth) ** 0.25, "main_cols": main_cols, "sb_cols": ds, "tn": tn,
        "w_all": w_all, "w_gate": w_gate, "b_gate_row": b_gate_row,
        "conv_w": conv_w[l], "conv_b": conv_b[l],
        "w_bm": w_branch_m[l].astype(BF16), "w_bs": w_branch_s[l].astype(BF16), "w_out": w_out[l].astype(BF16),
        "ln1_g": ln1_g[l].reshape(1, dm), "ln1_b": ln1_b[l].reshape(1, dm),
        "w_router": jnp.concatenate([w_router_hi, w_router_lo], axis=1), "b_router": b_router.astype(F32),
        "w_eg": w_exp_gate[l], "w_eu": w_exp_up[l], "w_ed": w_exp_down[l],
        "ln2_g": ln2_g[l].reshape(1, dm), "ln2_b": ln2_b[l].reshape(1, dm),
    }


def kernel(x_prompt, x_sample, cache_sb_k, cache_sb_v, state_mlstm_C, state_mlstm_n, state_mlstm_m, state_conv,
           w_in, b_gate, conv_w, conv_b, w_branch_m, w_branch_s, w_out, ln1_g, ln1_b,
           w_router_group, b_router_group, w_router_expert, b_router_expert,
           w_exp_gate, w_exp_up, w_exp_down, ln2_g, ln2_b):
    depth = w_in.shape[0]
    bp = x_prompt.shape[0]
    n_heads, head_dim = state_mlstm_C.shape[2], state_mlstm_C.shape[3]
    da2 = conv_w.shape[-1]
    xp, xs = x_prompt, x_sample
    st_p_all, st_s_all = [], []
    for l in range(depth):
        p = _prepare_layer(l, depth, w_in, b_gate, conv_w, conv_b, w_branch_m, w_branch_s, w_out, ln1_g, ln1_b,
                           w_router_group, b_router_group, w_router_expert, b_router_expert,
                           w_exp_gate, w_exp_up, w_exp_down, ln2_g, ln2_b, head_dim)
        n_p, n_s = xp.shape[0] * xp.shape[1], xs.shape[0] * xs.shape[1]
        x1, route, st_p = _mixer(
            xp, jnp.zeros((bp, CONV_W - 1, da2), F32), jnp.zeros((bp, n_heads, head_dim, head_dim), F32),
            jnp.zeros((bp, n_heads, head_dim), F32), jnp.full((bp, n_heads), NEG, F32), None, None, p,
            chunk_pref=256, total_rows=n_p + n_s, row0=0, prev=None)
        x1, route, st_s = _mixer(xs, state_conv[l], state_mlstm_C[l], state_mlstm_n[l], state_mlstm_m[l],
                                 cache_sb_k[l], cache_sb_v[l], p, chunk_pref=256,
                                 total_rows=n_p + n_s, row0=n_p, prev=(x1, route))
        y_pairs = _experts(x1, route, p["w_eg"], p["w_eu"], p["w_ed"])
        xp2, xs2 = _combine(x1, y_pairs, route, p["ln2_g"], p["ln2_b"], alpha=p["alpha"], n_first=n_p)
        xp, xs = xp2.reshape(xp.shape), xs2.reshape(xs.shape)
        st_p_all.append(st_p)
        st_s_all.append(st_s)
    stack = lambda states, i: jnp.stack([s[i] for s in states], axis=0)
    return (xp, xs,
            *(stack(st_p_all, i) for i in range(6)),
            *(stack(st_s_all, i) for i in range(6)))
```

```python
import functools
import math

import jax
import jax.numpy as jnp
from jax import lax
from jax.experimental import pallas as pl
from jax.experimental.pallas import tpu as pltpu

F32 = jnp.float32
BF16 = jnp.bfloat16

LANES = 128
SUBLANES = 8
LN_EPS = 1e-5
NEG = -1e30
CONV_W = 4
EXP_ZERO_BELOW = -104.0
MIB = 1024 * 1024

_NT = (((1,), (1,)), ((), ()))
_TN = (((0,), (0,)), ((), ()))


def _tile(n, pref):
    if n <= pref:
        return n
    t = pref
    while t >= SUBLANES:
        if n % t == 0 and t % SUBLANES == 0:
            return t
        t -= SUBLANES
    return n


def _log_sigmoid(x):
    return jnp.minimum(x, 0.0) - jnp.log1p(jnp.exp(-jnp.abs(x)))


def _layer_norm(x, g, b):
    mu = jnp.mean(x, axis=-1, keepdims=True)
    d = x - mu
    var = jnp.mean(d * d, axis=-1, keepdims=True)
    return d * lax.rsqrt(var + LN_EPS) * g + b


def _bf16_pieces(x, n):
    pieces = []
    rem = x
    for _ in range(n - 1):
        p = rem.astype(BF16)
        pieces.append(p)
        rem = rem - p.astype(F32)
    pieces.append(rem.astype(BF16))
    return pieces


def _dot_pieces(pieces, other, *, pieces_first, dims=None):
    total = None
    for p in pieces:
        a, b = (p, other) if pieces_first else (other, p)
        if dims is None:
            t = jnp.dot(a, b, preferred_element_type=F32)
        else:
            t = lax.dot_general(a, b, dims, preferred_element_type=F32)
        total = t if total is None else total + t
    return total


def _proj_kernel(x_ref, w_ref, wg_ref, main_ref, qs_ref, k32_ref, k16_ref, v32_ref, v16_ref, g_ref, xb_ref, *, ends):
    j = pl.program_id(1)
    end_a, end_qs, end_k, end_v = ends

    @pl.when(j == 0)
    def _():
        xb_ref[...] = x_ref[...].astype(BF16)
        g_ref[...] = jnp.dot(xb_ref[...], wg_ref[...], preferred_element_type=F32)

    acc = jnp.dot(xb_ref[...], w_ref[...], preferred_element_type=F32)

    def store_heads(f32_ref, bf16_ref):
        for h in range(f32_ref.shape[0]):
            piece = acc[:, h * LANES:(h + 1) * LANES]
            f32_ref[h] = piece
            bf16_ref[h] = piece.astype(BF16)

    @pl.when(jnp.logical_or(j < end_a, j >= end_v))
    def _():
        main_ref[...] = acc

    @pl.when(jnp.logical_and(j >= end_a, j < end_qs))
    def _():
        qs_ref[...] = acc.astype(BF16)

    @pl.when(jnp.logical_and(j >= end_qs, j < end_k))
    def _():
        store_heads(k32_ref, k16_ref)

    @pl.when(jnp.logical_and(j >= end_k, j < end_v))
    def _():
        store_heads(v32_ref, v16_ref)


def _project(x, w_all, w_gate, *, a_cols, sb_cols, b_cols, tn):
    n, k = x.shape
    tm = _tile(n, 1024)
    t_a, t_sb, t_b = a_cols // tn, sb_cols // tn, b_cols // tn
    ends = (t_a, t_a + t_sb, t_a + 2 * t_sb, t_a + 3 * t_sb)
    n_tiles = ends[-1] + t_b
    main_cols = a_cols + b_cols
    assert w_all.shape[1] == n_tiles * tn
    local = lambda j, start, count: jnp.clip(j - start, 0, count - 1)
    main_tile = lambda j: jnp.where(j < ends[3], jnp.minimum(j, t_a - 1), j - 3 * t_sb)
    heads = tn // LANES
    n_sb = sb_cols // LANES
    once = pl.Buffered(1)
    hm_spec = lambda start: pl.BlockSpec((heads, tm, LANES), lambda i, j: (local(j, start, t_sb), i, 0),
                                         pipeline_mode=once)
    return pl.pallas_call(
        functools.partial(_proj_kernel, ends=ends),
        grid=(n // tm, n_tiles),
        in_specs=[pl.BlockSpec((tm, k), lambda i, j: (i, 0), pipeline_mode=once),
                  pl.BlockSpec((k, tn), lambda i, j: (0, j)),
                  pl.BlockSpec((k, LANES), lambda i, j: (0, 0), pipeline_mode=once)],
        out_specs=[
            pl.BlockSpec((tm, tn), lambda i, j: (i, main_tile(j))),
            pl.BlockSpec((tm, tn), lambda i, j: (i, local(j, ends[0], t_sb)), pipeline_mode=once),
            hm_spec(ends[1]), hm_spec(ends[1]), hm_spec(ends[2]), hm_spec(ends[2]),
            pl.BlockSpec((tm, LANES), lambda i, j: (i, 0), pipeline_mode=once),
        ],
        out_shape=[
            jax.ShapeDtypeStruct((n, main_cols), F32),
            jax.ShapeDtypeStruct((n, sb_cols), BF16),
            jax.ShapeDtypeStruct((n_sb, n, LANES), F32), jax.ShapeDtypeStruct((n_sb, n, LANES), BF16),
            jax.ShapeDtypeStruct((n_sb, n, LANES), F32), jax.ShapeDtypeStruct((n_sb, n, LANES), BF16),
            jax.ShapeDtypeStruct((n, LANES), F32),
        ],
        scratch_shapes=[pltpu.VMEM((tm, k), BF16)],
        compiler_params=pltpu.CompilerParams(
            dimension_semantics=("parallel", "arbitrary"),
            vmem_limit_bytes=56 * MIB),
        name="in_proj",
    )(x, w_all, w_gate)


def _mlstm_kernel(q_ref, k_ref, v_ref, g_ref, bg_ref, cwq_ref, cwk_ref, cbq_ref, cbk_ref,
                  csq_ref, csk_ref, c0_ref, n0_ref, m0_ref,
                  h_ref, c_ref, n_ref, m_ref, qext_ref, kext_ref, *, chunk, n_heads, head_dim):
    step = pl.program_id(1)
    L, d = chunk, head_dim
    halo = CONV_W - 1
    base = SUBLANES - halo

    @pl.when(step == 0)
    def _():
        qext_ref[base:SUBLANES, :] = csq_ref[0]
        kext_ref[base:SUBLANES, :] = csk_ref[0]
        c_ref[...] = c0_ref[...]
        n_ref[...] = n0_ref[...]
        m_ref[...] = m0_ref[...]

    def conv_silu(raw_ref, ext_ref, cw_ref, cb_ref):
        ext_ref[SUBLANES:SUBLANES + L, :] = raw_ref[...]
        acc = cb_ref[...] + ext_ref[base:base + L, :] * cw_ref[0:1, :]
        for j in range(1, CONV_W):
            acc = acc + ext_ref[base + j:base + j + L, :] * cw_ref[j:j + 1, :]
        ext_ref[base:SUBLANES, :] = ext_ref[base + L:SUBLANES + L, :]
        return acc * jax.nn.sigmoid(acc)

    q_all = conv_silu(q_ref, qext_ref, cwq_ref, cbq_ref)
    k_all = conv_silu(k_ref, kext_ref, cwk_ref, cbk_ref) * (float(d) ** -0.5)

    lane = lax.broadcasted_iota(jnp.int32, (L, LANES), 1)
    pre = g_ref[...] + bg_ref[...]
    gates = jnp.where(lane < n_heads, pre, _log_sigmoid(pre))
    row_i = lax.broadcasted_iota(jnp.int32, (L, L), 0)
    col_i = lax.broadcasted_iota(jnp.int32, (L, L), 1)
    causal = col_i <= row_i
    tril = causal.astype(BF16)
    triu = (row_i <= col_i).astype(BF16)
    sel = (lax.broadcasted_iota(jnp.int32, (SUBLANES, LANES), 0)
           == lax.broadcasted_iota(jnp.int32, (SUBLANES, LANES), 1)).astype(BF16)
    gate_pieces = _bf16_pieces(gates, 3)
    cum_cols = _dot_pieces(gate_pieces, tril, pieces_first=False)
    gates_t = _dot_pieces(gate_pieces, sel, pieces_first=False, dims=_NT)
    cum_rows = _dot_pieces(_bf16_pieces(gates_t, 3), triu, pieces_first=True)

    for h in range(n_heads):
        cols = slice(h * d, (h + 1) * d)
        q = q_all[:, cols]
        k = k_all[:, cols]
        q_b = q.astype(BF16)
        k_b = k.astype(BF16)
        v_b = v_ref[:, cols].astype(BF16)
        i_col = gates[:, h:h + 1]
        b_col = cum_cols[:, n_heads + h:n_heads + h + 1]
        i_row = gates_t[h:h + 1, :]
        b_row = cum_rows[n_heads + h:n_heads + h + 1, :]
        c_prev = c_ref[0, h]
        n_prev = n_ref[0, h]
        m_prev = m_ref[0, h][:, 0:1]

        dmat = jnp.where(causal, b_col - b_row + i_row, NEG)
        inter = b_col + m_prev
        m_t = jnp.maximum(jnp.max(dmat, axis=-1, keepdims=True), inter)
        w = jnp.exp(dmat - m_t)
        s = lax.dot_general(q_b, k_b, _NT, preferred_element_type=F32) * w
        e_inter = jnp.exp(inter - m_t)
        num = (jnp.dot(s.astype(BF16), v_b, preferred_element_type=F32)
               + e_inter * jnp.dot(q_b, c_prev.astype(BF16), preferred_element_type=F32))
        den = jnp.sum(s, axis=-1, keepdims=True) + e_inter * jnp.sum(q * n_prev, axis=-1, keepdims=True)
        h_ref[:, cols] = num / jnp.maximum(jnp.abs(den), jnp.exp(-m_t))

        g_last = b_col[L - 1:L, :]
        w_last = g_last - b_col + i_col
        m_new = jnp.maximum(g_last + m_prev, jnp.max(w_last, axis=0, keepdims=True))
        decay = jnp.exp(g_last + m_prev - m_new)
        kw = jnp.exp(w_last - m_new) * k
        c_ref[0, h] = decay * c_prev + lax.dot_general(kw.astype(BF16), v_b, _TN, preferred_element_type=F32)
        n_ref[0, h] = decay * n_prev + jnp.sum(kw, axis=0, keepdims=True)
        m_ref[0, h] = jnp.broadcast_to(m_new, (1, LANES))


def _mlstm(proj, gates, b_gate_row, conv_w, conv_b, conv_state, c0, n0, m0, *, batch, seq, chunk,
           n_heads, head_dim, q_col, k_col, v_col):
    n_tok = batch * seq
    nc = seq // chunk
    d = head_dim
    da = n_heads * d
    qb, kb, vb = q_col // da, k_col // da, v_col // da
    kcw = (k_col - q_col) // da
    row = lambda b, c: b * nc + c
    n0r = n0.reshape(batch, n_heads, 1, d)
    m0r = jnp.broadcast_to(m0.reshape(batch, n_heads, 1, 1), (batch, n_heads, 1, LANES))
    cbr = conv_b.reshape(1, -1)
    state_spec = lambda shape: pl.BlockSpec(shape, lambda b, c: (b, 0, 0, 0))
    h, c_new, n_new, m_new = pl.pallas_call(
        functools.partial(_mlstm_kernel, chunk=chunk, n_heads=n_heads, head_dim=d),
        grid=(batch, nc),
        in_specs=[
            pl.BlockSpec((chunk, da), lambda b, c: (row(b, c), qb)),
            pl.BlockSpec((chunk, da), lambda b, c: (row(b, c), kb)),
            pl.BlockSpec((chunk, da), lambda b, c: (row(b, c), vb)),
            pl.BlockSpec((chunk, LANES), lambda b, c: (row(b, c), 0)),
            pl.BlockSpec((1, LANES), lambda b, c: (0, 0)),
            pl.BlockSpec((CONV_W, da), lambda b, c: (0, 0)),
            pl.BlockSpec((CONV_W, da), lambda b, c: (0, kcw)),
            pl.BlockSpec((1, da), lambda b, c: (0, 0)),
            pl.BlockSpec((1, da), lambda b, c: (0, kcw)),
            pl.BlockSpec((1, CONV_W - 1, da), lambda b, c: (b, 0, 0)),
            pl.BlockSpec((1, CONV_W - 1, da), lambda b, c: (b, 0, kcw)),
            state_spec((1, n_heads, d, d)),
            state_spec((1, n_heads, 1, d)),
            state_spec((1, n_heads, 1, LANES)),
        ],
        out_specs=[
            pl.BlockSpec((chunk, da), lambda b, c: (row(b, c), 0)),
            state_spec((1, n_heads, d, d)),
            state_spec((1, n_heads, 1, d)),
            state_spec((1, n_heads, 1, LANES)),
        ],
        out_shape=[
            jax.ShapeDtypeStruct((n_tok, da), F32),
            jax.ShapeDtypeStruct((batch, n_heads, d, d), F32),
            jax.ShapeDtypeStruct((batch, n_heads, 1, d), F32),
            jax.ShapeDtypeStruct((batch, n_heads, 1, LANES), F32),
        ],
        scratch_shapes=[pltpu.VMEM((chunk + SUBLANES, da), F32), pltpu.VMEM((chunk + SUBLANES, da), F32)],
        compiler_params=pltpu.CompilerParams(
            dimension_semantics=("parallel", "arbitrary"),
            vmem_limit_bytes=32 * MIB),
        name="mlstm",
    )(proj, proj, proj, gates, b_gate_row, conv_w, conv_w, cbr, cbr, conv_state, conv_state, c0, n0r, m0r)
    return h, c_new, n_new.reshape(batch, n_heads, d), m_new[:, :, 0, 0]


def _sb_scores(q_b, k_b, *, scale, diagonal):
    tq, tk = q_b.shape[0], k_b.shape[0]
    z = lax.dot_general(q_b, k_b, _NT, preferred_element_type=F32) * scale
    softplus = jnp.maximum(z, 0.0) + jnp.log(1.0 + jnp.exp(-jnp.abs(z)))
    log_beta = z - softplus
    drop, mask = softplus, None
    if diagonal:
        mask = (lax.broadcasted_iota(jnp.int32, (tq, tk), 1) < lax.broadcasted_iota(jnp.int32, (tq, tk), 0))
        drop = jnp.where(mask, softplus, 0.0)
    later_mat = (lax.broadcasted_iota(jnp.int32, (tk, tk), 0)
                 > lax.broadcasted_iota(jnp.int32, (tk, tk), 1)).astype(BF16)
    later = _dot_pieces(_bf16_pieces(drop, 2), later_mat, pieces_first=True)
    total = later[:, 0:1] + drop[:, 0:1]
    return log_beta, later, total, mask


def _sb_weights(scores, v_b, carry, acc):
    log_beta, later, total, mask = scores
    a = jnp.exp(log_beta - (later + carry))
    if mask is not None:
        a = jnp.where(mask, a, 0.0)
    return carry + total, acc + jnp.dot(a.astype(BF16), v_b, preferred_element_type=F32)


def _sb_block(q_b, k_b, v_b, carry, acc, *, scale, diagonal):
    return _sb_weights(_sb_scores(q_b, k_b, scale=scale, diagonal=diagonal), v_b, carry, acc)


def _sb_sweep(qs, k_ats, v_ats, first_block, carries, accs, *, tk, scale):
    def carry_min(cs):
        m = cs[0]
        for c in cs[1:]:
            m = jnp.minimum(m, c)
        return jnp.min(m)

    def cond(state):
        j, cmin, _, _ = state
        return jnp.logical_and(j >= 0, cmin < -EXP_ZERO_BELOW)

    def body(state):
        j, _, cs, acs = state
        start = pl.multiple_of(j * tk, tk)
        new = [_sb_block(q, k_at[pl.ds(start, tk), :].astype(BF16), v_at[pl.ds(start, tk), :].astype(BF16),
                         c, a, scale=scale, diagonal=False)
               for q, k_at, v_at, c, a in zip(qs, k_ats, v_ats, cs, acs)]
        cs = tuple(n[0] for n in new)
        return j - 1, carry_min(cs), cs, tuple(n[1] for n in new)

    carries = tuple(carries)
    _, _, _, accs = lax.while_loop(cond, body, (first_block, carry_min(carries), carries, tuple(accs)))
    return accs


def _sb_prompt_kernel(q_ref, k_ref, v_ref, o_ref, *, tq, scale):
    i = pl.program_id(1)
    n_heads, _, d = k_ref.shape
    start = pl.multiple_of(i * tq, tq)
    qs = [q_ref[:, h * d:(h + 1) * d] for h in range(n_heads)]
    zero_carry, zero_acc = jnp.zeros((tq, 1), F32), jnp.zeros((tq, d), F32)

    def finish(states, first_block):
        accs = _sb_sweep(qs, [k_ref.at[h] for h in range(n_heads)], [v_ref.at[h] for h in range(n_heads)],
                         first_block, [s[0] for s in states], [s[1] for s in states], tk=tq, scale=scale)
        for h in range(n_heads):
            o_ref[:, h * d:(h + 1) * d] = accs[h].astype(o_ref.dtype)

    @pl.when(i == 0)
    def _():
        finish([_sb_block(qs[h], k_ref[h, pl.ds(start, tq), :], v_ref[h, pl.ds(start, tq), :],
                          zero_carry, zero_acc, scale=scale, diagonal=True) for h in range(n_heads)], i - 1)

    @pl.when(i > 0)
    def _():
        prev = pl.multiple_of(start - tq, tq)
        states = []
        for h in range(n_heads):
            diag = _sb_scores(qs[h], k_ref[h, pl.ds(start, tq), :], scale=scale, diagonal=True)
            full = _sb_scores(qs[h], k_ref[h, pl.ds(prev, tq), :], scale=scale, diagonal=False)
            carry, acc = _sb_weights(diag, v_ref[h, pl.ds(start, tq), :], zero_carry, zero_acc)
            states.append(_sb_weights(full, v_ref[h, pl.ds(prev, tq), :], carry, acc))
        finish(states, i - 2)


def _sb_prompt(q, k_hm, v_hm, *, seq, heads_per_step=2):
    n_heads, _, d = k_hm.shape
    hb = heads_per_step if n_heads % heads_per_step == 0 else 1
    tq = _tile(seq, 256)
    kv_spec = pl.BlockSpec((hb, seq, d), lambda g, i: (g, 0, 0))
    return pl.pallas_call(
        functools.partial(_sb_prompt_kernel, tq=tq, scale=float(d) ** -0.5),
        grid=(n_heads // hb, seq // tq),
        in_specs=[pl.BlockSpec((tq, hb * d), lambda g, i: (i, g)), kv_spec, kv_spec],
        out_specs=pl.BlockSpec((tq, hb * d), lambda g, i: (i, g)),
        out_shape=jax.ShapeDtypeStruct((seq, n_heads * d), BF16),
        compiler_params=pltpu.CompilerParams(
            dimension_semantics=("parallel", "arbitrary"),
            vmem_limit_bytes=48 * MIB),
        name="sb_prompt",
    )(q, k_hm, v_hm)


def _sb_decode_kernel(q_ref, kn_ref, vn_ref, kp_ref, vp_ref, o_ref, *, tk, scale):
    n_heads, tq, d = kn_ref.shape
    last = kp_ref.shape[2] // tk - 1
    newest = pl.ds(last * tk, tk)
    zero_carry, zero_acc = jnp.zeros((tq, 1), F32), jnp.zeros((tq, d), F32)
    qs, carries, accs = [], [], []
    for h in range(n_heads):
        q_b = q_ref[:, h * d:(h + 1) * d]
        diag = _sb_scores(q_b, kn_ref[h], scale=scale, diagonal=True)
        full = _sb_scores(q_b, kp_ref[0, h, newest, :].astype(BF16), scale=scale, diagonal=False)
        carry, acc = _sb_weights(diag, vn_ref[h], zero_carry, zero_acc)
        carry, acc = _sb_weights(full, vp_ref[0, h, newest, :].astype(BF16), carry, acc)
        qs.append(q_b)
        carries.append(carry)
        accs.append(acc)
    accs = _sb_sweep(qs, [kp_ref.at[0, h] for h in range(n_heads)], [vp_ref.at[0, h] for h in range(n_heads)],
                     last - 1, carries, accs, tk=tk, scale=scale)
    for h in range(n_heads):
        o_ref[:, h * d:(h + 1) * d] = accs[h].astype(o_ref.dtype)


def _sb_decode(q, k_hm, v_hm, past_k, past_v, *, batch, seq, heads_per_step=4):
    n_heads, _, d = k_hm.shape
    hb = heads_per_step if n_heads % heads_per_step == 0 else 1
    past_len = past_k.shape[2]
    tk = _tile(past_len, 256)
    new_spec = pl.BlockSpec((hb, seq, d), lambda b, g: (g, b, 0))
    past_spec = pl.BlockSpec((1, hb, past_len, d), lambda b, g: (b, g, 0, 0))
    return pl.pallas_call(
        functools.partial(_sb_decode_kernel, tk=tk, scale=float(d) ** -0.5),
        grid=(batch, n_heads // hb),
        in_specs=[pl.BlockSpec((seq, hb * d), lambda b, g: (b, g)), new_spec, new_spec, past_spec, past_spec],
        out_specs=pl.BlockSpec((seq, hb * d), lambda b, g: (b, g)),
        out_shape=jax.ShapeDtypeStruct((batch * seq, n_heads * d), BF16),
        compiler_params=pltpu.CompilerParams(dimension_semantics=("parallel", "parallel")),
        name="sb_decode",
    )(q, k_hm, v_hm, past_k, past_v)


ROUTE_E1, ROUTE_E2, ROUTE_W1, ROUTE_W2 = 0, 1, 2, 3


def _route(logits, *, n_experts, n_groups):
    per_group = n_experts // n_groups
    lane = lax.broadcasted_iota(jnp.int32, logits.shape, 1)
    big = jnp.int32(LANES)
    is_group = jnp.logical_and(lane >= n_experts, lane < n_experts + n_groups)
    lg = jnp.where(is_group, logits, -jnp.inf)
    lg_max = jnp.max(lg, axis=-1, keepdims=True)
    p_sel = 1.0 / jnp.sum(jnp.exp(lg - lg_max), axis=-1, keepdims=True)
    grp = jnp.min(jnp.where(lg == lg_max, lane - n_experts, big), axis=-1, keepdims=True)
    in_grp = jnp.logical_and(lane >= grp * per_group, lane < (grp + 1) * per_group)
    le = jnp.where(in_grp, logits, -jnp.inf)
    v1 = jnp.max(le, axis=-1, keepdims=True)
    i1 = jnp.min(jnp.where(le == v1, lane, big), axis=-1, keepdims=True)
    le2 = jnp.where(lane == i1, -jnp.inf, le)
    v2 = jnp.max(le2, axis=-1, keepdims=True)
    i2 = jnp.min(jnp.where(le2 == v2, lane, big), axis=-1, keepdims=True)
    e2 = jnp.exp(v2 - v1)
    w1 = p_sel / (1.0 + e2)
    w2 = p_sel * e2 / (1.0 + e2)
    rec = jnp.where(lane == ROUTE_E1, i1.astype(F32), 0.0)
    rec = jnp.where(lane == ROUTE_E2, i2.astype(F32), rec)
    rec = jnp.where(lane == ROUTE_W1, w1, rec)
    return jnp.where(lane == ROUTE_W2, w2, rec)


def _merge_kernel(x_ref, ha_ref, om_ref, hs_ref, gm_ref, gs_ref, wbm_ref, wbs_ref, wout_ref, g_ref, b_ref,
                  wr_ref, br_ref, *refs, alpha, n_experts, n_groups, own_blocks):
    o_ref, route_ref = refs[-2:]
    i = pl.program_id(0)

    @pl.when(i < own_blocks)
    def _():
        h_m = (jax.nn.sigmoid(om_ref[...]) * ha_ref[...]).astype(BF16)
        t_m = jnp.dot(h_m, wbm_ref[...], preferred_element_type=F32)
        t_s = jnp.dot(hs_ref[...], wbs_ref[...], preferred_element_type=F32)
        merged = jax.nn.sigmoid(gm_ref[...]) * t_m + jax.nn.sigmoid(gs_ref[...]) * t_s
        y = jnp.dot(merged.astype(BF16), wout_ref[...], preferred_element_type=F32)
        x1 = _layer_norm(alpha * x_ref[...] + y, g_ref[...], b_ref[...])
        o_ref[...] = x1
        x_hi, x_lo = _bf16_pieces(x1, 2)
        hi_both = jnp.dot(x_hi, wr_ref[...], preferred_element_type=F32)
        lo_hi = jnp.dot(x_lo, wr_ref[:, :LANES], preferred_element_type=F32)
        logits = (hi_both[:, :LANES] + (hi_both[:, LANES:] + lo_hi)) + br_ref[...]
        route_ref[...] = _route(logits, n_experts=n_experts, n_groups=n_groups)

    @pl.when(i >= own_blocks)
    def _():
        o_ref[...] = jnp.zeros_like(o_ref)
        route_ref[...] = jnp.zeros_like(route_ref)


def _merge(x, h_a, proj, h_s, w_bm, w_bs, w_out, ln_g, ln_b, w_router, b_router, *,
           o_col, gm_col, gs_col, alpha, n_experts, n_groups, total_rows, row0, prev):
    n, dm = x.shape
    da, ds = h_a.shape[1], h_s.shape[1]
    tm = _tile(math.gcd(n, total_rows - n) if total_rows > n else n, 256)
    assert row0 % tm == 0 and total_rows % tm == 0 and (prev is not None or row0 == 0)
    blk0 = row0 // tm
    own = n // tm
    steps = own if prev is not None else total_rows // tm
    const = lambda shape: pl.BlockSpec(shape, lambda i: (0, 0), pipeline_mode=pl.Buffered(1))
    rows = lambda i: jnp.minimum(i, own - 1)
    prev = () if prev is None else tuple(prev)
    n_in = 13
    return pl.pallas_call(
        functools.partial(_merge_kernel, alpha=alpha, n_experts=n_experts, n_groups=n_groups, own_blocks=own),
        grid=(steps,),
        in_specs=[
            pl.BlockSpec((tm, dm), lambda i: (rows(i), 0)),
            pl.BlockSpec((tm, da), lambda i: (rows(i), 0)),
            pl.BlockSpec((tm, da), lambda i: (rows(i), o_col // da)),
            pl.BlockSpec((tm, ds), lambda i: (rows(i), 0)),
            pl.BlockSpec((tm, dm), lambda i: (rows(i), gm_col // dm)),
            pl.BlockSpec((tm, dm), lambda i: (rows(i), gs_col // dm)),
            const((da, dm)), const((ds, dm)), const((dm, dm)), const((1, dm)), const((1, dm)),
            const((dm, 2 * LANES)), const((1, LANES)),
        ] + [pl.BlockSpec(memory_space=pl.ANY)] * len(prev),
        out_specs=[pl.BlockSpec((tm, dm), lambda i: (blk0 + i, 0)),
                   pl.BlockSpec((tm, LANES), lambda i: (blk0 + i, 0))],
        out_shape=[jax.ShapeDtypeStruct((total_rows, dm), F32), jax.ShapeDtypeStruct((total_rows, LANES), F32)],
        input_output_aliases={n_in + k: k for k in range(len(prev))},
        compiler_params=pltpu.CompilerParams(dimension_semantics=("parallel",), vmem_limit_bytes=48 * MIB),
        name="merge_out_ln1",
    )(x, h_a, proj, h_s, proj, proj, w_bm, w_bs, w_out, ln_g, ln_b, w_router, b_router, *prev)


def _expert_kernel(te_ref, tb_ref, tv_ref, na_ref, perm_ref,
                   x_hbm, wg_ref, wu_ref, wd_ref, y_hbm,
                   xbuf, obuf, wgb, wub, wdb, gsem, ssem, *, n_tok):
    j = pl.program_id(0)
    n_act = na_ref[0]
    slot = lax.rem(j, 2)

    def gather_copy(s, i, token):
        return pltpu.make_async_copy(x_hbm.at[pl.ds(token, 1), :], xbuf.at[s, pl.ds(i, 1), :], gsem.at[s])

    def scatter_copy(s, i, row):
        return pltpu.make_async_copy(obuf.at[s, pl.ds(i, 1), :], y_hbm.at[pl.ds(row, 1), :], ssem.at[s])

    def for_rows(count, fn):
        groups = lax.shift_right_logical(count, 3)

        def group_body(g, c):
            for u in range(SUBLANES):
                fn(g * SUBLANES + u)
            return c

        def row_body(i, c):
            fn(i)
            return c

        lax.fori_loop(0, groups, group_body, 0)
        lax.fori_loop(groups * SUBLANES, count, row_body, 0)

    def start_gather(t, s):
        base = tb_ref[t]

        def start(i):
            pair = perm_ref[base + i]
            gather_copy(s, i, jnp.where(pair >= n_tok, pair - n_tok, pair)).start()

        for_rows(tv_ref[t], start)

    def start_scatter(t, s):
        base = tb_ref[t]
        for_rows(tv_ref[t], lambda i: scatter_copy(s, i, perm_ref[base + i]).start())

    def wait_rows(count, block_copy, row_copy):
        bulk = pl.multiple_of(lax.shift_right_logical(count, 3) * SUBLANES, SUBLANES)

        @pl.when(bulk > 0)
        def _():
            block_copy(pl.ds(0, bulk)).wait()

        def row_body(i, c):
            row_copy(i).wait()
            return c

        lax.fori_loop(bulk, count, row_body, 0)

    def wait_gather(t, s):
        wait_rows(tv_ref[t],
                  lambda rows: pltpu.make_async_copy(x_hbm.at[rows, :], xbuf.at[s, rows, :], gsem.at[s]),
                  lambda i: gather_copy(s, i, 0))

    def wait_scatter(t, s):
        wait_rows(tv_ref[t],
                  lambda rows: pltpu.make_async_copy(obuf.at[s, rows, :], y_hbm.at[rows, :], ssem.at[s]),
                  lambda i: scatter_copy(s, i, 0))

    @pl.when(j == 0)
    def _():
        xbuf[...] = jnp.zeros_like(xbuf)
        start_gather(0, 0)

    @pl.when(j + 1 < n_act)
    def _():
        start_gather(j + 1, 1 - slot)

    @pl.when(j < n_act)
    def _():
        wait_gather(j, slot)

        @pl.when(j >= 2)
        def _():
            wait_scatter(j - 2, slot)

        @pl.when(jnp.logical_or(j == 0, te_ref[j] != te_ref[jnp.maximum(j - 1, 0)]))
        def _():
            wgb[...] = wg_ref[0].astype(BF16)
            wub[...] = wu_ref[0].astype(BF16)
            wdb[...] = wd_ref[0].astype(BF16)

        x = xbuf[slot].astype(BF16)
        gate = jnp.dot(x, wgb[...], preferred_element_type=F32)
        up = jnp.dot(x, wub[...], preferred_element_type=F32)
        hid = (gate * jax.nn.sigmoid(gate) * up).astype(BF16)
        obuf[slot] = jnp.dot(hid, wdb[...], preferred_element_type=F32)
        start_scatter(j, slot)

        @pl.when(j == n_act - 1)
        def _():
            @pl.when(j >= 1)
            def _():
                wait_scatter(j - 1, 1 - slot)

            wait_scatter(j, slot)


def _expert_tables(route, n_experts, tile):
    n_tok = route.shape[0]
    i32 = jnp.int32
    e_flat = jnp.concatenate([route[:, ROUTE_E1], route[:, ROUTE_E2]]).astype(i32)
    n_pad = (1 << (2 * n_tok - 1).bit_length()) - 2 * n_tok
    keys = jnp.concatenate([e_flat, jnp.full((n_pad,), n_experts, i32)])
    perm = jnp.argsort(keys, stable=True)[:2 * n_tok].astype(i32)
    eids = jnp.arange(n_experts, dtype=i32)
    counts = jnp.sum((e_flat[:, None] == eids[None, :]).astype(i32), axis=0)
    tiles_e = (counts + (tile - 1)) // tile
    before = eids[None, :] < eids[:, None]
    pair_start = jnp.sum(jnp.where(before, counts[None, :], 0), axis=1)
    tile_start = jnp.sum(jnp.where(before, tiles_e[None, :], 0), axis=1)
    n_act = jnp.sum(tiles_e)
    t_max = (2 * n_tok) // tile + n_experts
    j = jnp.arange(t_max, dtype=i32)
    owner = jnp.logical_and(j[:, None] >= tile_start[None, :], j[:, None] < (tile_start + tiles_e)[None, :])
    pick = lambda table: jnp.sum(jnp.where(owner, table[None, :], 0), axis=1)
    active = j < n_act
    k = j - pick(tile_start)
    last_e = jnp.max(jnp.where(tiles_e > 0, eids, 0))
    te = jnp.where(active, pick(eids), last_e).astype(i32)
    tb = jnp.where(active, pick(pair_start) + k * tile, 0).astype(i32)
    tv = jnp.where(active, jnp.clip(pick(counts) - k * tile, 0, tile), 0).astype(i32)
    return te, tb, tv, n_act.reshape(1).astype(i32), perm, t_max


def _experts(x1, route, w_gate, w_up, w_down):
    n_tok, dm = x1.shape
    n_experts, _, de = w_gate.shape
    tile = 256
    te, tb, tv, n_act, perm, t_max = _expert_tables(route, n_experts, tile)
    w_in_spec = pl.BlockSpec((1, dm, de), lambda j, te, tb, tv, na, perm: (te[j], 0, 0))
    w_out_spec = pl.BlockSpec((1, de, dm), lambda j, te, tb, tv, na, perm: (te[j], 0, 0))
    grid_spec = pltpu.PrefetchScalarGridSpec(
        num_scalar_prefetch=5,
        grid=(t_max,),
        in_specs=[pl.BlockSpec(memory_space=pl.ANY), w_in_spec, w_in_spec, w_out_spec],
        out_specs=pl.BlockSpec(memory_space=pl.ANY),
        scratch_shapes=[
            pltpu.VMEM((2, tile, dm), F32), pltpu.VMEM((2, tile, dm), F32),
            pltpu.VMEM((dm, de), BF16), pltpu.VMEM((dm, de), BF16), pltpu.VMEM((de, dm), BF16),
            pltpu.SemaphoreType.DMA((2,)), pltpu.SemaphoreType.DMA((2,)),
        ],
    )
    return pl.pallas_call(
        functools.partial(_expert_kernel, n_tok=n_tok),
        grid_spec=grid_spec,
        out_shape=jax.ShapeDtypeStruct((2 * n_tok, dm), F32),
        compiler_params=pltpu.CompilerParams(dimension_semantics=("arbitrary",), vmem_limit_bytes=48 * MIB),
        name="moe_experts",
    )(te, tb, tv, n_act, perm, x1, w_gate, w_up, w_down)


def _combine_kernel(x_ref, ya_ref, yb_ref, r_ref, g_ref, b_ref, first_ref, second_ref, *, alpha, first_blocks):
    r = r_ref[...]
    moe = r[:, ROUTE_W1:ROUTE_W1 + 1] * ya_ref[...] + r[:, ROUTE_W2:ROUTE_W2 + 1] * yb_ref[...]
    out = _layer_norm(alpha * x_ref[...] + moe, g_ref[...], b_ref[...])
    i = pl.program_id(0)

    @pl.when(i < first_blocks)
    def _():
        first_ref[...] = out

    @pl.when(i >= first_blocks)
    def _():
        second_ref[...] = out


def _combine(x1, y_pairs, route, ln_g, ln_b, *, alpha, n_first):
    n, dm = x1.shape
    tm = _tile(math.gcd(n_first, n - n_first), 256)
    nb, nb_first = n // tm, n_first // tm
    const = lambda shape: pl.BlockSpec(shape, lambda i: (0, 0))
    return pl.pallas_call(
        functools.partial(_combine_kernel, alpha=alpha, first_blocks=nb_first),
        grid=(nb,),
        in_specs=[
            pl.BlockSpec((tm, dm), lambda i: (i, 0)),
            pl.BlockSpec((tm, dm), lambda i: (i, 0)),
            pl.BlockSpec((tm, dm), lambda i: (nb + i, 0)),
            pl.BlockSpec((tm, LANES), lambda i: (i, 0)),
            const((1, dm)), const((1, dm)),
        ],
        out_specs=[pl.BlockSpec((tm, dm), lambda i: (jnp.minimum(i, nb_first - 1), 0)),
                   pl.BlockSpec((tm, dm), lambda i: (jnp.maximum(i - nb_first, 0), 0))],
        out_shape=[jax.ShapeDtypeStruct((n_first, dm), F32), jax.ShapeDtypeStruct((n - n_first, dm), F32)],
        compiler_params=pltpu.CompilerParams(dimension_semantics=("arbitrary",), vmem_limit_bytes=48 * MIB),
        name="moe_combine_ln2",
    )(x1, y_pairs, y_pairs, route, ln_g, ln_b)


def _mixer(x, conv_state, c0, n0, m0, past_k, past_v, p, *, chunk_pref, total_rows, row0, prev):
    batch, seq, dm = x.shape
    n_tok = batch * seq
    xf = x.reshape(n_tok, dm)
    n_heads, d = p["n_heads"], p["head_dim"]
    da = n_heads * d
    cols = p["cols"]

    proj, q_s, k_hm, k_hm16, v_hm, v_hm16, gates = _project(
        xf, p["w_all"], p["w_gate"], a_cols=p["a_cols"], sb_cols=p["sb_cols"], b_cols=p["b_cols"], tn=p["tn"])

    chunk = _tile(seq, chunk_pref)
    h_a, c_new, n_new, m_new = _mlstm(
        proj, gates, p["b_gate_row"], p["conv_w"], p["conv_b"], conv_state, c0, n0, m0,
        batch=batch, seq=seq, chunk=chunk, n_heads=n_heads, head_dim=d,
        q_col=cols["q"], k_col=cols["k"], v_col=cols["v"])

    if past_k is None:
        assert batch == 1
        h_s = _sb_prompt(q_s, k_hm16, v_hm16, seq=seq)
    else:
        h_s = _sb_decode(q_s, k_hm16, v_hm16, past_k, past_v, batch=batch, seq=seq)

    x1, route = _merge(xf, h_a, proj, h_s, p["w_bm"], p["w_bs"], p["w_out"], p["ln1_g"], p["ln1_b"],
                       p["w_router"], p["b_router"],
                       o_col=cols["o"], gm_col=cols["gm"], gs_col=cols["gs"],
                       alpha=p["alpha"], n_experts=p["n_experts"], n_groups=p["n_groups"],
                       total_rows=total_rows, row0=row0, prev=prev)

    n_sb = k_hm.shape[0]
    sb_k = k_hm.reshape(n_sb, batch, seq, -1).transpose(1, 0, 2, 3)
    sb_v = v_hm.reshape(n_sb, batch, seq, -1).transpose(1, 0, 2, 3)
    new_conv = proj.reshape(batch, seq, -1)[:, seq - (CONV_W - 1):, cols["q"]:cols["q"] + 2 * da]
    return x1, route, (sb_k, sb_v, c_new, n_new, m_new, new_conv)


def _prepare_layer(l, depth, w_in, b_gate, conv_w, conv_b, w_branch_m, w_branch_s, w_out, ln1_g, ln1_b,
                   w_router_group, b_router_group, w_router_expert, b_router_expert,
                   w_exp_gate, w_exp_up, w_exp_down, ln2_g, ln2_b, head_dim):
    dm = w_in.shape[1]
    da2 = conv_w.shape[-1]
    da = da2 // 2
    n_heads = b_gate.shape[-1] // 2
    ds = w_branch_s.shape[1]
    sizes = (da2, da, da, 2 * n_heads, ds, ds, ds, dm, dm)
    offs = [0]
    for s in sizes:
        offs.append(offs[-1] + s)
    w = w_in[l]
    part = lambda i: w[:, offs[i]:offs[i + 1]]
    a_cols, b_cols = da2 + 2 * da, 2 * dm
    tn = math.gcd(math.gcd(math.gcd(1024, a_cols), b_cols), ds)
    w_all = jnp.concatenate([w[:, :offs[3]].astype(BF16), w[:, offs[4]:].astype(BF16)], axis=1)
    w_gate = jnp.pad(part(3).astype(BF16), ((0, 0), (0, LANES - 2 * n_heads)))
    cols = {"q": 0, "k": da, "v": da2, "o": da2 + da, "gm": da2 + 2 * da, "gs": da2 + 2 * da + dm}
    assert cols["k"] % da == 0 and cols["o"] % da == 0 and cols["gm"] % dm == 0 and cols["gs"] % dm == 0
    b_gate_row = jnp.pad(b_gate[l].astype(F32), (0, LANES - 2 * n_heads)).reshape(1, LANES)
    n_groups = w_router_group.shape[-1]
    n_experts = w_router_expert.shape[-1]
    pad_r = LANES - n_experts - n_groups
    w_router = jnp.pad(jnp.concatenate([w_router_expert[l], w_router_group[l]], axis=1),
                       ((0, 0), (0, pad_r))).astype(F32)
    w_router_hi = w_router.astype(BF16)
    w_router_lo = (w_router - w_router_hi.astype(F32)).astype(BF16)
    b_router = jnp.pad(jnp.concatenate([b_router_expert[l], b_router_group[l]]), (0, pad_r)).reshape(1, LANES)
    return {
        "n_heads": n_heads, "head_dim": head_dim, "cols": cols, "n_groups": n_groups, "n_experts": n_experts,
        "alpha": (2.0 * depth) ** 0.25, "a_cols": a_cols, "b_cols": b_cols, "sb_cols": ds, "tn": tn,
        "w_all": w_all, "w_gate": w_gate, "b_gate_row": b_gate_row,
        "conv_w": conv_w[l], "conv_b": conv_b[l],
        "w_bm": w_branch_m[l].astype(BF16), "w_bs": w_branch_s[l].astype(BF16), "w_out": w_out[l].astype(BF16),
        "ln1_g": ln1_g[l].reshape(1, dm), "ln1_b": ln1_b[l].reshape(1, dm),
        "w_router": jnp.concatenate([w_router_hi, w_router_lo], axis=1), "b_router": b_router.astype(F32),
        "w_eg": w_exp_gate[l], "w_eu": w_exp_up[l], "w_ed": w_exp_down[l],
        "ln2_g": ln2_g[l].reshape(1, dm), "ln2_b": ln2_b[l].reshape(1, dm),
    }


def kernel(x_prompt, x_sample, cache_sb_k, cache_sb_v, state_mlstm_C, state_mlstm_n, state_mlstm_m, state_conv,
           w_in, b_gate, conv_w, conv_b, w_branch_m, w_branch_s, w_out, ln1_g, ln1_b,
           w_router_group, b_router_group, w_router_expert, b_router_expert,
           w_exp_gate, w_exp_up, w_exp_down, ln2_g, ln2_b):
    depth = w_in.shape[0]
    bp = x_prompt.shape[0]
    n_heads, head_dim = state_mlstm_C.shape[2], state_mlstm_C.shape[3]
    da2 = conv_w.shape[-1]
    xp, xs = x_prompt, x_sample
    st_p_all, st_s_all = [], []
    for l in range(depth):
        p = _prepare_layer(l, depth, w_in, b_gate, conv_w, conv_b, w_branch_m, w_branch_s, w_out, ln1_g, ln1_b,
                           w_router_group, b_router_group, w_router_expert, b_router_expert,
                           w_exp_gate, w_exp_up, w_exp_down, ln2_g, ln2_b, head_dim)
        n_p, n_s = xp.shape[0] * xp.shape[1], xs.shape[0] * xs.shape[1]
        x1, route, st_p = _mixer(
            xp, jnp.zeros((bp, CONV_W - 1, da2), F32), jnp.zeros((bp, n_heads, head_dim, head_dim), F32),
            jnp.zeros((bp, n_heads, head_dim), F32), jnp.full((bp, n_heads), NEG, F32), None, None, p,
            chunk_pref=256, total_rows=n_p + n_s, row0=0, prev=None)
        x1, route, st_s = _mixer(xs, state_conv[l], state_mlstm_C[l], state_mlstm_n[l], state_mlstm_m[l],
                                 cache_sb_k[l], cache_sb_v[l], p, chunk_pref=256,
                                 total_rows=n_p + n_s, row0=n_p, prev=(x1, route))
        y_pairs = _experts(x1, route, p["w_eg"], p["w_eu"], p["w_ed"])
        xp2, xs2 = _combine(x1, y_pairs, route, p["ln2_g"], p["ln2_b"], alpha=p["alpha"], n_first=n_p)
        xp, xs = xp2.reshape(xp.shape), xs2.reshape(xs.shape)
        st_p_all.append(st_p)
        st_s_all.append(st_s)
    stack = lambda states, i: jnp.stack([s[i] for s in states], axis=0)
    return (xp, xs,
            *(stack(st_p_all, i) for i in range(6)),
            *(stack(st_s_all, i) for i in range(6)))
```

```python
import functools
import math

import jax
import jax.numpy as jnp
from jax import lax
from jax.experimental import pallas as pl
from jax.experimental.pallas import tpu as pltpu

F32 = jnp.float32
BF16 = jnp.bfloat16

LANES = 128
SUBLANES = 8
LN_EPS = 1e-5
NEG = -1e30
CONV_W = 4
EXP_ZERO_BELOW = -104.0
MIB = 1024 * 1024

_NT = (((1,), (1,)), ((), ()))
_TN = (((0,), (0,)), ((), ()))


def _tile(n, pref):
    if n <= pref:
        return n
    t = pref
    while t >= SUBLANES:
        if n % t == 0 and t % SUBLANES == 0:
            return t
        t -= SUBLANES
    return n


def _log_sigmoid(x):
    return jnp.minimum(x, 0.0) - jnp.log1p(jnp.exp(-jnp.abs(x)))


def _layer_norm(x, g, b):
    mu = jnp.mean(x, axis=-1, keepdims=True)
    d = x - mu
    var = jnp.mean(d * d, axis=-1, keepdims=True)
    return d * lax.rsqrt(var + LN_EPS) * g + b


def _bf16_pieces(x, n):
    pieces = []
    rem = x
    for _ in range(n - 1):
        p = rem.astype(BF16)
        pieces.append(p)
        rem = rem - p.astype(F32)
    pieces.append(rem.astype(BF16))
    return pieces


def _dot_pieces(pieces, other, *, pieces_first, dims=None):
    total = None
    for p in pieces:
        a, b = (p, other) if pieces_first else (other, p)
        if dims is None:
            t = jnp.dot(a, b, preferred_element_type=F32)
        else:
            t = lax.dot_general(a, b, dims, preferred_element_type=F32)
        total = t if total is None else total + t
    return total


def _repack_kernel(lo_ref, hi_ref, o_ref, g_ref, *, a_tiles, n_gate):
    c = pl.program_id(1)

    @pl.when(c < a_tiles)
    def _():
        o_ref[...] = lo_ref[...].astype(BF16)

    @pl.when(c >= a_tiles)
    def _():
        o_ref[...] = jnp.concatenate([lo_ref[:, n_gate:], hi_ref[:, :n_gate]], axis=1).astype(BF16)

    @pl.when(c == a_tiles)
    def _():
        lane = lax.broadcasted_iota(jnp.int32, g_ref.shape, 1)
        g_ref[...] = jnp.where(lane < n_gate, lo_ref[:, :LANES], 0.0).astype(BF16)


def _repack_weight(w, *, a_cols, n_gate, tn):
    k, c_in = w.shape
    c_out = c_in - n_gate
    assert a_cols % tn == 0 and c_out % tn == 0 and n_gate <= LANES
    a_tiles, n_tiles = a_cols // tn, c_out // tn
    tr = _tile(k, 512)
    return pl.pallas_call(
        functools.partial(_repack_kernel, a_tiles=a_tiles, n_gate=n_gate),
        grid=(k // tr, n_tiles),
        in_specs=[pl.BlockSpec((tr, tn), lambda r, c: (r, c)),
                  pl.BlockSpec((tr, tn), lambda r, c: (r, c + 1))],
        out_specs=[pl.BlockSpec((tr, tn), lambda r, c: (r, c)),
                   pl.BlockSpec((tr, LANES), lambda r, c: (r, 0))],
        out_shape=[jax.ShapeDtypeStruct((k, c_out), BF16), jax.ShapeDtypeStruct((k, LANES), BF16)],
        compiler_params=pltpu.CompilerParams(dimension_semantics=("parallel", "arbitrary")),
        name="repack_w_in",
    )(w, w)


def _proj_kernel(x_ref, w_ref, wg_ref, main_ref, qs_ref, k32_ref, k16_ref, v32_ref, v16_ref, g_ref, xb_ref, *, ends):
    j = pl.program_id(1)
    end_a, end_qs, end_k, end_v = ends

    @pl.when(j == 0)
    def _():
        xb_ref[...] = x_ref[...].astype(BF16)
        g_ref[...] = jnp.dot(xb_ref[...], wg_ref[...], preferred_element_type=F32)

    acc = jnp.dot(xb_ref[...], w_ref[...], preferred_element_type=F32)

    def store_heads(f32_ref, bf16_ref):
        for h in range(f32_ref.shape[0]):
            piece = acc[:, h * LANES:(h + 1) * LANES]
            f32_ref[h] = piece
            bf16_ref[h] = piece.astype(BF16)

    @pl.when(jnp.logical_or(j < end_a, j >= end_v))
    def _():
        main_ref[...] = acc

    @pl.when(jnp.logical_and(j >= end_a, j < end_qs))
    def _():
        qs_ref[...] = acc.astype(BF16)

    @pl.when(jnp.logical_and(j >= end_qs, j < end_k))
    def _():
        store_heads(k32_ref, k16_ref)

    @pl.when(jnp.logical_and(j >= end_k, j < end_v))
    def _():
        store_heads(v32_ref, v16_ref)


def _project(x, w_all, w_gate, *, a_cols, sb_cols, b_cols, tn):
    n, k = x.shape
    tm = _tile(n, 1024)
    t_a, t_sb, t_b = a_cols // tn, sb_cols // tn, b_cols // tn
    ends = (t_a, t_a + t_sb, t_a + 2 * t_sb, t_a + 3 * t_sb)
    n_tiles = ends[-1] + t_b
    main_cols = a_cols + b_cols
    assert w_all.shape[1] == n_tiles * tn
    local = lambda j, start, count: jnp.clip(j - start, 0, count - 1)
    main_tile = lambda j: jnp.where(j < ends[3], jnp.minimum(j, t_a - 1), j - 3 * t_sb)
    heads = tn // LANES
    n_sb = sb_cols // LANES
    once = pl.Buffered(1)
    hm_spec = lambda start: pl.BlockSpec((heads, tm, LANES), lambda i, j: (local(j, start, t_sb), i, 0),
                                         pipeline_mode=once)
    return pl.pallas_call(
        functools.partial(_proj_kernel, ends=ends),
        grid=(n // tm, n_tiles),
        in_specs=[pl.BlockSpec((tm, k), lambda i, j: (i, 0), pipeline_mode=once),
                  pl.BlockSpec((k, tn), lambda i, j: (0, j)),
                  pl.BlockSpec((k, LANES), lambda i, j: (0, 0), pipeline_mode=once)],
        out_specs=[
            pl.BlockSpec((tm, tn), lambda i, j: (i, main_tile(j))),
            pl.BlockSpec((tm, tn), lambda i, j: (i, local(j, ends[0], t_sb)), pipeline_mode=once),
            hm_spec(ends[1]), hm_spec(ends[1]), hm_spec(ends[2]), hm_spec(ends[2]),
            pl.BlockSpec((tm, LANES), lambda i, j: (i, 0), pipeline_mode=once),
        ],
        out_shape=[
            jax.ShapeDtypeStruct((n, main_cols), F32),
            jax.ShapeDtypeStruct((n, sb_cols), BF16),
            jax.ShapeDtypeStruct((n_sb, n, LANES), F32), jax.ShapeDtypeStruct((n_sb, n, LANES), BF16),
            jax.ShapeDtypeStruct((n_sb, n, LANES), F32), jax.ShapeDtypeStruct((n_sb, n, LANES), BF16),
            jax.ShapeDtypeStruct((n, LANES), F32),
        ],
        scratch_shapes=[pltpu.VMEM((tm, k), BF16)],
        compiler_params=pltpu.CompilerParams(
            dimension_semantics=("parallel", "arbitrary"),
            vmem_limit_bytes=56 * MIB),
        name="in_proj",
    )(x, w_all, w_gate)


def _mlstm_kernel(q_ref, k_ref, v_ref, g_ref, bg_ref, cwq_ref, cwk_ref, cbq_ref, cbk_ref,
                  csq_ref, csk_ref, c0_ref, n0_ref, m0_ref,
                  h_ref, c_ref, n_ref, m_ref, qext_ref, kext_ref, *, chunk, n_heads, head_dim):
    step = pl.program_id(1)
    L, d = chunk, head_dim
    halo = CONV_W - 1
    base = SUBLANES - halo

    @pl.when(step == 0)
    def _():
        qext_ref[base:SUBLANES, :] = csq_ref[0]
        kext_ref[base:SUBLANES, :] = csk_ref[0]
        c_ref[...] = c0_ref[...]
        n_ref[...] = n0_ref[...]
        m_ref[...] = m0_ref[...]

    def conv_silu(raw_ref, ext_ref, cw_ref, cb_ref):
        ext_ref[SUBLANES:SUBLANES + L, :] = raw_ref[...]
        acc = cb_ref[...] + ext_ref[base:base + L, :] * cw_ref[0:1, :]
        for j in range(1, CONV_W):
            acc = acc + ext_ref[base + j:base + j + L, :] * cw_ref[j:j + 1, :]
        ext_ref[base:SUBLANES, :] = ext_ref[base + L:SUBLANES + L, :]
        return acc * jax.nn.sigmoid(acc)

    q_all = conv_silu(q_ref, qext_ref, cwq_ref, cbq_ref)
    k_all = conv_silu(k_ref, kext_ref, cwk_ref, cbk_ref) * (float(d) ** -0.5)

    lane = lax.broadcasted_iota(jnp.int32, (L, LANES), 1)
    pre = g_ref[...] + bg_ref[...]
    gates = jnp.where(lane < n_heads, pre, _log_sigmoid(pre))
    row_i = lax.broadcasted_iota(jnp.int32, (L, L), 0)
    col_i = lax.broadcasted_iota(jnp.int32, (L, L), 1)
    causal = col_i <= row_i
    tril = causal.astype(BF16)
    triu = (row_i <= col_i).astype(BF16)
    sel = (lax.broadcasted_iota(jnp.int32, (SUBLANES, LANES), 0)
           == lax.broadcasted_iota(jnp.int32, (SUBLANES, LANES), 1)).astype(BF16)
    gate_pieces = _bf16_pieces(gates, 3)
    cum_cols = _dot_pieces(gate_pieces, tril, pieces_first=False)
    gates_t = _dot_pieces(gate_pieces, sel, pieces_first=False, dims=_NT)
    cum_rows = _dot_pieces(_bf16_pieces(gates_t, 3), triu, pieces_first=True)

    for h in range(n_heads):
        cols = slice(h * d, (h + 1) * d)
        q = q_all[:, cols]
        k = k_all[:, cols]
        q_b = q.astype(BF16)
        k_b = k.astype(BF16)
        v_b = v_ref[:, cols].astype(BF16)
        i_col = gates[:, h:h + 1]
        b_col = cum_cols[:, n_heads + h:n_heads + h + 1]
        i_row = gates_t[h:h + 1, :]
        b_row = cum_rows[n_heads + h:n_heads + h + 1, :]
        c_prev = c_ref[0, h]
        n_prev = n_ref[0, h]
        m_prev = m_ref[0, h][:, 0:1]

        dmat = jnp.where(causal, b_col - b_row + i_row, NEG)
        inter = b_col + m_prev
        m_t = jnp.maximum(jnp.max(dmat, axis=-1, keepdims=True), inter)
        w = jnp.exp(dmat - m_t)
        s = lax.dot_general(q_b, k_b, _NT, preferred_element_type=F32) * w
        e_inter = jnp.exp(inter - m_t)
        num = (jnp.dot(s.astype(BF16), v_b, preferred_element_type=F32)
               + e_inter * jnp.dot(q_b, c_prev.astype(BF16), preferred_element_type=F32))
        den = jnp.sum(s, axis=-1, keepdims=True) + e_inter * jnp.sum(q * n_prev, axis=-1, keepdims=True)
        h_ref[:, cols] = num / jnp.maximum(jnp.abs(den), jnp.exp(-m_t))

        g_last = b_col[L - 1:L, :]
        w_last = g_last - b_col + i_col
        m_new = jnp.maximum(g_last + m_prev, jnp.max(w_last, axis=0, keepdims=True))
        decay = jnp.exp(g_last + m_prev - m_new)
        kw = jnp.exp(w_last - m_new) * k
        c_ref[0, h] = decay * c_prev + lax.dot_general(kw.astype(BF16), v_b, _TN, preferred_element_type=F32)
        n_ref[0, h] = decay * n_prev + jnp.sum(kw, axis=0, keepdims=True)
        m_ref[0, h] = jnp.broadcast_to(m_new, (1, LANES))


def _mlstm(proj, gates, b_gate_row, conv_w, conv_b, conv_state, c0, n0, m0, *, batch, seq, chunk,
           n_heads, head_dim, q_col, k_col, v_col):
    n_tok = batch * seq
    nc = seq // chunk
    d = head_dim
    da = n_heads * d
    qb, kb, vb = q_col // da, k_col // da, v_col // da
    kcw = (k_col - q_col) // da
    row = lambda b, c: b * nc + c
    n0r = n0.reshape(batch, n_heads, 1, d)
    m0r = jnp.broadcast_to(m0.reshape(batch, n_heads, 1, 1), (batch, n_heads, 1, LANES))
    cbr = conv_b.reshape(1, -1)
    state_spec = lambda shape: pl.BlockSpec(shape, lambda b, c: (b, 0, 0, 0))
    h, c_new, n_new, m_new = pl.pallas_call(
        functools.partial(_mlstm_kernel, chunk=chunk, n_heads=n_heads, head_dim=d),
        grid=(batch, nc),
        in_specs=[
            pl.BlockSpec((chunk, da), lambda b, c: (row(b, c), qb)),
            pl.BlockSpec((chunk, da), lambda b, c: (row(b, c), kb)),
            pl.BlockSpec((chunk, da), lambda b, c: (row(b, c), vb)),
            pl.BlockSpec((chunk, LANES), lambda b, c: (row(b, c), 0)),
            pl.BlockSpec((1, LANES), lambda b, c: (0, 0)),
            pl.BlockSpec((CONV_W, da), lambda b, c: (0, 0)),
            pl.BlockSpec((CONV_W, da), lambda b, c: (0, kcw)),
            pl.BlockSpec((1, da), lambda b, c: (0, 0)),
            pl.BlockSpec((1, da), lambda b, c: (0, kcw)),
            pl.BlockSpec((1, CONV_W - 1, da), lambda b, c: (b, 0, 0)),
            pl.BlockSpec((1, CONV_W - 1, da), lambda b, c: (b, 0, kcw)),
            state_spec((1, n_heads, d, d)),
            state_spec((1, n_heads, 1, d)),
            state_spec((1, n_heads, 1, LANES)),
        ],
        out_specs=[
            pl.BlockSpec((chunk, da), lambda b, c: (row(b, c), 0)),
            state_spec((1, n_heads, d, d)),
            state_spec((1, n_heads, 1, d)),
            state_spec((1, n_heads, 1, LANES)),
        ],
        out_shape=[
            jax.ShapeDtypeStruct((n_tok, da), F32),
            jax.ShapeDtypeStruct((batch, n_heads, d, d), F32),
            jax.ShapeDtypeStruct((batch, n_heads, 1, d), F32),
            jax.ShapeDtypeStruct((batch, n_heads, 1, LANES), F32),
        ],
        scratch_shapes=[pltpu.VMEM((chunk + SUBLANES, da), F32), pltpu.VMEM((chunk + SUBLANES, da), F32)],
        compiler_params=pltpu.CompilerParams(
            dimension_semantics=("parallel", "arbitrary"),
            vmem_limit_bytes=32 * MIB),
        name="mlstm",
    )(proj, proj, proj, gates, b_gate_row, conv_w, conv_w, cbr, cbr, conv_state, conv_state, c0, n0r, m0r)
    return h, c_new, n_new.reshape(batch, n_heads, d), m_new[:, :, 0, 0]


def _sb_scores(q_b, k_b, *, scale, diagonal):
    tq, tk = q_b.shape[0], k_b.shape[0]
    z = lax.dot_general(q_b, k_b, _NT, preferred_element_type=F32) * scale
    softplus = jnp.maximum(z, 0.0) + jnp.log(1.0 + jnp.exp(-jnp.abs(z)))
    log_beta = z - softplus
    drop, mask = softplus, None
    if diagonal:
        mask = (lax.broadcasted_iota(jnp.int32, (tq, tk), 1) < lax.broadcasted_iota(jnp.int32, (tq, tk), 0))
        drop = jnp.where(mask, softplus, 0.0)
    later_mat = (lax.broadcasted_iota(jnp.int32, (tk, tk), 0)
                 > lax.broadcasted_iota(jnp.int32, (tk, tk), 1)).astype(BF16)
    later = _dot_pieces(_bf16_pieces(drop, 2), later_mat, pieces_first=True)
    total = later[:, 0:1] + drop[:, 0:1]
    return log_beta, later, total, mask


def _sb_weights(scores, v_b, carry, acc):
    log_beta, later, total, mask = scores
    a = jnp.exp(log_beta - (later + carry))
    if mask is not None:
        a = jnp.where(mask, a, 0.0)
    return carry + total, acc + jnp.dot(a.astype(BF16), v_b, preferred_element_type=F32)


def _sb_block(q_b, k_b, v_b, carry, acc, *, scale, diagonal):
    return _sb_weights(_sb_scores(q_b, k_b, scale=scale, diagonal=diagonal), v_b, carry, acc)


def _sb_sweep(qs, k_ats, v_ats, first_block, carries, accs, *, tk, scale):
    def carry_min(cs):
        m = cs[0]
        for c in cs[1:]:
            m = jnp.minimum(m, c)
        return jnp.min(m)

    def cond(state):
        j, cmin, _, _ = state
        return jnp.logical_and(j >= 0, cmin < -EXP_ZERO_BELOW)

    def body(state):
        j, _, cs, acs = state
        start = pl.multiple_of(j * tk, tk)
        new = [_sb_block(q, k_at[pl.ds(start, tk), :].astype(BF16), v_at[pl.ds(start, tk), :].astype(BF16),
                         c, a, scale=scale, diagonal=False)
               for q, k_at, v_at, c, a in zip(qs, k_ats, v_ats, cs, acs)]
        cs = tuple(n[0] for n in new)
        return j - 1, carry_min(cs), cs, tuple(n[1] for n in new)

    carries = tuple(carries)
    _, _, _, accs = lax.while_loop(cond, body, (first_block, carry_min(carries), carries, tuple(accs)))
    return accs


def _sb_prompt_kernel(q_ref, k_ref, v_ref, o_ref, *, tq, scale):
    i = pl.program_id(1)
    n_heads, _, d = k_ref.shape
    start = pl.multiple_of(i * tq, tq)
    qs = [q_ref[:, h * d:(h + 1) * d] for h in range(n_heads)]
    zero_carry, zero_acc = jnp.zeros((tq, 1), F32), jnp.zeros((tq, d), F32)

    def finish(states, first_block):
        accs = _sb_sweep(qs, [k_ref.at[h] for h in range(n_heads)], [v_ref.at[h] for h in range(n_heads)],
                         first_block, [s[0] for s in states], [s[1] for s in states], tk=tq, scale=scale)
        for h in range(n_heads):
            o_ref[:, h * d:(h + 1) * d] = accs[h].astype(o_ref.dtype)

    @pl.when(i == 0)
    def _():
        finish([_sb_block(qs[h], k_ref[h, pl.ds(start, tq), :], v_ref[h, pl.ds(start, tq), :],
                          zero_carry, zero_acc, scale=scale, diagonal=True) for h in range(n_heads)], i - 1)

    @pl.when(i > 0)
    def _():
        prev = pl.multiple_of(start - tq, tq)
        states = []
        for h in range(n_heads):
            diag = _sb_scores(qs[h], k_ref[h, pl.ds(start, tq), :], scale=scale, diagonal=True)
            full = _sb_scores(qs[h], k_ref[h, pl.ds(prev, tq), :], scale=scale, diagonal=False)
            carry, acc = _sb_weights(diag, v_ref[h, pl.ds(start, tq), :], zero_carry, zero_acc)
            states.append(_sb_weights(full, v_ref[h, pl.ds(prev, tq), :], carry, acc))
        finish(states, i - 2)


def _sb_prompt(q, k_hm, v_hm, *, seq, heads_per_step=2):
    n_heads, _, d = k_hm.shape
    hb = heads_per_step if n_heads % heads_per_step == 0 else 1
    tq = _tile(seq, 256)
    kv_spec = pl.BlockSpec((hb, seq, d), lambda g, i: (g, 0, 0))
    return pl.pallas_call(
        functools.partial(_sb_prompt_kernel, tq=tq, scale=float(d) ** -0.5),
        grid=(n_heads // hb, seq // tq),
        in_specs=[pl.BlockSpec((tq, hb * d), lambda g, i: (i, g)), kv_spec, kv_spec],
        out_specs=pl.BlockSpec((tq, hb * d), lambda g, i: (i, g)),
        out_shape=jax.ShapeDtypeStruct((seq, n_heads * d), BF16),
        compiler_params=pltpu.CompilerParams(
            dimension_semantics=("parallel", "arbitrary"),
            vmem_limit_bytes=48 * MIB),
        name="sb_prompt",
    )(q, k_hm, v_hm)


def _sb_decode_kernel(q_ref, kn_ref, vn_ref, kp_ref, vp_ref, o_ref, *, tk, scale):
    n_heads, tq, d = kn_ref.shape
    last = kp_ref.shape[2] // tk - 1
    newest = pl.ds(last * tk, tk)
    zero_carry, zero_acc = jnp.zeros((tq, 1), F32), jnp.zeros((tq, d), F32)
    qs, carries, accs = [], [], []
    for h in range(n_heads):
        q_b = q_ref[:, h * d:(h + 1) * d]
        diag = _sb_scores(q_b, kn_ref[h], scale=scale, diagonal=True)
        full = _sb_scores(q_b, kp_ref[0, h, newest, :].astype(BF16), scale=scale, diagonal=False)
        carry, acc = _sb_weights(diag, vn_ref[h], zero_carry, zero_acc)
        carry, acc = _sb_weights(full, vp_ref[0, h, newest, :].astype(BF16), carry, acc)
        qs.append(q_b)
        carries.append(carry)
        accs.append(acc)
    accs = _sb_sweep(qs, [kp_ref.at[0, h] for h in range(n_heads)], [vp_ref.at[0, h] for h in range(n_heads)],
                     last - 1, carries, accs, tk=tk, scale=scale)
    for h in range(n_heads):
        o_ref[:, h * d:(h + 1) * d] = accs[h].astype(o_ref.dtype)


def _sb_decode(q, k_hm, v_hm, past_k, past_v, *, batch, seq, heads_per_step=4):
    n_heads, _, d = k_hm.shape
    hb = heads_per_step if n_heads % heads_per_step == 0 else 1
    past_len = past_k.shape[2]
    tk = _tile(past_len, 256)
    new_spec = pl.BlockSpec((hb, seq, d), lambda b, g: (g, b, 0))
    past_spec = pl.BlockSpec((1, hb, past_len, d), lambda b, g: (b, g, 0, 0))
    return pl.pallas_call(
        functools.partial(_sb_decode_kernel, tk=tk, scale=float(d) ** -0.5),
        grid=(batch, n_heads // hb),
        in_specs=[pl.BlockSpec((seq, hb * d), lambda b, g: (b, g)), new_spec, new_spec, past_spec, past_spec],
        out_specs=pl.BlockSpec((seq, hb * d), lambda b, g: (b, g)),
        out_shape=jax.ShapeDtypeStruct((batch * seq, n_heads * d), BF16),
        compiler_params=pltpu.CompilerParams(dimension_semantics=("parallel", "parallel")),
        name="sb_decode",
    )(q, k_hm, v_hm, past_k, past_v)


ROUTE_E1, ROUTE_E2, ROUTE_W1, ROUTE_W2 = 0, 1, 2, 3


def _route(logits, *, n_experts, n_groups):
    per_group = n_experts // n_groups
    lane = lax.broadcasted_iota(jnp.int32, logits.shape, 1)
    big = jnp.int32(LANES)
    is_group = jnp.logical_and(lane >= n_experts, lane < n_experts + n_groups)
    lg = jnp.where(is_group, logits, -jnp.inf)
    lg_max = jnp.max(lg, axis=-1, keepdims=True)
    p_sel = 1.0 / jnp.sum(jnp.exp(lg - lg_max), axis=-1, keepdims=True)
    grp = jnp.min(jnp.where(lg == lg_max, lane - n_experts, big), axis=-1, keepdims=True)
    in_grp = jnp.logical_and(lane >= grp * per_group, lane < (grp + 1) * per_group)
    le = jnp.where(in_grp, logits, -jnp.inf)
    v1 = jnp.max(le, axis=-1, keepdims=True)
    i1 = jnp.min(jnp.where(le == v1, lane, big), axis=-1, keepdims=True)
    le2 = jnp.where(lane == i1, -jnp.inf, le)
    v2 = jnp.max(le2, axis=-1, keepdims=True)
    i2 = jnp.min(jnp.where(le2 == v2, lane, big), axis=-1, keepdims=True)
    e2 = jnp.exp(v2 - v1)
    w1 = p_sel / (1.0 + e2)
    w2 = p_sel * e2 / (1.0 + e2)
    rec = jnp.where(lane == ROUTE_E1, i1.astype(F32), 0.0)
    rec = jnp.where(lane == ROUTE_E2, i2.astype(F32), rec)
    rec = jnp.where(lane == ROUTE_W1, w1, rec)
    return jnp.where(lane == ROUTE_W2, w2, rec)


def _merge_kernel(x_ref, ha_ref, om_ref, hs_ref, gm_ref, gs_ref, wbm_ref, wbs_ref, wout_ref, g_ref, b_ref,
                  wr_ref, br_ref, *refs, alpha, n_experts, n_groups, own_blocks):
    o_ref, route_ref = refs[-2:]
    i = pl.program_id(0)

    @pl.when(i < own_blocks)
    def _():
        h_m = (jax.nn.sigmoid(om_ref[...]) * ha_ref[...]).astype(BF16)
        t_m = jnp.dot(h_m, wbm_ref[...], preferred_element_type=F32)
        t_s = jnp.dot(hs_ref[...], wbs_ref[...], preferred_element_type=F32)
        merged = jax.nn.sigmoid(gm_ref[...]) * t_m + jax.nn.sigmoid(gs_ref[...]) * t_s
        y = jnp.dot(merged.astype(BF16), wout_ref[...], preferred_element_type=F32)
        x1 = _layer_norm(alpha * x_ref[...] + y, g_ref[...], b_ref[...])
        o_ref[...] = x1
        x_hi, x_lo = _bf16_pieces(x1, 2)
        hi_both = jnp.dot(x_hi, wr_ref[...], preferred_element_type=F32)
        lo_hi = jnp.dot(x_lo, wr_ref[:, :LANES], preferred_element_type=F32)
        logits = (hi_both[:, :LANES] + (hi_both[:, LANES:] + lo_hi)) + br_ref[...]
        route_ref[...] = _route(logits, n_experts=n_experts, n_groups=n_groups)

    @pl.when(i >= own_blocks)
    def _():
        o_ref[...] = jnp.zeros_like(o_ref)
        route_ref[...] = jnp.zeros_like(route_ref)


def _merge(x, h_a, proj, h_s, w_bm, w_bs, w_out, ln_g, ln_b, w_router, b_router, *,
           o_col, gm_col, gs_col, alpha, n_experts, n_groups, total_rows, row0, prev):
    n, dm = x.shape
    da, ds = h_a.shape[1], h_s.shape[1]
    tm = _tile(math.gcd(n, total_rows - n) if total_rows > n else n, 256)
    assert row0 % tm == 0 and total_rows % tm == 0 and (prev is not None or row0 == 0)
    blk0 = row0 // tm
    own = n // tm
    steps = own if prev is not None else total_rows // tm
    const = lambda shape: pl.BlockSpec(shape, lambda i: (0, 0), pipeline_mode=pl.Buffered(1))
    rows = lambda i: jnp.minimum(i, own - 1)
    prev = () if prev is None else tuple(prev)
    n_in = 13
    return pl.pallas_call(
        functools.partial(_merge_kernel, alpha=alpha, n_experts=n_experts, n_groups=n_groups, own_blocks=own),
        grid=(steps,),
        in_specs=[
            pl.BlockSpec((tm, dm), lambda i: (rows(i), 0)),
            pl.BlockSpec((tm, da), lambda i: (rows(i), 0)),
            pl.BlockSpec((tm, da), lambda i: (rows(i), o_col // da)),
            pl.BlockSpec((tm, ds), lambda i: (rows(i), 0)),
            pl.BlockSpec((tm, dm), lambda i: (rows(i), gm_col // dm)),
            pl.BlockSpec((tm, dm), lambda i: (rows(i), gs_col // dm)),
            const((da, dm)), const((ds, dm)), const((dm, dm)), const((1, dm)), const((1, dm)),
            const((dm, 2 * LANES)), const((1, LANES)),
        ] + [pl.BlockSpec(memory_space=pl.ANY)] * len(prev),
        out_specs=[pl.BlockSpec((tm, dm), lambda i: (blk0 + i, 0)),
                   pl.BlockSpec((tm, LANES), lambda i: (blk0 + i, 0))],
        out_shape=[jax.ShapeDtypeStruct((total_rows, dm), F32), jax.ShapeDtypeStruct((total_rows, LANES), F32)],
        input_output_aliases={n_in + k: k for k in range(len(prev))},
        compiler_params=pltpu.CompilerParams(dimension_semantics=("parallel",), vmem_limit_bytes=48 * MIB),
        name="merge_out_ln1",
    )(x, h_a, proj, h_s, proj, proj, w_bm, w_bs, w_out, ln_g, ln_b, w_router, b_router, *prev)


def _expert_kernel(te_ref, tb_ref, tv_ref, na_ref, perm_ref,
                   x_hbm, wg_ref, wu_ref, wd_ref, y_hbm,
                   xbuf, obuf, wgb, wub, wdb, gsem, ssem, *, n_tok):
    j = pl.program_id(0)
    n_act = na_ref[0]
    slot = lax.rem(j, 2)

    def gather_copy(s, i, token):
        return pltpu.make_async_copy(x_hbm.at[pl.ds(token, 1), :], xbuf.at[s, pl.ds(i, 1), :], gsem.at[s])

    def scatter_copy(s, i, row):
        return pltpu.make_async_copy(obuf.at[s, pl.ds(i, 1), :], y_hbm.at[pl.ds(row, 1), :], ssem.at[s])

    def for_rows(count, fn):
        groups = lax.shift_right_logical(count, 3)

        def group_body(g, c):
            for u in range(SUBLANES):
                fn(g * SUBLANES + u)
            return c

        def row_body(i, c):
            fn(i)
            return c

        lax.fori_loop(0, groups, group_body, 0)
        lax.fori_loop(groups * SUBLANES, count, row_body, 0)

    def start_gather(t, s):
        base = tb_ref[t]

        def start(i):
            pair = perm_ref[base + i]
            gather_copy(s, i, jnp.where(pair >= n_tok, pair - n_tok, pair)).start()

        for_rows(tv_ref[t], start)

    def start_scatter(t, s):
        base = tb_ref[t]
        for_rows(tv_ref[t], lambda i: scatter_copy(s, i, perm_ref[base + i]).start())

    def wait_rows(count, block_copy, row_copy):
        bulk = pl.multiple_of(lax.shift_right_logical(count, 3) * SUBLANES, SUBLANES)

        @pl.when(bulk > 0)
        def _():
            block_copy(pl.ds(0, bulk)).wait()

        def row_body(i, c):
            row_copy(i).wait()
            return c

        lax.fori_loop(bulk, count, row_body, 0)

    def wait_gather(t, s):
        wait_rows(tv_ref[t],
                  lambda rows: pltpu.make_async_copy(x_hbm.at[rows, :], xbuf.at[s, rows, :], gsem.at[s]),
                  lambda i: gather_copy(s, i, 0))

    def wait_scatter(t, s):
        wait_rows(tv_ref[t],
                  lambda rows: pltpu.make_async_copy(obuf.at[s, rows, :], y_hbm.at[rows, :], ssem.at[s]),
                  lambda i: scatter_copy(s, i, 0))

    @pl.when(j == 0)
    def _():
        xbuf[...] = jnp.zeros_like(xbuf)
        start_gather(0, 0)

    @pl.when(j + 1 < n_act)
    def _():
        start_gather(j + 1, 1 - slot)

    @pl.when(j < n_act)
    def _():
        wait_gather(j, slot)

        @pl.when(j >= 2)
        def _():
            wait_scatter(j - 2, slot)

        @pl.when(jnp.logical_or(j == 0, te_ref[j] != te_ref[jnp.maximum(j - 1, 0)]))
        def _():
            wgb[...] = wg_ref[0].astype(BF16)
            wub[...] = wu_ref[0].astype(BF16)
            wdb[...] = wd_ref[0].astype(BF16)

        x = xbuf[slot].astype(BF16)
        gate = jnp.dot(x, wgb[...], preferred_element_type=F32)
        up = jnp.dot(x, wub[...], preferred_element_type=F32)
        hid = (gate * jax.nn.sigmoid(gate) * up).astype(BF16)
        obuf[slot] = jnp.dot(hid, wdb[...], preferred_element_type=F32)
        start_scatter(j, slot)

        @pl.when(j == n_act - 1)
        def _():
            @pl.when(j >= 1)
            def _():
                wait_scatter(j - 1, 1 - slot)

            wait_scatter(j, slot)


def _expert_tables(route, n_experts, tile):
    n_tok = route.shape[0]
    i32 = jnp.int32
    e_flat = jnp.concatenate([route[:, ROUTE_E1], route[:, ROUTE_E2]]).astype(i32)
    n_pad = (1 << (2 * n_tok - 1).bit_length()) - 2 * n_tok
    keys = jnp.concatenate([e_flat, jnp.full((n_pad,), n_experts, i32)])
    perm = jnp.argsort(keys, stable=True)[:2 * n_tok].astype(i32)
    eids = jnp.arange(n_experts, dtype=i32)
    counts = jnp.sum((e_flat[:, None] == eids[None, :]).astype(i32), axis=0)
    tiles_e = (counts + (tile - 1)) // tile
    before = eids[None, :] < eids[:, None]
    pair_start = jnp.sum(jnp.where(before, counts[None, :], 0), axis=1)
    tile_start = jnp.sum(jnp.where(before, tiles_e[None, :], 0), axis=1)
    n_act = jnp.sum(tiles_e)
    t_max = (2 * n_tok) // tile + n_experts
    j = jnp.arange(t_max, dtype=i32)
    owner = jnp.logical_and(j[:, None] >= tile_start[None, :], j[:, None] < (tile_start + tiles_e)[None, :])
    pick = lambda table: jnp.sum(jnp.where(owner, table[None, :], 0), axis=1)
    active = j < n_act
    k = j - pick(tile_start)
    last_e = jnp.max(jnp.where(tiles_e > 0, eids, 0))
    te = jnp.where(active, pick(eids), last_e).astype(i32)
    tb = jnp.where(active, pick(pair_start) + k * tile, 0).astype(i32)
    tv = jnp.where(active, jnp.clip(pick(counts) - k * tile, 0, tile), 0).astype(i32)
    return te, tb, tv, n_act.reshape(1).astype(i32), perm, t_max


def _experts(x1, route, w_gate, w_up, w_down):
    n_tok, dm = x1.shape
    n_experts, _, de = w_gate.shape
    tile = 256
    te, tb, tv, n_act, perm, t_max = _expert_tables(route, n_experts, tile)
    w_in_spec = pl.BlockSpec((1, dm, de), lambda j, te, tb, tv, na, perm: (te[j], 0, 0))
    w_out_spec = pl.BlockSpec((1, de, dm), lambda j, te, tb, tv, na, perm: (te[j], 0, 0))
    grid_spec = pltpu.PrefetchScalarGridSpec(
        num_scalar_prefetch=5,
        grid=(t_max,),
        in_specs=[pl.BlockSpec(memory_space=pl.ANY), w_in_spec, w_in_spec, w_out_spec],
        out_specs=pl.BlockSpec(memory_space=pl.ANY),
        scratch_shapes=[
            pltpu.VMEM((2, tile, dm), F32), pltpu.VMEM((2, tile, dm), F32),
            pltpu.VMEM((dm, de), BF16), pltpu.VMEM((dm, de), BF16), pltpu.VMEM((de, dm), BF16),
            pltpu.SemaphoreType.DMA((2,)), pltpu.SemaphoreType.DMA((2,)),
        ],
    )
    return pl.pallas_call(
        functools.partial(_expert_kernel, n_tok=n_tok),
        grid_spec=grid_spec,
        out_shape=jax.ShapeDtypeStruct((2 * n_tok, dm), F32),
        compiler_params=pltpu.CompilerParams(dimension_semantics=("arbitrary",), vmem_limit_bytes=48 * MIB),
        name="moe_experts",
    )(te, tb, tv, n_act, perm, x1, w_gate, w_up, w_down)


def _combine_kernel(x_ref, ya_ref, yb_ref, r_ref, g_ref, b_ref, first_ref, second_ref, *, alpha, first_blocks):
    r = r_ref[...]
    moe = r[:, ROUTE_W1:ROUTE_W1 + 1] * ya_ref[...] + r[:, ROUTE_W2:ROUTE_W2 + 1] * yb_ref[...]
    out = _layer_norm(alpha * x_ref[...] + moe, g_ref[...], b_ref[...])
    i = pl.program_id(0)

    @pl.when(i < first_blocks)
    def _():
        first_ref[...] = out

    @pl.when(i >= first_blocks)
    def _():
        second_ref[...] = out


def _combine(x1, y_pairs, route, ln_g, ln_b, *, alpha, n_first):
    n, dm = x1.shape
    tm = _tile(math.gcd(n_first, n - n_first), 256)
    nb, nb_first = n // tm, n_first // tm
    const = lambda shape: pl.BlockSpec(shape, lambda i: (0, 0))
    return pl.pallas_call(
        functools.partial(_combine_kernel, alpha=alpha, first_blocks=nb_first),
        grid=(nb,),
        in_specs=[
            pl.BlockSpec((tm, dm), lambda i: (i, 0)),
            pl.BlockSpec((tm, dm), lambda i: (i, 0)),
            pl.BlockSpec((tm, dm), lambda i: (nb + i, 0)),
            pl.BlockSpec((tm, LANES), lambda i: (i, 0)),
            const((1, dm)), const((1, dm)),
        ],
        out_specs=[pl.BlockSpec((tm, dm), lambda i: (jnp.minimum(i, nb_first - 1), 0)),
                   pl.BlockSpec((tm, dm), lambda i: (jnp.maximum(i - nb_first, 0), 0))],
        out_shape=[jax.ShapeDtypeStruct((n_first, dm), F32), jax.ShapeDtypeStruct((n - n_first, dm), F32)],
        compiler_params=pltpu.CompilerParams(dimension_semantics=("arbitrary",), vmem_limit_bytes=48 * MIB),
        name="moe_combine_ln2",
    )(x1, y_pairs, y_pairs, route, ln_g, ln_b)


def _mixer(x, conv_state, c0, n0, m0, past_k, past_v, p, *, chunk_pref, total_rows, row0, prev):
    batch, seq, dm = x.shape
    n_tok = batch * seq
    xf = x.reshape(n_tok, dm)
    n_heads, d = p["n_heads"], p["head_dim"]
    da = n_heads * d
    cols = p["cols"]

    proj, q_s, k_hm, k_hm16, v_hm, v_hm16, gates = _project(
        xf, p["w_all"], p["w_gate"], a_cols=p["a_cols"], sb_cols=p["sb_cols"], b_cols=p["b_cols"], tn=p["tn"])

    chunk = _tile(seq, chunk_pref)
    h_a, c_new, n_new, m_new = _mlstm(
        proj, gates, p["b_gate_row"], p["conv_w"], p["conv_b"], conv_state, c0, n0, m0,
        batch=batch, seq=seq, chunk=chunk, n_heads=n_heads, head_dim=d,
        q_col=cols["q"], k_col=cols["k"], v_col=cols["v"])

    if past_k is None:
        assert batch == 1
        h_s = _sb_prompt(q_s, k_hm16, v_hm16, seq=seq)
    else:
        h_s = _sb_decode(q_s, k_hm16, v_hm16, past_k, past_v, batch=batch, seq=seq)

    x1, route = _merge(xf, h_a, proj, h_s, p["w_bm"], p["w_bs"], p["w_out"], p["ln1_g"], p["ln1_b"],
                       p["w_router"], p["b_router"],
                       o_col=cols["o"], gm_col=cols["gm"], gs_col=cols["gs"],
                       alpha=p["alpha"], n_experts=p["n_experts"], n_groups=p["n_groups"],
                       total_rows=total_rows, row0=row0, prev=prev)

    n_sb = k_hm.shape[0]
    sb_k = k_hm.reshape(n_sb, batch, seq, -1).transpose(1, 0, 2, 3)
    sb_v = v_hm.reshape(n_sb, batch, seq, -1).transpose(1, 0, 2, 3)
    new_conv = proj.reshape(batch, seq, -1)[:, seq - (CONV_W - 1):, cols["q"]:cols["q"] + 2 * da]
    return x1, route, (sb_k, sb_v, c_new, n_new, m_new, new_conv)


def _prepare_layer(l, depth, w_in, b_gate, conv_w, conv_b, w_branch_m, w_branch_s, w_out, ln1_g, ln1_b,
                   w_router_group, b_router_group, w_router_expert, b_router_expert,
                   w_exp_gate, w_exp_up, w_exp_down, ln2_g, ln2_b, head_dim):
    dm = w_in.shape[1]
    da2 = conv_w.shape[-1]
    da = da2 // 2
    n_heads = b_gate.shape[-1] // 2
    ds = w_branch_s.shape[1]
    sizes = (da2, da, da, 2 * n_heads, ds, ds, ds, dm, dm)
    offs = [0]
    for s in sizes:
        offs.append(offs[-1] + s)
    w = w_in[l]
    part = lambda i: w[:, offs[i]:offs[i + 1]]
    a_cols, b_cols = da2 + 2 * da, 2 * dm
    tn = math.gcd(math.gcd(math.gcd(1024, a_cols), b_cols), ds)
    w_all, w_gate = _repack_weight(w, a_cols=a_cols, n_gate=2 * n_heads, tn=tn)
    cols = {"q": 0, "k": da, "v": da2, "o": da2 + da, "gm": da2 + 2 * da, "gs": da2 + 2 * da + dm}
    assert cols["k"] % da == 0 and cols["o"] % da == 0 and cols["gm"] % dm == 0 and cols["gs"] % dm == 0
    b_gate_row = jnp.pad(b_gate[l].astype(F32), (0, LANES - 2 * n_heads)).reshape(1, LANES)
    n_groups = w_router_group.shape[-1]
    n_experts = w_router_expert.shape[-1]
    pad_r = LANES - n_experts - n_groups
    w_router = jnp.pad(jnp.concatenate([w_router_expert[l], w_router_group[l]], axis=1),
                       ((0, 0), (0, pad_r))).astype(F32)
    w_router_hi = w_router.astype(BF16)
    w_router_lo = (w_router - w_router_hi.astype(F32)).astype(BF16)
    b_router = jnp.pad(jnp.concatenate([b_router_expert[l], b_router_group[l]]), (0, pad_r)).reshape(1, LANES)
    return {
        "n_heads": n_heads, "head_dim": head_dim, "cols": cols, "n_groups": n_groups, "n_experts": n_experts,
        "alpha": (2.0 * depth) ** 0.25, "a_cols": a_cols, "b_cols": b_cols, "sb_cols": ds, "tn": tn,
        "w_all": w_all, "w_gate": w_gate, "b_gate_row": b_gate_row,
        "conv_w": conv_w[l], "conv_b": conv_b[l],
        "w_bm": w_branch_m[l].astype(BF16), "w_bs": w_branch_s[l].astype(BF16), "w_out": w_out[l].astype(BF16),
        "ln1_g": ln1_g[l].reshape(1, dm), "ln1_b": ln1_b[l].reshape(1, dm),
        "w_router": jnp.concatenate([w_router_hi, w_router_lo], axis=1), "b_router": b_router.astype(F32),
        "w_eg": w_exp_gate[l], "w_eu": w_exp_up[l], "w_ed": w_exp_down[l],
        "ln2_g": ln2_g[l].reshape(1, dm), "ln2_b": ln2_b[l].reshape(1, dm),
    }


def kernel(x_prompt, x_sample, cache_sb_k, cache_sb_v, state_mlstm_C, state_mlstm_n, state_mlstm_m, state_conv,
           w_in, b_gate, conv_w, conv_b, w_branch_m, w_branch_s, w_out, ln1_g, ln1_b,
           w_router_group, b_router_group, w_router_expert, b_router_expert,
           w_exp_gate, w_exp_up, w_exp_down, ln2_g, ln2_b):
    depth = w_in.shape[0]
    bp = x_prompt.shape[0]
    n_heads, head_dim = state_mlstm_C.shape[2], state_mlstm_C.shape[3]
    da2 = conv_w.shape[-1]
    xp, xs = x_prompt, x_sample
    st_p_all, st_s_all = [], []
    for l in range(depth):
        p = _prepare_layer(l, depth, w_in, b_gate, conv_w, conv_b, w_branch_m, w_branch_s, w_out, ln1_g, ln1_b,
                           w_router_group, b_router_group, w_router_expert, b_router_expert,
                           w_exp_gate, w_exp_up, w_exp_down, ln2_g, ln2_b, head_dim)
        n_p, n_s = xp.shape[0] * xp.shape[1], xs.shape[0] * xs.shape[1]
        x1, route, st_p = _mixer(
            xp, jnp.zeros((bp, CONV_W - 1, da2), F32), jnp.zeros((bp, n_heads, head_dim, head_dim), F32),
            jnp.zeros((bp, n_heads, head_dim), F32), jnp.full((bp, n_heads), NEG, F32), None, None, p,
            chunk_pref=256, total_rows=n_p + n_s, row0=0, prev=None)
        x1, route, st_s = _mixer(xs, state_conv[l], state_mlstm_C[l], state_mlstm_n[l], state_mlstm_m[l],
                                 cache_sb_k[l], cache_sb_v[l], p, chunk_pref=256,
                                 total_rows=n_p + n_s, row0=n_p, prev=(x1, route))
        y_pairs = _experts(x1, route, p["w_eg"], p["w_eu"], p["w_ed"])
        xp2, xs2 = _combine(x1, y_pairs, route, p["ln2_g"], p["ln2_b"], alpha=p["alpha"], n_first=n_p)
        xp, xs = xp2.reshape(xp.shape), xs2.reshape(xs.shape)
        st_p_all.append(st_p)
        st_s_all.append(st_s)
    stack = lambda states, i: jnp.stack([s[i] for s in states], axis=0)
    return (xp, xs,
            *(stack(st_p_all, i) for i in range(6)),
            *(stack(st_s_all, i) for i in range(6)))
```

```python
import functools
import math

import jax
import jax.numpy as jnp
from jax import lax
from jax.experimental import pallas as pl
from jax.experimental.pallas import tpu as pltpu

F32 = jnp.float32
BF16 = jnp.bfloat16

LANES = 128
SUBLANES = 8
LN_EPS = 1e-5
NEG = -1e30
CONV_W = 4
EXP_ZERO_BELOW = -104.0
MIB = 1024 * 1024

_NT = (((1,), (1,)), ((), ()))
_TN = (((0,), (0,)), ((), ()))


def _tile(n, pref):
    if n <= pref:
        return n
    t = pref
    while t >= SUBLANES:
        if n % t == 0 and t % SUBLANES == 0:
            return t
        t -= SUBLANES
    return n


def _log_sigmoid(x):
    return jnp.minimum(x, 0.0) - jnp.log1p(jnp.exp(-jnp.abs(x)))


def _layer_norm(x, g, b):
    mu = jnp.mean(x, axis=-1, keepdims=True)
    d = x - mu
    var = jnp.mean(d * d, axis=-1, keepdims=True)
    return d * lax.rsqrt(var + LN_EPS) * g + b


def _bf16_pieces(x, n):
    pieces = []
    rem = x
    for _ in range(n - 1):
        p = rem.astype(BF16)
        pieces.append(p)
        rem = rem - p.astype(F32)
    pieces.append(rem.astype(BF16))
    return pieces


def _dot_pieces(pieces, other, *, pieces_first, dims=None):
    total = None
    for p in pieces:
        a, b = (p, other) if pieces_first else (other, p)
        if dims is None:
            t = jnp.dot(a, b, preferred_element_type=F32)
        else:
            t = lax.dot_general(a, b, dims, preferred_element_type=F32)
        total = t if total is None else total + t
    return total


def _proj_kernel(x_ref, w_ref, wg_ref, main_ref, qs_ref, k32_ref, k16_ref, v32_ref, v16_ref, g_ref, xb_ref, *, ends):
    j = pl.program_id(1)
    end_main, end_qs, end_k = ends

    @pl.when(j == 0)
    def _():
        xb_ref[...] = x_ref[...].astype(BF16)
        g_ref[...] = jnp.dot(xb_ref[...], wg_ref[...], preferred_element_type=F32)

    acc = jnp.dot(xb_ref[...], w_ref[...], preferred_element_type=F32)

    def store_heads(f32_ref, bf16_ref):
        for h in range(f32_ref.shape[0]):
            piece = acc[:, h * LANES:(h + 1) * LANES]
            f32_ref[h] = piece
            bf16_ref[h] = piece.astype(BF16)

    @pl.when(j < end_main)
    def _():
        main_ref[...] = acc

    @pl.when(jnp.logical_and(j >= end_main, j < end_qs))
    def _():
        qs_ref[...] = acc.astype(BF16)

    @pl.when(jnp.logical_and(j >= end_qs, j < end_k))
    def _():
        store_heads(k32_ref, k16_ref)

    @pl.when(j >= end_k)
    def _():
        store_heads(v32_ref, v16_ref)


def _project(x, w_all, w_gate, *, main_cols, sb_cols, tn):
    n, k = x.shape
    tm = _tile(n, 1024)
    t_main, t_sb = main_cols // tn, sb_cols // tn
    ends = (t_main, t_main + t_sb, t_main + 2 * t_sb)
    n_tiles = t_main + 3 * t_sb
    assert w_all.shape[1] == n_tiles * tn
    local = lambda j, start, count: jnp.clip(j - start, 0, count - 1)
    heads = tn // LANES
    n_sb = sb_cols // LANES
    once = pl.Buffered(1)
    hm_spec = lambda start: pl.BlockSpec((heads, tm, LANES), lambda i, j: (local(j, start, t_sb), i, 0),
                                         pipeline_mode=once)
    return pl.pallas_call(
        functools.partial(_proj_kernel, ends=ends),
        grid=(n // tm, n_tiles),
        in_specs=[pl.BlockSpec((tm, k), lambda i, j: (i, 0), pipeline_mode=once),
                  pl.BlockSpec((k, tn), lambda i, j: (0, j)),
                  pl.BlockSpec((k, LANES), lambda i, j: (0, 0), pipeline_mode=once)],
        out_specs=[
            pl.BlockSpec((tm, tn), lambda i, j: (i, local(j, 0, t_main))),
            pl.BlockSpec((tm, tn), lambda i, j: (i, local(j, ends[0], t_sb)), pipeline_mode=once),
            hm_spec(ends[1]), hm_spec(ends[1]), hm_spec(ends[2]), hm_spec(ends[2]),
            pl.BlockSpec((tm, LANES), lambda i, j: (i, 0), pipeline_mode=once),
        ],
        out_shape=[
            jax.ShapeDtypeStruct((n, main_cols), F32),
            jax.ShapeDtypeStruct((n, sb_cols), BF16),
            jax.ShapeDtypeStruct((n_sb, n, LANES), F32), jax.ShapeDtypeStruct((n_sb, n, LANES), BF16),
            jax.ShapeDtypeStruct((n_sb, n, LANES), F32), jax.ShapeDtypeStruct((n_sb, n, LANES), BF16),
            jax.ShapeDtypeStruct((n, LANES), F32),
        ],
        scratch_shapes=[pltpu.VMEM((tm, k), BF16)],
        compiler_params=pltpu.CompilerParams(
            dimension_semantics=("parallel", "arbitrary"),
            vmem_limit_bytes=56 * MIB),
        name="in_proj",
    )(x, w_all, w_gate)


def _mlstm_kernel(q_ref, k_ref, v_ref, g_ref, bg_ref, cwq_ref, cwk_ref, cbq_ref, cbk_ref,
                  csq_ref, csk_ref, c0_ref, n0_ref, m0_ref,
                  h_ref, c_ref, n_ref, m_ref, qext_ref, kext_ref, *, chunk, n_heads, head_dim):
    step = pl.program_id(1)
    L, d = chunk, head_dim
    halo = CONV_W - 1
    base = SUBLANES - halo

    @pl.when(step == 0)
    def _():
        qext_ref[base:SUBLANES, :] = csq_ref[0]
        kext_ref[base:SUBLANES, :] = csk_ref[0]
        c_ref[...] = c0_ref[...]
        n_ref[...] = n0_ref[...]
        m_ref[...] = m0_ref[...]

    def conv_silu(raw_ref, ext_ref, cw_ref, cb_ref):
        ext_ref[SUBLANES:SUBLANES + L, :] = raw_ref[...]
        acc = cb_ref[...] + ext_ref[base:base + L, :] * cw_ref[0:1, :]
        for j in range(1, CONV_W):
            acc = acc + ext_ref[base + j:base + j + L, :] * cw_ref[j:j + 1, :]
        ext_ref[base:SUBLANES, :] = ext_ref[base + L:SUBLANES + L, :]
        return acc * jax.nn.sigmoid(acc)

    q_all = conv_silu(q_ref, qext_ref, cwq_ref, cbq_ref)
    k_all = conv_silu(k_ref, kext_ref, cwk_ref, cbk_ref) * (float(d) ** -0.5)

    lane = lax.broadcasted_iota(jnp.int32, (L, LANES), 1)
    pre = g_ref[...] + bg_ref[...]
    gates = jnp.where(lane < n_heads, pre, _log_sigmoid(pre))
    row_i = lax.broadcasted_iota(jnp.int32, (L, L), 0)
    col_i = lax.broadcasted_iota(jnp.int32, (L, L), 1)
    causal = col_i <= row_i
    tril = causal.astype(BF16)
    triu = (row_i <= col_i).astype(BF16)
    sel = (lax.broadcasted_iota(jnp.int32, (SUBLANES, LANES), 0)
           == lax.broadcasted_iota(jnp.int32, (SUBLANES, LANES), 1)).astype(BF16)
    gate_pieces = _bf16_pieces(gates, 3)
    cum_cols = _dot_pieces(gate_pieces, tril, pieces_first=False)
    gates_t = _dot_pieces(gate_pieces, sel, pieces_first=False, dims=_NT)
    cum_rows = _dot_pieces(_bf16_pieces(gates_t, 3), triu, pieces_first=True)

    for h in range(n_heads):
        cols = slice(h * d, (h + 1) * d)
        q = q_all[:, cols]
        k = k_all[:, cols]
        q_b = q.astype(BF16)
        k_b = k.astype(BF16)
        v_b = v_ref[:, cols].astype(BF16)
        i_col = gates[:, h:h + 1]
        b_col = cum_cols[:, n_heads + h:n_heads + h + 1]
        i_row = gates_t[h:h + 1, :]
        b_row = cum_rows[n_heads + h:n_heads + h + 1, :]
        c_prev = c_ref[0, h]
        n_prev = n_ref[0, h]
        m_prev = m_ref[0, h][:, 0:1]

        dmat = jnp.where(causal, b_col - b_row + i_row, NEG)
        inter = b_col + m_prev
        m_t = jnp.maximum(jnp.max(dmat, axis=-1, keepdims=True), inter)
        w = jnp.exp(dmat - m_t)
        s = lax.dot_general(q_b, k_b, _NT, preferred_element_type=F32) * w
        e_inter = jnp.exp(inter - m_t)
        num = (jnp.dot(s.astype(BF16), v_b, preferred_element_type=F32)
               + e_inter * jnp.dot(q_b, c_prev.astype(BF16), preferred_element_type=F32))
        den = jnp.sum(s, axis=-1, keepdims=True) + e_inter * jnp.sum(q * n_prev, axis=-1, keepdims=True)
        h_ref[:, cols] = num / jnp.maximum(jnp.abs(den), jnp.exp(-m_t))

        g_last = b_col[L - 1:L, :]
        w_last = g_last - b_col + i_col
        m_new = jnp.maximum(g_last + m_prev, jnp.max(w_last, axis=0, keepdims=True))
        decay = jnp.exp(g_last + m_prev - m_new)
        kw = jnp.exp(w_last - m_new) * k
        c_ref[0, h] = decay * c_prev + lax.dot_general(kw.astype(BF16), v_b, _TN, preferred_element_type=F32)
        n_ref[0, h] = decay * n_prev + jnp.sum(kw, axis=0, keepdims=True)
        m_ref[0, h] = jnp.broadcast_to(m_new, (1, LANES))


def _mlstm(proj, gates, b_gate_row, conv_w, conv_b, conv_state, c0, n0, m0, *, batch, seq, chunk,
           n_heads, head_dim, q_col, k_col, v_col):
    n_tok = batch * seq
    nc = seq // chunk
    d = head_dim
    da = n_heads * d
    qb, kb, vb = q_col // da, k_col // da, v_col // da
    kcw = (k_col - q_col) // da
    row = lambda b, c: b * nc + c
    n0r = n0.reshape(batch, n_heads, 1, d)
    m0r = jnp.broadcast_to(m0.reshape(batch, n_heads, 1, 1), (batch, n_heads, 1, LANES))
    cbr = conv_b.reshape(1, -1)
    state_spec = lambda shape: pl.BlockSpec(shape, lambda b, c: (b, 0, 0, 0))
    h, c_new, n_new, m_new = pl.pallas_call(
        functools.partial(_mlstm_kernel, chunk=chunk, n_heads=n_heads, head_dim=d),
        grid=(batch, nc),
        in_specs=[
            pl.BlockSpec((chunk, da), lambda b, c: (row(b, c), qb)),
            pl.BlockSpec((chunk, da), lambda b, c: (row(b, c), kb)),
            pl.BlockSpec((chunk, da), lambda b, c: (row(b, c), vb)),
            pl.BlockSpec((chunk, LANES), lambda b, c: (row(b, c), 0)),
            pl.BlockSpec((1, LANES), lambda b, c: (0, 0)),
            pl.BlockSpec((CONV_W, da), lambda b, c: (0, 0)),
            pl.BlockSpec((CONV_W, da), lambda b, c: (0, kcw)),
            pl.BlockSpec((1, da), lambda b, c: (0, 0)),
            pl.BlockSpec((1, da), lambda b, c: (0, kcw)),
            pl.BlockSpec((1, CONV_W - 1, da), lambda b, c: (b, 0, 0)),
            pl.BlockSpec((1, CONV_W - 1, da), lambda b, c: (b, 0, kcw)),
            state_spec((1, n_heads, d, d)),
            state_spec((1, n_heads, 1, d)),
            state_spec((1, n_heads, 1, LANES)),
        ],
        out_specs=[
            pl.BlockSpec((chunk, da), lambda b, c: (row(b, c), 0)),
            state_spec((1, n_heads, d, d)),
            state_spec((1, n_heads, 1, d)),
            state_spec((1, n_heads, 1, LANES)),
        ],
        out_shape=[
            jax.ShapeDtypeStruct((n_tok, da), F32),
            jax.ShapeDtypeStruct((batch, n_heads, d, d), F32),
            jax.ShapeDtypeStruct((batch, n_heads, 1, d), F32),
            jax.ShapeDtypeStruct((batch, n_heads, 1, LANES), F32),
        ],
        scratch_shapes=[pltpu.VMEM((chunk + SUBLANES, da), F32), pltpu.VMEM((chunk + SUBLANES, da), F32)],
        compiler_params=pltpu.CompilerParams(
            dimension_semantics=("parallel", "arbitrary"),
            vmem_limit_bytes=32 * MIB),
        name="mlstm",
    )(proj, proj, proj, gates, b_gate_row, conv_w, conv_w, cbr, cbr, conv_state, conv_state, c0, n0r, m0r)
    return h, c_new, n_new.reshape(batch, n_heads, d), m_new[:, :, 0, 0]


def _sb_scores(q_b, k_b, *, scale, diagonal):
    tq, tk = q_b.shape[0], k_b.shape[0]
    z = lax.dot_general(q_b, k_b, _NT, preferred_element_type=F32) * scale
    softplus = jnp.maximum(z, 0.0) + jnp.log(1.0 + jnp.exp(-jnp.abs(z)))
    log_beta = z - softplus
    drop, mask = softplus, None
    if diagonal:
        mask = (lax.broadcasted_iota(jnp.int32, (tq, tk), 1) < lax.broadcasted_iota(jnp.int32, (tq, tk), 0))
        drop = jnp.where(mask, softplus, 0.0)
    later_mat = (lax.broadcasted_iota(jnp.int32, (tk, tk), 0)
                 > lax.broadcasted_iota(jnp.int32, (tk, tk), 1)).astype(BF16)
    later = _dot_pieces(_bf16_pieces(drop, 2), later_mat, pieces_first=True)
    total = later[:, 0:1] + drop[:, 0:1]
    return log_beta, later, total, mask


def _sb_weights(scores, v_b, carry, acc):
    log_beta, later, total, mask = scores
    a = jnp.exp(log_beta - (later + carry))
    if mask is not None:
        a = jnp.where(mask, a, 0.0)
    return carry + total, acc + jnp.dot(a.astype(BF16), v_b, preferred_element_type=F32)


def _sb_block(q_b, k_b, v_b, carry, acc, *, scale, diagonal):
    return _sb_weights(_sb_scores(q_b, k_b, scale=scale, diagonal=diagonal), v_b, carry, acc)


def _sb_sweep(qs, k_ats, v_ats, first_block, carries, accs, *, tk, scale):
    def carry_min(cs):
        m = cs[0]
        for c in cs[1:]:
            m = jnp.minimum(m, c)
        return jnp.min(m)

    def cond(state):
        j, cmin, _, _ = state
        return jnp.logical_and(j >= 0, cmin < -EXP_ZERO_BELOW)

    def body(state):
        j, _, cs, acs = state
        start = pl.multiple_of(j * tk, tk)
        new = [_sb_block(q, k_at[pl.ds(start, tk), :].astype(BF16), v_at[pl.ds(start, tk), :].astype(BF16),
                         c, a, scale=scale, diagonal=False)
               for q, k_at, v_at, c, a in zip(qs, k_ats, v_ats, cs, acs)]
        cs = tuple(n[0] for n in new)
        return j - 1, carry_min(cs), cs, tuple(n[1] for n in new)

    carries = tuple(carries)
    _, _, _, accs = lax.while_loop(cond, body, (first_block, carry_min(carries), carries, tuple(accs)))
    return accs


def _sb_prompt_kernel(q_ref, k_ref, v_ref, o_ref, *, tq, scale):
    i = pl.program_id(1)
    n_heads, _, d = k_ref.shape
    start = pl.multiple_of(i * tq, tq)
    qs = [q_ref[:, h * d:(h + 1) * d] for h in range(n_heads)]
    zero_carry, zero_acc = jnp.zeros((tq, 1), F32), jnp.zeros((tq, d), F32)

    def finish(states, first_block):
        accs = _sb_sweep(qs, [k_ref.at[h] for h in range(n_heads)], [v_ref.at[h] for h in range(n_heads)],
                         first_block, [s[0] for s in states], [s[1] for s in states], tk=tq, scale=scale)
        for h in range(n_heads):
            o_ref[:, h * d:(h + 1) * d] = accs[h].astype(o_ref.dtype)

    @pl.when(i == 0)
    def _():
        finish([_sb_block(qs[h], k_ref[h, pl.ds(start, tq), :], v_ref[h, pl.ds(start, tq), :],
                          zero_carry, zero_acc, scale=scale, diagonal=True) for h in range(n_heads)], i - 1)

    @pl.when(i > 0)
    def _():
        prev = pl.multiple_of(start - tq, tq)
        states = []
        for h in range(n_heads):
            diag = _sb_scores(qs[h], k_ref[h, pl.ds(start, tq), :], scale=scale, diagonal=True)
            full = _sb_scores(qs[h], k_ref[h, pl.ds(prev, tq), :], scale=scale, diagonal=False)
            carry, acc = _sb_weights(diag, v_ref[h, pl.ds(start, tq), :], zero_carry, zero_acc)
            states.append(_sb_weights(full, v_ref[h, pl.ds(prev, tq), :], carry, acc))
        finish(states, i - 2)


def _sb_prompt(q, k_hm, v_hm, *, seq, heads_per_step=2):
    n_heads, _, d = k_hm.shape
    hb = heads_per_step if n_heads % heads_per_step == 0 else 1
    tq = _tile(seq, 256)
    kv_spec = pl.BlockSpec((hb, seq, d), lambda g, i: (g, 0, 0))
    return pl.pallas_call(
        functools.partial(_sb_prompt_kernel, tq=tq, scale=float(d) ** -0.5),
        grid=(n_heads // hb, seq // tq),
        in_specs=[pl.BlockSpec((tq, hb * d), lambda g, i: (i, g)), kv_spec, kv_spec],
        out_specs=pl.BlockSpec((tq, hb * d), lambda g, i: (i, g)),
        out_shape=jax.ShapeDtypeStruct((seq, n_heads * d), BF16),
        compiler_params=pltpu.CompilerParams(
            dimension_semantics=("parallel", "arbitrary"),
            vmem_limit_bytes=48 * MIB),
        name="sb_prompt",
    )(q, k_hm, v_hm)


def _sb_decode_kernel(q_ref, kn_ref, vn_ref, kp_ref, vp_ref, o_ref, *, tk, scale):
    n_heads, tq, d = kn_ref.shape
    qs, carries, accs = [], [], []
    for h in range(n_heads):
        q_b = q_ref[:, h * d:(h + 1) * d]
        carry, acc = _sb_block(q_b, kn_ref[h], vn_ref[h], jnp.zeros((tq, 1), F32), jnp.zeros((tq, d), F32),
                               scale=scale, diagonal=True)
        qs.append(q_b)
        carries.append(carry)
        accs.append(acc)
    past_len = kp_ref.shape[2]
    accs = _sb_sweep(qs, [kp_ref.at[0, h] for h in range(n_heads)], [vp_ref.at[0, h] for h in range(n_heads)],
                     past_len // tk - 1, carries, accs, tk=tk, scale=scale)
    for h in range(n_heads):
        o_ref[:, h * d:(h + 1) * d] = accs[h].astype(o_ref.dtype)


def _sb_decode(q, k_hm, v_hm, past_k, past_v, *, batch, seq, heads_per_step=4):
    n_heads, _, d = k_hm.shape
    hb = heads_per_step if n_heads % heads_per_step == 0 else 1
    past_len = past_k.shape[2]
    tk = _tile(past_len, 256)
    new_spec = pl.BlockSpec((hb, seq, d), lambda b, g: (g, b, 0))
    past_spec = pl.BlockSpec((1, hb, past_len, d), lambda b, g: (b, g, 0, 0))
    return pl.pallas_call(
        functools.partial(_sb_decode_kernel, tk=tk, scale=float(d) ** -0.5),
        grid=(batch, n_heads // hb),
        in_specs=[pl.BlockSpec((seq, hb * d), lambda b, g: (b, g)), new_spec, new_spec, past_spec, past_spec],
        out_specs=pl.BlockSpec((seq, hb * d), lambda b, g: (b, g)),
        out_shape=jax.ShapeDtypeStruct((batch * seq, n_heads * d), BF16),
        compiler_params=pltpu.CompilerParams(dimension_semantics=("parallel", "parallel")),
        name="sb_decode",
    )(q, k_hm, v_hm, past_k, past_v)


ROUTE_E1, ROUTE_E2, ROUTE_W1, ROUTE_W2 = 0, 1, 2, 3


def _route(logits, *, n_experts, n_groups):
    per_group = n_experts // n_groups
    lane = lax.broadcasted_iota(jnp.int32, logits.shape, 1)
    big = jnp.int32(LANES)
    is_group = jnp.logical_and(lane >= n_experts, lane < n_experts + n_groups)
    lg = jnp.where(is_group, logits, -jnp.inf)
    lg_max = jnp.max(lg, axis=-1, keepdims=True)
    p_sel = 1.0 / jnp.sum(jnp.exp(lg - lg_max), axis=-1, keepdims=True)
    grp = jnp.min(jnp.where(lg == lg_max, lane - n_experts, big), axis=-1, keepdims=True)
    in_grp = jnp.logical_and(lane >= grp * per_group, lane < (grp + 1) * per_group)
    le = jnp.where(in_grp, logits, -jnp.inf)
    v1 = jnp.max(le, axis=-1, keepdims=True)
    i1 = jnp.min(jnp.where(le == v1, lane, big), axis=-1, keepdims=True)
    le2 = jnp.where(lane == i1, -jnp.inf, le)
    v2 = jnp.max(le2, axis=-1, keepdims=True)
    i2 = jnp.min(jnp.where(le2 == v2, lane, big), axis=-1, keepdims=True)
    e2 = jnp.exp(v2 - v1)
    w1 = p_sel / (1.0 + e2)
    w2 = p_sel * e2 / (1.0 + e2)
    rec = jnp.where(lane == ROUTE_E1, i1.astype(F32), 0.0)
    rec = jnp.where(lane == ROUTE_E2, i2.astype(F32), rec)
    rec = jnp.where(lane == ROUTE_W1, w1, rec)
    return jnp.where(lane == ROUTE_W2, w2, rec)


def _merge_kernel(x_ref, ha_ref, om_ref, hs_ref, gm_ref, gs_ref, wbm_ref, wbs_ref, wout_ref, g_ref, b_ref,
                  wr_ref, br_ref, *refs, alpha, n_experts, n_groups, own_blocks):
    o_ref, route_ref = refs[-2:]
    i = pl.program_id(0)

    @pl.when(i < own_blocks)
    def _():
        h_m = (jax.nn.sigmoid(om_ref[...]) * ha_ref[...]).astype(BF16)
        t_m = jnp.dot(h_m, wbm_ref[...], preferred_element_type=F32)
        t_s = jnp.dot(hs_ref[...], wbs_ref[...], preferred_element_type=F32)
        merged = jax.nn.sigmoid(gm_ref[...]) * t_m + jax.nn.sigmoid(gs_ref[...]) * t_s
        y = jnp.dot(merged.astype(BF16), wout_ref[...], preferred_element_type=F32)
        x1 = _layer_norm(alpha * x_ref[...] + y, g_ref[...], b_ref[...])
        o_ref[...] = x1
        x_hi, x_lo = _bf16_pieces(x1, 2)
        hi_both = jnp.dot(x_hi, wr_ref[...], preferred_element_type=F32)
        lo_hi = jnp.dot(x_lo, wr_ref[:, :LANES], preferred_element_type=F32)
        logits = (hi_both[:, :LANES] + (hi_both[:, LANES:] + lo_hi)) + br_ref[...]
        route_ref[...] = _route(logits, n_experts=n_experts, n_groups=n_groups)

    @pl.when(i >= own_blocks)
    def _():
        o_ref[...] = jnp.zeros_like(o_ref)
        route_ref[...] = jnp.zeros_like(route_ref)


def _merge(x, h_a, proj, h_s, w_bm, w_bs, w_out, ln_g, ln_b, w_router, b_router, *,
           o_col, gm_col, gs_col, alpha, n_experts, n_groups, total_rows, row0, prev):
    n, dm = x.shape
    da, ds = h_a.shape[1], h_s.shape[1]
    tm = _tile(math.gcd(n, total_rows - n) if total_rows > n else n, 256)
    assert row0 % tm == 0 and total_rows % tm == 0 and (prev is not None or row0 == 0)
    blk0 = row0 // tm
    own = n // tm
    steps = own if prev is not None else total_rows // tm
    const = lambda shape: pl.BlockSpec(shape, lambda i: (0, 0), pipeline_mode=pl.Buffered(1))
    rows = lambda i: jnp.minimum(i, own - 1)
    prev = () if prev is None else tuple(prev)
    n_in = 13
    return pl.pallas_call(
        functools.partial(_merge_kernel, alpha=alpha, n_experts=n_experts, n_groups=n_groups, own_blocks=own),
        grid=(steps,),
        in_specs=[
            pl.BlockSpec((tm, dm), lambda i: (rows(i), 0)),
            pl.BlockSpec((tm, da), lambda i: (rows(i), 0)),
            pl.BlockSpec((tm, da), lambda i: (rows(i), o_col // da)),
            pl.BlockSpec((tm, ds), lambda i: (rows(i), 0)),
            pl.BlockSpec((tm, dm), lambda i: (rows(i), gm_col // dm)),
            pl.BlockSpec((tm, dm), lambda i: (rows(i), gs_col // dm)),
            const((da, dm)), const((ds, dm)), const((dm, dm)), const((1, dm)), const((1, dm)),
            const((dm, 2 * LANES)), const((1, LANES)),
        ] + [pl.BlockSpec(memory_space=pl.ANY)] * len(prev),
        out_specs=[pl.BlockSpec((tm, dm), lambda i: (blk0 + i, 0)),
                   pl.BlockSpec((tm, LANES), lambda i: (blk0 + i, 0))],
        out_shape=[jax.ShapeDtypeStruct((total_rows, dm), F32), jax.ShapeDtypeStruct((total_rows, LANES), F32)],
        input_output_aliases={n_in + k: k for k in range(len(prev))},
        compiler_params=pltpu.CompilerParams(dimension_semantics=("parallel",), vmem_limit_bytes=48 * MIB),
        name="merge_out_ln1",
    )(x, h_a, proj, h_s, proj, proj, w_bm, w_bs, w_out, ln_g, ln_b, w_router, b_router, *prev)


def _expert_kernel(te_ref, tb_ref, tv_ref, na_ref, perm_ref,
                   x_hbm, wg_ref, wu_ref, wd_ref, y_hbm,
                   xbuf, obuf, wgb, wub, wdb, gsem, ssem, *, n_tok):
    j = pl.program_id(0)
    n_act = na_ref[0]
    slot = lax.rem(j, 2)

    def gather_copy(s, i, token):
        return pltpu.make_async_copy(x_hbm.at[pl.ds(token, 1), :], xbuf.at[s, pl.ds(i, 1), :], gsem.at[s])

    def scatter_copy(s, i, row):
        return pltpu.make_async_copy(obuf.at[s, pl.ds(i, 1), :], y_hbm.at[pl.ds(row, 1), :], ssem.at[s])

    def for_rows(count, fn):
        groups = lax.shift_right_logical(count, 3)

        def group_body(g, c):
            for u in range(SUBLANES):
                fn(g * SUBLANES + u, u % 2)
            return c

        def row_body(i, c):
            fn(i, 0)
            return c

        lax.fori_loop(0, groups, group_body, 0)
        lax.fori_loop(groups * SUBLANES, count, row_body, 0)

    def start_gather(t, s):
        base = tb_ref[t]

        def start(i, priority):
            pair = perm_ref[base + i]
            gather_copy(s, i, jnp.where(pair >= n_tok, pair - n_tok, pair)).start(priority=priority)

        for_rows(tv_ref[t], start)

    def start_scatter(t, s):
        base = tb_ref[t]
        for_rows(tv_ref[t], lambda i, priority: scatter_copy(s, i, perm_ref[base + i]).start(priority=priority))

    def wait_rows(count, block_copy, row_copy):
        bulk = pl.multiple_of(lax.shift_right_logical(count, 3) * SUBLANES, SUBLANES)

        @pl.when(bulk > 0)
        def _():
            block_copy(pl.ds(0, bulk)).wait()

        def row_body(i, c):
            row_copy(i).wait()
            return c

        lax.fori_loop(bulk, count, row_body, 0)

    def wait_gather(t, s):
        wait_rows(tv_ref[t],
                  lambda rows: pltpu.make_async_copy(x_hbm.at[rows, :], xbuf.at[s, rows, :], gsem.at[s]),
                  lambda i: gather_copy(s, i, 0))

    def wait_scatter(t, s):
        wait_rows(tv_ref[t],
                  lambda rows: pltpu.make_async_copy(obuf.at[s, rows, :], y_hbm.at[rows, :], ssem.at[s]),
                  lambda i: scatter_copy(s, i, 0))

    @pl.when(j == 0)
    def _():
        xbuf[...] = jnp.zeros_like(xbuf)
        start_gather(0, 0)

    @pl.when(j + 1 < n_act)
    def _():
        start_gather(j + 1, 1 - slot)

    @pl.when(j < n_act)
    def _():
        wait_gather(j, slot)

        @pl.when(j >= 2)
        def _():
            wait_scatter(j - 2, slot)

        @pl.when(jnp.logical_or(j == 0, te_ref[j] != te_ref[jnp.maximum(j - 1, 0)]))
        def _():
            wgb[...] = wg_ref[0].astype(BF16)
            wub[...] = wu_ref[0].astype(BF16)
            wdb[...] = wd_ref[0].astype(BF16)

        x = xbuf[slot].astype(BF16)
        gate = jnp.dot(x, wgb[...], preferred_element_type=F32)
        up = jnp.dot(x, wub[...], preferred_element_type=F32)
        hid = (gate * jax.nn.sigmoid(gate) * up).astype(BF16)
        obuf[slot] = jnp.dot(hid, wdb[...], preferred_element_type=F32)
        start_scatter(j, slot)

        @pl.when(j == n_act - 1)
        def _():
            @pl.when(j >= 1)
            def _():
                wait_scatter(j - 1, 1 - slot)

            wait_scatter(j, slot)


def _expert_tables(route, n_experts, tile):
    n_tok = route.shape[0]
    i32 = jnp.int32
    e_flat = jnp.concatenate([route[:, ROUTE_E1], route[:, ROUTE_E2]]).astype(i32)
    n_pad = (1 << (2 * n_tok - 1).bit_length()) - 2 * n_tok
    keys = jnp.concatenate([e_flat, jnp.full((n_pad,), n_experts, i32)])
    perm = jnp.argsort(keys, stable=True)[:2 * n_tok].astype(i32)
    eids = jnp.arange(n_experts, dtype=i32)
    counts = jnp.sum((e_flat[:, None] == eids[None, :]).astype(i32), axis=0)
    tiles_e = (counts + (tile - 1)) // tile
    before = eids[None, :] < eids[:, None]
    pair_start = jnp.sum(jnp.where(before, counts[None, :], 0), axis=1)
    tile_start = jnp.sum(jnp.where(before, tiles_e[None, :], 0), axis=1)
    n_act = jnp.sum(tiles_e)
    t_max = (2 * n_tok) // tile + n_experts
    j = jnp.arange(t_max, dtype=i32)
    owner = jnp.logical_and(j[:, None] >= tile_start[None, :], j[:, None] < (tile_start + tiles_e)[None, :])
    pick = lambda table: jnp.sum(jnp.where(owner, table[None, :], 0), axis=1)
    active = j < n_act
    k = j - pick(tile_start)
    last_e = jnp.max(jnp.where(tiles_e > 0, eids, 0))
    te = jnp.where(active, pick(eids), last_e).astype(i32)
    tb = jnp.where(active, pick(pair_start) + k * tile, 0).astype(i32)
    tv = jnp.where(active, jnp.clip(pick(counts) - k * tile, 0, tile), 0).astype(i32)
    return te, tb, tv, n_act.reshape(1).astype(i32), perm, t_max


def _experts(x1, route, w_gate, w_up, w_down):
    n_tok, dm = x1.shape
    n_experts, _, de = w_gate.shape
    tile = 256
    te, tb, tv, n_act, perm, t_max = _expert_tables(route, n_experts, tile)
    w_in_spec = pl.BlockSpec((1, dm, de), lambda j, te, tb, tv, na, perm: (te[j], 0, 0))
    w_out_spec = pl.BlockSpec((1, de, dm), lambda j, te, tb, tv, na, perm: (te[j], 0, 0))
    grid_spec = pltpu.PrefetchScalarGridSpec(
        num_scalar_prefetch=5,
        grid=(t_max,),
        in_specs=[pl.BlockSpec(memory_space=pl.ANY), w_in_spec, w_in_spec, w_out_spec],
        out_specs=pl.BlockSpec(memory_space=pl.ANY),
        scratch_shapes=[
            pltpu.VMEM((2, tile, dm), F32), pltpu.VMEM((2, tile, dm), F32),
            pltpu.VMEM((dm, de), BF16), pltpu.VMEM((dm, de), BF16), pltpu.VMEM((de, dm), BF16),
            pltpu.SemaphoreType.DMA((2,)), pltpu.SemaphoreType.DMA((2,)),
        ],
    )
    return pl.pallas_call(
        functools.partial(_expert_kernel, n_tok=n_tok),
        grid_spec=grid_spec,
        out_shape=jax.ShapeDtypeStruct((2 * n_tok, dm), F32),
        compiler_params=pltpu.CompilerParams(dimension_semantics=("arbitrary",), vmem_limit_bytes=48 * MIB),
        name="moe_experts",
    )(te, tb, tv, n_act, perm, x1, w_gate, w_up, w_down)


def _combine_kernel(x_ref, ya_ref, yb_ref, r_ref, g_ref, b_ref, first_ref, second_ref, *, alpha, first_blocks):
    r = r_ref[...]
    moe = r[:, ROUTE_W1:ROUTE_W1 + 1] * ya_ref[...] + r[:, ROUTE_W2:ROUTE_W2 + 1] * yb_ref[...]
    out = _layer_norm(alpha * x_ref[...] + moe, g_ref[...], b_ref[...])
    i = pl.program_id(0)

    @pl.when(i < first_blocks)
    def _():
        first_ref[...] = out

    @pl.when(i >= first_blocks)
    def _():
        second_ref[...] = out


def _combine(x1, y_pairs, route, ln_g, ln_b, *, alpha, n_first):
    n, dm = x1.shape
    tm = _tile(math.gcd(n_first, n - n_first), 256)
    nb, nb_first = n // tm, n_first // tm
    const = lambda shape: pl.BlockSpec(shape, lambda i: (0, 0))
    return pl.pallas_call(
        functools.partial(_combine_kernel, alpha=alpha, first_blocks=nb_first),
        grid=(nb,),
        in_specs=[
            pl.BlockSpec((tm, dm), lambda i: (i, 0)),
            pl.BlockSpec((tm, dm), lambda i: (i, 0)),
            pl.BlockSpec((tm, dm), lambda i: (nb + i, 0)),
            pl.BlockSpec((tm, LANES), lambda i: (i, 0)),
            const((1, dm)), const((1, dm)),
        ],
        out_specs=[pl.BlockSpec((tm, dm), lambda i: (jnp.minimum(i, nb_first - 1), 0)),
                   pl.BlockSpec((tm, dm), lambda i: (jnp.maximum(i - nb_first, 0), 0))],
        out_shape=[jax.ShapeDtypeStruct((n_first, dm), F32), jax.ShapeDtypeStruct((n - n_first, dm), F32)],
        compiler_params=pltpu.CompilerParams(dimension_semantics=("arbitrary",), vmem_limit_bytes=48 * MIB),
        name="moe_combine_ln2",
    )(x1, y_pairs, y_pairs, route, ln_g, ln_b)


def _mixer(x, conv_state, c0, n0, m0, past_k, past_v, p, *, chunk_pref, total_rows, row0, prev):
    batch, seq, dm = x.shape
    n_tok = batch * seq
    xf = x.reshape(n_tok, dm)
    n_heads, d = p["n_heads"], p["head_dim"]
    da = n_heads * d
    cols = p["cols"]

    proj, q_s, k_hm, k_hm16, v_hm, v_hm16, gates = _project(
        xf, p["w_all"], p["w_gate"], main_cols=p["main_cols"], sb_cols=p["sb_cols"], tn=p["tn"])

    chunk = _tile(seq, chunk_pref)
    h_a, c_new, n_new, m_new = _mlstm(
        proj, gates, p["b_gate_row"], p["conv_w"], p["conv_b"], conv_state, c0, n0, m0,
        batch=batch, seq=seq, chunk=chunk, n_heads=n_heads, head_dim=d,
        q_col=cols["q"], k_col=cols["k"], v_col=cols["v"])

    if past_k is None:
        assert batch == 1
        h_s = _sb_prompt(q_s, k_hm16, v_hm16, seq=seq)
    else:
        h_s = _sb_decode(q_s, k_hm16, v_hm16, past_k, past_v, batch=batch, seq=seq)

    x1, route = _merge(xf, h_a, proj, h_s, p["w_bm"], p["w_bs"], p["w_out"], p["ln1_g"], p["ln1_b"],
                       p["w_router"], p["b_router"],
                       o_col=cols["o"], gm_col=cols["gm"], gs_col=cols["gs"],
                       alpha=p["alpha"], n_experts=p["n_experts"], n_groups=p["n_groups"],
                       total_rows=total_rows, row0=row0, prev=prev)

    n_sb = k_hm.shape[0]
    sb_k = k_hm.reshape(n_sb, batch, seq, -1).transpose(1, 0, 2, 3)
    sb_v = v_hm.reshape(n_sb, batch, seq, -1).transpose(1, 0, 2, 3)
    new_conv = proj.reshape(batch, seq, -1)[:, seq - (CONV_W - 1):, cols["q"]:cols["q"] + 2 * da]
    return x1, route, (sb_k, sb_v, c_new, n_new, m_new, new_conv)


def _prepare_layer(l, depth, w_in, b_gate, conv_w, conv_b, w_branch_m, w_branch_s, w_out, ln1_g, ln1_b,
                   w_router_group, b_router_group, w_router_expert, b_router_expert,
                   w_exp_gate, w_exp_up, w_exp_down, ln2_g, ln2_b, head_dim):
    dm = w_in.shape[1]
    da2 = conv_w.shape[-1]
    da = da2 // 2
    n_heads = b_gate.shape[-1] // 2
    ds = w_branch_s.shape[1]
    sizes = (da2, da, da, 2 * n_heads, ds, ds, ds, dm, dm)
    offs = [0]
    for s in sizes:
        offs.append(offs[-1] + s)
    w = w_in[l]
    part = lambda i: w[:, offs[i]:offs[i + 1]]
    main_cols = da2 + 2 * da + 2 * dm
    tn = math.gcd(math.gcd(1024, main_cols), ds)
    w_all = jnp.concatenate([part(i).astype(BF16) for i in (0, 1, 2, 7, 8, 4, 5, 6)], axis=1)
    w_gate = jnp.pad(part(3).astype(BF16), ((0, 0), (0, LANES - 2 * n_heads)))
    cols = {"q": 0, "k": da, "v": da2, "o": da2 + da, "gm": da2 + 2 * da, "gs": da2 + 2 * da + dm}
    assert cols["k"] % da == 0 and cols["o"] % da == 0 and cols["gm"] % dm == 0 and cols["gs"] % dm == 0
    b_gate_row = jnp.pad(b_gate[l].astype(F32), (0, LANES - 2 * n_heads)).reshape(1, LANES)
    n_groups = w_router_group.shape[-1]
    n_experts = w_router_expert.shape[-1]
    pad_r = LANES - n_experts - n_groups
    w_router = jnp.pad(jnp.concatenate([w_router_expert[l], w_router_group[l]], axis=1),
                       ((0, 0), (0, pad_r))).astype(F32)
    w_router_hi = w_router.astype(BF16)
    w_router_lo = (w_router - w_router_hi.astype(F32)).astype(BF16)
    b_router = jnp.pad(jnp.concatenate([b_router_expert[l], b_router_group[l]]), (0, pad_r)).reshape(1, LANES)
    return {
        "n_heads": n_heads, "head_dim": head_dim, "cols": cols, "n_groups": n_groups, "n_experts": n_experts,
        "alpha": (2.0 * depth) ** 0.25, "main_cols": main_cols, "sb_cols": ds, "tn": tn,
        "w_all": w_all, "w_gate": w_gate, "b_gate_row": b_gate_row,
        "conv_w": conv_w[l], "conv_b": conv_b[l],
        "w_bm": w_branch_m[l].astype(BF16), "w_bs": w_branch_s[l].astype(BF16), "w_out": w_out[l].astype(BF16),
        "ln1_g": ln1_g[l].reshape(1, dm), "ln1_b": ln1_b[l].reshape(1, dm),
        "w_router": jnp.concatenate([w_router_hi, w_router_lo], axis=1), "b_router": b_router.astype(F32),
        "w_eg": w_exp_gate[l], "w_eu": w_exp_up[l], "w_ed": w_exp_down[l],
        "ln2_g": ln2_g[l].reshape(1, dm), "ln2_b": ln2_b[l].reshape(1, dm),
    }


def kernel(x_prompt, x_sample, cache_sb_k, cache_sb_v, state_mlstm_C, state_mlstm_n, state_mlstm_m, state_conv,
           w_in, b_gate, conv_w, conv_b, w_branch_m, w_branch_s, w_out, ln1_g, ln1_b,
           w_router_group, b_router_group, w_router_expert, b_router_expert,
           w_exp_gate, w_exp_up, w_exp_down, ln2_g, ln2_b):
    depth = w_in.shape[0]
    bp = x_prompt.shape[0]
    n_heads, head_dim = state_mlstm_C.shape[2], state_mlstm_C.shape[3]
    da2 = conv_w.shape[-1]
    xp, xs = x_prompt, x_sample
    st_p_all, st_s_all = [], []
    for l in range(depth):
        p = _prepare_layer(l, depth, w_in, b_gate, conv_w, conv_b, w_branch_m, w_branch_s, w_out, ln1_g, ln1_b,
                           w_router_group, b_router_group, w_router_expert, b_router_expert,
                           w_exp_gate, w_exp_up, w_exp_down, ln2_g, ln2_b, head_dim)
        n_p, n_s = xp.shape[0] * xp.shape[1], xs.shape[0] * xs.shape[1]
        x1, route, st_p = _mixer(
            xp, jnp.zeros((bp, CONV_W - 1, da2), F32), jnp.zeros((bp, n_heads, head_dim, head_dim), F32),
            jnp.zeros((bp, n_heads, head_dim), F32), jnp.full((bp, n_heads), NEG, F32), None, None, p,
            chunk_pref=256, total_rows=n_p + n_s, row0=0, prev=None)
        x1, route, st_s = _mixer(xs, state_conv[l], state_mlstm_C[l], state_mlstm_n[l], state_mlstm_m[l],
                                 cache_sb_k[l], cache_sb_v[l], p, chunk_pref=256,
                                 total_rows=n_p + n_s, row0=n_p, prev=(x1, route))
        y_pairs = _experts(x1, route, p["w_eg"], p["w_eu"], p["w_ed"])
        xp2, xs2 = _combine(x1, y_pairs, route, p["ln2_g"], p["ln2_b"], alpha=p["alpha"], n_first=n_p)
        xp, xs = xp2.reshape(xp.shape), xs2.reshape(xs.shape)
        st_p_all.append(st_p)
        st_s_all.append(st_s)
    stack = lambda states, i: jnp.stack([s[i] for s in states], axis=0)
    return (xp, xs,
            *(stack(st_p_all, i) for i in range(6)),
            *(stack(st_s_all, i) for i in range(6)))
```
